```python
import jax
import jax.numpy as jnp
from jax import lax
import numpy as np

D_MODEL = 4096
BATCH = 4
SEQ = 4096
DEPTH = 2
DEC_BATCH = 8
DEC_SEQ = 2048
PAST_LEN = 128

PLE_DIM = 256
CHUNK = 128
D_GMLP = D_MODEL // 2
GMLP_HEAD = 128
N_GMLP_HEADS = D_GMLP // GMLP_HEAD
D_RWKV = D_MODEL - D_GMLP
RWKV_HEAD = 64
N_RWKV_HEADS = D_RWKV // RWKV_HEAD
DECAY_LORA = 96
AAA_LORA = 96
MV_LORA = 64
GATE_LORA = 256
CONV_W = 3
RWKV_SIZES = (D_RWKV, D_RWKV, D_RWKV, DECAY_LORA, DECAY_LORA, AAA_LORA, AAA_LORA, GATE_LORA)
C_RWKV0 = sum(RWKV_SIZES)
C_RWKV = C_RWKV0 + MV_LORA
C_IN0 = 2 * D_GMLP + C_RWKV0
C_IN = 2 * D_GMLP + C_RWKV
D_FF = 7 * D_MODEL // 2
D_FF_E = D_MODEL // 2
N_EXPERTS = 8
TOP_K = 2
N_DENSE = (DEPTH + 1) // 2
N_MOE = DEPTH // 2
ALPHA = (2 * DEPTH) ** 0.25
BETA = (8 * DEPTH) ** -0.25
LN_EPS = 1e-5
GN_EPS = 64e-5
L2_EPS = 1e-12

kernel_name = 'hymba_gmlp_rwkv7_deepnorm_encoder'


def _layernorm(x, g, b, eps=LN_EPS):
    xf = x.astype(jnp.float32)
    mu = jnp.mean(xf, -1, keepdims=True)
    var = jnp.mean(jnp.square(xf - mu), -1, keepdims=True)
    y = (xf - mu) * lax.rsqrt(var + eps)
    return (y * g.astype(jnp.float32) + b.astype(jnp.float32)).astype(x.dtype)


def _split_cols(z, sizes):
    idx = [int(i) for i in np.cumsum(sizes)[:-1]]
    return jnp.split(z, idx, axis=-1)


def _centered_conv(z, w):
    half = CONV_W // 2
    s = z.shape[1]
    zp = jnp.pad(z, ((0, 0), (half, half), (0, 0)))
    return sum(zp[:, j:j + s] * w[j] for j in range(CONV_W))


def _spatial_gating(u, v, ln_g, ln_b, w_s, b_s):
    u = jax.nn.gelu(u)
    v = _layernorm(jax.nn.gelu(v), ln_g, ln_b)
    bsz, s, _ = v.shape
    vc = v.reshape(bsz, s // CHUNK, CHUNK, N_GMLP_HEADS, GMLP_HEAD)
    sv = jnp.einsum('hpq,bcqhd->bcphd', w_s, vc) + b_s.T[None, None, :, :, None]
    return u * sv.reshape(bsz, s, D_GMLP)


def _heads(t):
    return t.reshape(t.shape[0], t.shape[1], N_RWKV_HEADS, RWKV_HEAD)


def _wkv_scan(r, decay, k, v, kk, b, reverse):
    bsz, _, h, n = r.shape
    xs = tuple(jnp.moveaxis(t, 1, 0) for t in (r, decay, k, v, kk, b))

    def step(state, inp):
        r_t, w_t, k_t, v_t, kk_t, b_t = inp
        sa = jnp.einsum('bhij,bhj->bhi', state, kk_t)
        new = (state * w_t[:, :, None, :] - sa[..., None] * b_t[:, :, None, :]
               + v_t[..., None] * k_t[:, :, None, :])
        y = jnp.einsum('bhij,bhj->bhi', state if reverse else new, r_t)
        return new, y

    s0 = jnp.zeros((bsz, h, n, n), jnp.float32)
    _, ys = lax.scan(step, s0, xs, reverse=reverse)
    return jnp.moveaxis(ys, 0, 1)


def _rwkv_mix(zr, conv_w, w0, w2, a0, a2, g2, k_k, k_a, r_k, gn_g, gn_b, v_first, v0, v2):
    f32 = jnp.float32
    bsz, s, _ = zr.shape
    zr = _centered_conv(zr, conv_w).astype(f32)
    sizes = RWKV_SIZES if v0 is None else RWKV_SIZES + (MV_LORA,)
    parts = _split_cols(zr, sizes)
    r, k, v, wd_f, wd_b, ad_f, ad_b, gd = parts[:8]
    if v0 is None:
        v_first = v
    else:
        v = v + (v_first - v) * jax.nn.sigmoid(v0 + parts[8] @ v2)
    g = jax.nn.sigmoid(gd) @ g2
    kk = _heads(k * k_k)
    kk = kk * lax.rsqrt(jnp.sum(kk * kk, -1, keepdims=True) + L2_EPS)
    r_h, v_h = _heads(r), _heads(v)
    ys, k_dir = [], []
    for d, (wd, ad) in enumerate(((wd_f, ad_f), (wd_b, ad_b))):
        w_log = -jax.nn.softplus(-(w0[d] + jnp.tanh(wd) @ w2[d])) - 0.5
        decay = jnp.exp(-jnp.exp(w_log))
        a = jax.nn.sigmoid(a0[d] + ad @ a2[d])
        k_d = _heads(k * (1.0 + (a - 1.0) * k_a))
        ys.append(_wkv_scan(r_h, _heads(decay), k_d, v_h, kk, kk * _heads(a), reverse=(d == 1)))
        k_dir.append(k_d)
    y = ys[0] + ys[1]
    mu = jnp.mean(y, -1, keepdims=True)
    var = jnp.mean(jnp.square(y - mu), -1, keepdims=True)
    y = ((y - mu) * lax.rsqrt(var + GN_EPS)).reshape(bsz, s, D_RWKV) * gn_g + gn_b
    bonus = (jnp.sum(r_h * k_dir[0] * r_k, -1, keepdims=True) * v_h).reshape(bsz, s, D_RWKV)
    return (y + bonus) * g, v_first


def _swiglu(x, wg, wu, wd):
    h = jax.nn.silu(jnp.einsum('bsd,df->bsf', x, wg)) * jnp.einsum('bsd,df->bsf', x, wu)
    return jnp.einsum('bsf,fd->bsd', h, wd)


def _moe(x, router, we_gate, we_up, we_down):
    logits = jnp.einsum('bsd,de->bse', x, router).astype(jnp.float32)
    top_vals, top_idx = lax.top_k(logits, TOP_K)
    top_w = jax.nn.softmax(top_vals, axis=-1)
    gates = jnp.sum(jax.nn.one_hot(top_idx, N_EXPERTS, dtype=jnp.float32) * top_w[..., None], axis=-2)
    out = jnp.zeros_like(x)
    for e in range(N_EXPERTS):
        ye = _swiglu(x, we_gate[e], we_up[e], we_down[e])
        out = out + gates[..., e:e + 1].astype(x.dtype) * ye
    return out


def _trunk(x, p, P):
    v_first = None
    for l in range(DEPTH):
        if l == 0:
            w_in_l, conv_l, v0_l, v2_l = P['w_in0'], P['conv0'], None, None
        else:
            w_in_l, conv_l, v0_l, v2_l = P['w_in'][l - 1], P['conv'][l - 1], P['v0'][l - 1], P['v2'][l - 1]
        z = jnp.einsum('bsd,dc->bsc', x, w_in_l)
        u, vg, zr = z[..., :D_GMLP], z[..., D_GMLP:2 * D_GMLP], z[..., 2 * D_GMLP:]
        y_g = _spatial_gating(u, vg, P['sgu_ln_g'][l], P['sgu_ln_b'][l], P['w_s'][l], P['b_s'][l])
        y_r, v_first = _rwkv_mix(zr, conv_l, P['w0'][l], P['w2'][l], P['a0'][l], P['a2'][l], P['g2'][l],
                                 P['k_k'][l], P['k_a'][l], P['r_k'][l], P['gn_g'][l], P['gn_b'][l],
                                 v_first, v0_l, v2_l)
        mixed = jnp.concatenate([y_g, y_r.astype(y_g.dtype)], axis=-1)
        mix = jnp.einsum('bsc,cd->bsd', mixed, P['w_out'][l])
        x = _layernorm(ALPHA * x + mix, P['ln1_g'][l], P['ln1_b'][l])
        j = l // 2
        if l % 2 == 0:
            ff = _swiglu(x, P['w_ff_gate'][j], P['w_ff_up'][j], P['w_ff_down'][j])
        else:
            ff = _moe(x, P['router'][j], P['we_gate'][j], P['we_up'][j], P['we_down'][j])
        x = _layernorm(ALPHA * x + ff, P['ln2_g'][l], P['ln2_b'][l])
        gate = jax.nn.sigmoid(jnp.einsum('bsd,de->bse', x, P['w_pgate'][l]))
        x = x + gate * jnp.einsum('bsk,kd->bsd', p[l], P['w_pproj'][l])
    return x


def setup_inputs(seed: int = 0) -> dict:
    key = jax.random.key(seed)
    ks = iter(jax.random.split(key, 48))
    f32 = jnp.float32

    def nrm(shape, scale):
        return jax.random.normal(next(ks), shape, f32) * scale

    def gain(shape):
        return 1.0 + nrm(shape, 0.05)

    conv_base = jnp.array([0.25, 0.5, 0.25], f32)[:, None]
    return {
        'x_prompt': nrm((BATCH, SEQ, D_MODEL), 1.0),
        'x_sample': nrm((DEC_BATCH, DEC_SEQ, D_MODEL), 1.0),
        'p_prompt': nrm((DEPTH, BATCH, SEQ, PLE_DIM), 1.0),
        'p_sample': nrm((DEPTH, DEC_BATCH, DEC_SEQ, PLE_DIM), 1.0),
        'w_in0': nrm((D_MODEL, C_IN0), D_MODEL ** -0.5),
        'conv0': conv_base + nrm((CONV_W, C_RWKV0), 0.05),
        'w_in': nrm((DEPTH - 1, D_MODEL, C_IN), D_MODEL ** -0.5),
        'conv': conv_base + nrm((DEPTH - 1, CONV_W, C_RWKV), 0.05),
        'sgu_ln_g': gain((DEPTH, D_GMLP)),
        'sgu_ln_b': nrm((DEPTH, D_GMLP), 0.02),
        'w_s': nrm((DEPTH, N_GMLP_HEADS, CHUNK, CHUNK), CHUNK ** -0.5),
        'b_s': gain((DEPTH, N_GMLP_HEADS, CHUNK)),
        'w0': jax.random.uniform(next(ks), (DEPTH, 2, D_RWKV), f32, -6.0, 0.0),
        'w2': nrm((DEPTH, 2, DECAY_LORA, D_RWKV), 0.5 * DECAY_LORA ** -0.5),
        'a0': nrm((DEPTH, 2, D_RWKV), 0.5),
        'a2': nrm((DEPTH, 2, AAA_LORA, D_RWKV), 0.5 * AAA_LORA ** -0.5),
        'g2': nrm((DEPTH, GATE_LORA, D_RWKV), GATE_LORA ** -0.5),
        'k_k': 0.85 + nrm((DEPTH, D_RWKV), 0.05),
        'k_a': gain((DEPTH, D_RWKV)),
        'r_k': nrm((DEPTH, N_RWKV_HEADS, RWKV_HEAD), 0.1),
        'gn_g': gain((DEPTH, D_RWKV)),
        'gn_b': nrm((DEPTH, D_RWKV), 0.02),
        'v0': nrm((DEPTH - 1, D_RWKV), 0.5),
        'v2': nrm((DEPTH - 1, MV_LORA, D_RWKV), 0.5 * MV_LORA ** -0.5),
        'w_out': nrm((DEPTH, D_MODEL, D_MODEL), BETA * D_MODEL ** -0.5),
        'ln1_g': gain((DEPTH, D_MODEL)),
        'ln1_b': nrm((DEPTH, D_MODEL), 0.02),
        'ln2_g': gain((DEPTH, D_MODEL)),
        'ln2_b': nrm((DEPTH, D_MODEL), 0.02),
        'w_ff_gate': nrm((N_DENSE, D_MODEL, D_FF), D_MODEL ** -0.5),
        'w_ff_up': nrm((N_DENSE, D_MODEL, D_FF), D_MODEL ** -0.5),
        'w_ff_down': nrm((N_DENSE, D_FF, D_MODEL), BETA * D_FF ** -0.5),
        'router': nrm((N_MOE, D_MODEL, N_EXPERTS), D_MODEL ** -0.5),
        'we_gate': nrm((N_MOE, N_EXPERTS, D_MODEL, D_FF_E), D_MODEL ** -0.5),
        'we_up': nrm((N_MOE, N_EXPERTS, D_MODEL, D_FF_E), D_MODEL ** -0.5),
        'we_down': nrm((N_MOE, N_EXPERTS, D_FF_E, D_MODEL), BETA * D_FF_E ** -0.5),
        'w_pproj': nrm((DEPTH, PLE_DIM, D_MODEL), BETA * PLE_DIM ** -0.5),
        'w_pgate': nrm((DEPTH, D_MODEL, D_MODEL), D_MODEL ** -0.5),
    }


def reference(x_prompt, x_sample, p_prompt, p_sample, w_in0, conv0, w_in, conv, sgu_ln_g, sgu_ln_b,
              w_s, b_s, w0, w2, a0, a2, g2, k_k, k_a, r_k, gn_g, gn_b, v0, v2, w_out,
              ln1_g, ln1_b, ln2_g, ln2_b, w_ff_gate, w_ff_up, w_ff_down, router, we_gate, we_up,
              we_down, w_pproj, w_pgate):
    P = dict(w_in0=w_in0, conv0=conv0, w_in=w_in, conv=conv, sgu_ln_g=sgu_ln_g, sgu_ln_b=sgu_ln_b,
             w_s=w_s, b_s=b_s, w0=w0, w2=w2, a0=a0, a2=a2, g2=g2, k_k=k_k, k_a=k_a, r_k=r_k,
             gn_g=gn_g, gn_b=gn_b, v0=v0, v2=v2, w_out=w_out, ln1_g=ln1_g, ln1_b=ln1_b,
             ln2_g=ln2_g, ln2_b=ln2_b, w_ff_gate=w_ff_gate, w_ff_up=w_ff_up, w_ff_down=w_ff_down,
             router=router, we_gate=we_gate, we_up=we_up, we_down=we_down,
             w_pproj=w_pproj, w_pgate=w_pgate)
    y_prompt = _trunk(x_prompt, p_prompt, P)
    y_sample = _trunk(x_sample, p_sample, P)
    return (y_prompt, y_sample)
```

```python
import functools

import jax
import jax.numpy as jnp
from jax import lax
from jax.experimental import pallas as pl
from jax.experimental.pallas import tpu as pltpu

F32 = jnp.float32
BF16 = jnp.bfloat16
HIGHEST = lax.Precision.HIGHEST

LANES = 128
SUBLANES = 8
VMEM_LIMIT_BYTES = 56 * 1024 * 1024

CHUNK = 128
GMLP_HEAD = 128
RWKV_HEAD = 64
DECAY_LORA = 96
AAA_LORA = 96
MV_LORA = 64
GATE_LORA = 256
N_EXPERTS = 8
LN_EPS = 1e-5
GN_EPS = 64e-5
L2_EPS = 1e-12

SCAN_L = 64
SCAN_W = 256
HEADS_PER_GROUP = SCAN_W // RWKV_HEAD
LORA_SLOT = 128
LORA_COLS = 1024


def _params(*sem):
    return pltpu.CompilerParams(dimension_semantics=sem, vmem_limit_bytes=VMEM_LIMIT_BYTES)


def _is_seq_start(pos, segs):
    (t1, s1), (_, s2) = segs
    return jnp.where(pos < t1, pos % s1 == 0, (pos - t1) % s2 == 0)


def _mm_kernel(x_ref, w_ref, o_ref, *scratch, nk):
    if nk == 1:
        o_ref[...] = jnp.dot(x_ref[...], w_ref[...], preferred_element_type=F32).astype(o_ref.dtype)
        return
    acc_ref, = scratch
    k = pl.program_id(2)

    @pl.when(k == 0)
    def _():
        acc_ref[...] = jnp.zeros_like(acc_ref)

    acc_ref[...] += jnp.dot(x_ref[...], w_ref[...], preferred_element_type=F32)

    @pl.when(k == nk - 1)
    def _():
        o_ref[...] = acc_ref[...].astype(o_ref.dtype)


def _matmul(x, w, out_dtype, tm=1024, tn=1024, tk=None):
    m, kdim = x.shape
    n = w.shape[1]
    tm, tn = min(tm, m), min(tn, n)
    tk = kdim if tk is None else min(tk, kdim)
    nk = kdim // tk
    assert m % tm == 0 and n % tn == 0 and kdim % tk == 0
    scratch = [] if nk == 1 else [pltpu.VMEM((tm, tn), F32)]
    return pl.pallas_call(
        functools.partial(_mm_kernel, nk=nk),
        grid=(m // tm, n // tn, nk),
        in_specs=[pl.BlockSpec((tm, tk), lambda i, j, k: (i, k)),
                  pl.BlockSpec((tk, tn), lambda i, j, k: (k, j))],
        out_specs=pl.BlockSpec((tm, tn), lambda i, j, k: (i, j)),
        out_shape=jax.ShapeDtypeStruct((m, n), out_dtype),
        scratch_shapes=scratch,
        compiler_params=_params("parallel", "parallel", "arbitrary"),
        name="matmul",
    )(x, w)


def _sgu_kernel(u_ref, v_ref, g_ref, b_ref, ws_ref, bs_ref, o_ref, *, n_chunks):
    gv = jax.nn.gelu(v_ref[...].astype(F32))
    mu = jnp.mean(gv, axis=-1, keepdims=True)
    cen = gv - mu
    var = jnp.mean(cen * cen, axis=-1, keepdims=True)
    vn = (cen * lax.rsqrt(var + LN_EPS) * g_ref[...] + b_ref[...]).astype(BF16)
    n_heads = ws_ref.shape[0]
    for c in range(n_chunks):
        rows = slice(c * CHUNK, (c + 1) * CHUNK)
        for h in range(n_heads):
            cols = slice(h * GMLP_HEAD, (h + 1) * GMLP_HEAD)
            sv = jnp.dot(ws_ref[h], vn[rows, cols], preferred_element_type=F32) + bs_ref[:, cols]
            gu = jax.nn.gelu(u_ref[rows, cols].astype(F32))
            o_ref[rows, cols] = (gu * sv).astype(o_ref.dtype)


def _spatial_gating(uv, ln_g, ln_b, w_s, b_s, tb=512):
    t = uv.shape[0]
    dg = uv.shape[1] // 2
    tb = min(tb, t)
    n_heads = w_s.shape[0]
    bs_full = jnp.repeat(b_s.T, GMLP_HEAD, axis=1)
    return pl.pallas_call(
        functools.partial(_sgu_kernel, n_chunks=tb // CHUNK),
        grid=(t // tb,),
        in_specs=[pl.BlockSpec((tb, dg), lambda i: (i, 0)),
                  pl.BlockSpec((tb, dg), lambda i: (i, 1)),
                  pl.BlockSpec((1, dg), lambda i: (0, 0)),
                  pl.BlockSpec((1, dg), lambda i: (0, 0)),
                  pl.BlockSpec((n_heads, CHUNK, CHUNK), lambda i: (0, 0, 0)),
                  pl.BlockSpec((CHUNK, dg), lambda i: (0, 0))],
        out_specs=pl.BlockSpec((tb, dg), lambda i: (i, 0)),
        out_shape=jax.ShapeDtypeStruct((t, dg), BF16),
        compiler_params=_params("parallel"),
        name="spatial_gating",
    )(uv, uv, ln_g.reshape(1, dg), ln_b.reshape(1, dg), w_s.astype(BF16), bs_full)


def _head_sum(x, ones_bd):
    parts = []
    for g in range(x.shape[1] // SCAN_W):
        xs = x[:, g * SCAN_W:(g + 1) * SCAN_W]
        hi = xs.astype(BF16)
        lo = (xs - hi.astype(F32)).astype(BF16)
        parts.append(jnp.dot(hi, ones_bd, preferred_element_type=F32)
                     + jnp.dot(lo, ones_bd, preferred_element_type=F32))
    return jnp.concatenate(parts, axis=1)


def _prep_kernel(*refs, layer1, tb, segs, dr):
    if layer1:
        (z_ref, zp_ref, zn_ref, conv_ref, w0_ref, w2_ref, a0_ref, a2_ref, g2_ref, kk_ref, ka_ref, rk_ref,
         ones_ref, vf_ref, v0_ref, v2_ref, r_o, v_o, kkn_o, g_o, bonus_o, lw_o, k_o, b_o) = refs
    else:
        (z_ref, zp_ref, zn_ref, conv_ref, w0_ref, w2_ref, a0_ref, a2_ref, g2_ref, kk_ref, ka_ref, rk_ref,
         ones_ref, r_o, v_o, kkn_o, g_o, bonus_o, lw_o, k_o, b_o) = refs
    pos0 = pl.program_id(0) * tb
    first = _is_seq_start(pos0, segs)
    last = _is_seq_start(pos0 + tb, segs)
    row = lax.broadcasted_iota(jnp.int32, (tb, 1), 0)

    def conv(c0, width):
        cols = slice(c0, c0 + width)
        z = z_ref[:, cols]
        prev_row = jnp.where(first, 0.0, zp_ref[SUBLANES - 1:SUBLANES, cols])
        next_row = jnp.where(last, 0.0, zn_ref[0:1, cols])
        zprev = jnp.where(row == 0, prev_row, pltpu.roll(z, 1, 0))
        znext = jnp.where(row == tb - 1, next_row, pltpu.roll(z, tb - 1, 0))
        return zprev * conv_ref[0:1, cols] + z * conv_ref[1:2, cols] + znext * conv_ref[2:3, cols]

    lo = conv(3 * dr, LORA_COLS)
    s = LORA_SLOT
    wd = (lo[:, 0:s], lo[:, s:2 * s])
    ad = (lo[:, 2 * s:3 * s], lo[:, 3 * s:4 * s])
    gd = lo[:, 4 * s:6 * s]

    def lora(x, w):
        return jnp.dot(x.astype(BF16), w, preferred_element_type=F32)

    g_o[...] = lora(jax.nn.sigmoid(gd), g2_ref[...])
    v = conv(2 * dr, dr)
    if layer1:
        mv = lo[:, 6 * s:7 * s]
        v = v + (vf_ref[...] - v) * jax.nn.sigmoid(v0_ref[...] + lora(mv, v2_ref[...]))
    v_o[...] = v
    r = conv(0, dr)
    r_o[...] = r
    k = conv(dr, dr)
    ones_bd = ones_ref[...]
    kk = k * kk_ref[...]
    kk = kk * lax.rsqrt(_head_sum(kk * kk, ones_bd) + L2_EPS)
    kkn_o[...] = kk
    for d in range(2):
        zw = w0_ref[d:d + 1, :] + lora(jnp.tanh(wd[d]), w2_ref[d])
        w_log = -(jnp.maximum(-zw, 0.0) + jnp.log(1.0 + jnp.exp(-jnp.abs(zw)))) - 0.5
        lw_o[d] = -jnp.exp(w_log)
        a = jax.nn.sigmoid(a0_ref[d:d + 1, :] + lora(ad[d], a2_ref[d]))
        k_d = k * (1.0 + (a - 1.0) * ka_ref[...])
        k_o[d] = k_d
        b_o[d] = kk * a
        if d == 0:
            bonus_o[...] = _head_sum(r * k_d * rk_ref[...], ones_bd) * v


def _rwkv_prep(zr, conv_w, w0, w2, a0, a2, g2, k_k, k_a, r_k, ones_bd, segs, v_first=None, v0=None, v2=None,
               tb=128):
    t, c = zr.shape
    dr = k_k.shape[-1]
    tb = min(tb, t)
    layer1 = v_first is not None
    hb = tb // SUBLANES
    n_halo = t // SUBLANES
    const2 = lambda i: (0, 0)
    const3 = lambda i: (0, 0, 0)
    in_specs = [pl.BlockSpec((tb, c), lambda i: (i, 0)),
                pl.BlockSpec((SUBLANES, c), lambda i: (jnp.maximum(i * hb - 1, 0), 0)),
                pl.BlockSpec((SUBLANES, c), lambda i: (jnp.minimum((i + 1) * hb, n_halo - 1), 0)),
                pl.BlockSpec((3, c), const2),
                pl.BlockSpec((2, dr), const2),
                pl.BlockSpec((2, LORA_SLOT, dr), const3),
                pl.BlockSpec((2, dr), const2),
                pl.BlockSpec((2, LORA_SLOT, dr), const3),
                pl.BlockSpec((GATE_LORA, dr), const2),
                pl.BlockSpec((1, dr), const2),
                pl.BlockSpec((1, dr), const2),
                pl.BlockSpec((1, dr), const2),
                pl.BlockSpec((SCAN_W, SCAN_W), const2)]
    args = [zr, zr, zr, conv_w, w0, w2, a0, a2, g2, k_k.reshape(1, dr), k_a.reshape(1, dr), r_k.reshape(1, dr),
            ones_bd]
    if layer1:
        in_specs += [pl.BlockSpec((tb, dr), lambda i: (i, 0)),
                     pl.BlockSpec((1, dr), const2),
                     pl.BlockSpec((LORA_SLOT, dr), const2)]
        args += [v_first, v0.reshape(1, dr), v2]
    tok = pl.BlockSpec((tb, dr), lambda i: (i, 0))
    tok2 = pl.BlockSpec((2, tb, dr), lambda i: (0, i, 0))
    one = jax.ShapeDtypeStruct((t, dr), F32)
    two = jax.ShapeDtypeStruct((2, t, dr), F32)
    return pl.pallas_call(
        functools.partial(_prep_kernel, layer1=layer1, tb=tb, segs=segs, dr=dr),
        grid=(t // tb,),
        in_specs=in_specs,
        out_specs=[tok, tok, tok, tok, tok, tok2, tok2, tok2],
        out_shape=[one, one, one, one, one, two, two, two],
        compiler_params=_params("parallel"),
        name="rwkv_prep",
    )(*args)


def _scan_kernel(r_ref, v_ref, kk_ref, lw_ref, k_ref, b_ref, y_ref, s_ref, *, tb, segs, nblk):
    L, W, G = SCAN_L, SCAN_W, HEADS_PER_GROUP
    d = pl.program_id(0)
    c = pl.program_id(2)
    blk = c + d * (nblk - 1 - 2 * c)
    pos = blk * tb
    fwd = d == 0
    start = jnp.where(fwd, _is_seq_start(pos, segs), _is_seq_start(pos + tb, segs))

    @pl.when(start)
    def _():
        s_ref[...] = jnp.zeros_like(s_ref)

    sgn = 1 - 2 * d
    t_i = lax.broadcasted_iota(jnp.int32, (L, L), 0)
    s_i = lax.broadcasted_iota(jnp.int32, (L, L), 1)
    tri = jnp.where(sgn * (t_i - s_i) >= 0, 1.0, 0.0).astype(F32)
    tc = lax.broadcasted_iota(jnp.int32, (L, G * L), 0)
    sc = lax.broadcasted_iota(jnp.int32, (L, G * L), 1) % L
    before = sgn * (tc - sc)
    strict = before > 0
    rmask = before >= d
    eye = jnp.where(sc == tc, 1.0, 0.0).astype(F32)
    d_f = d.astype(F32)
    bi = lax.broadcasted_iota(jnp.int32, (W, W), 0) // RWKV_HEAD
    bj = lax.broadcasted_iota(jnp.int32, (W, W), 1) // RWKV_HEAD
    bdmask = bi == bj

    def bd(x):
        return jnp.where(bdmask, jnp.tile(x, (G, 1)), jnp.zeros((), x.dtype))

    nt = (((1,), (1,)), ((), ()))
    tn = (((0,), (0,)), ((), ()))
    n_chunks = tb // L

    def chunk(j, carry):
        ci = jnp.where(fwd, j, n_chunks - 1 - j)
        rows = pl.ds(pl.multiple_of(ci * L, L), L)
        lw = lw_ref[rows, :]
        cum = jnp.dot(tri, lw, precision=HIGHEST, preferred_element_type=F32)
        tot = jnp.sum(lw, axis=0, keepdims=True)
        cum_ex = cum - lw
        e_ex = jnp.exp(cum_ex)
        e_neg = jnp.exp(-cum)
        e_rem = jnp.exp(tot - cum)
        kq = kk_ref[rows, :] * e_ex
        rq = r_ref[rows, :] * jnp.exp(cum - d_f * lw)
        k = k_ref[rows, :]
        b = b_ref[rows, :]
        vb = v_ref[rows, :].astype(BF16)
        q = jnp.concatenate([kq, rq], axis=0).astype(BF16)
        keys = jnp.concatenate([bd((k * e_neg).astype(BF16)), bd((b * e_neg).astype(BF16))], axis=0)
        a = lax.dot_general(q, keys, nt, preferred_element_type=F32)
        a_kk = jnp.where(strict, a[:L, :G * L], 0.0)
        a_kb = jnp.where(strict, a[:L, G * L:], 0.0)
        a_rk = jnp.where(rmask, a[L:, :G * L], 0.0)
        a_rb = jnp.where(rmask, a[L:, G * L:], 0.0)
        s0 = s_ref[...]
        qs = lax.dot_general(q, s0.astype(BF16), nt, preferred_element_type=F32)
        av = jnp.dot(jnp.concatenate([a_kk, a_rk], axis=0).astype(BF16), bd(vb), preferred_element_type=F32)
        rhs = qs[:L] + av[:L]
        p = -a_kb
        tinv = eye + p
        p = jnp.dot(p.astype(BF16), bd(p.astype(BF16)), preferred_element_type=F32)
        levels = L.bit_length() - 2
        for lvl in range(levels):
            wt = bd(p.astype(BF16))
            if lvl < levels - 1:
                tp = jnp.dot(jnp.concatenate([tinv, p], axis=0).astype(BF16), wt, preferred_element_type=F32)
                tinv = tinv + tp[:L]
                p = tp[L:]
            else:
                tinv = tinv + jnp.dot(tinv.astype(BF16), wt, preferred_element_type=F32)
        u = jnp.dot(tinv.astype(BF16), bd(rhs.astype(BF16)), preferred_element_type=F32)
        ub = u.astype(BF16)
        y_ref[rows, :] = qs[L:] + av[L:] - jnp.dot(a_rb.astype(BF16), bd(ub), preferred_element_type=F32)
        vu = jnp.concatenate([vb, -ub], axis=0)
        kb = jnp.concatenate([(k * e_rem).astype(BF16), (b * e_rem).astype(BF16)], axis=0)
        upd = lax.dot_general(vu, kb, tn, preferred_element_type=F32)
        s_ref[...] = s0 * jnp.exp(tot) + jnp.where(bdmask, upd, 0.0)
        return carry

    lax.fori_loop(0, n_chunks, chunk, 0)


def _rwkv_scan(r, v, kk, lw, k, b, segs, tb=256):
    t, dr = r.shape
    tb = min(tb, t)
    nblk = t // tb
    tok = lambda d, g, c: (c + d * (nblk - 1 - 2 * c), g)
    tok2 = lambda d, g, c: (d, c + d * (nblk - 1 - 2 * c), g)
    one = pl.BlockSpec((tb, SCAN_W), tok)
    two = pl.BlockSpec((None, tb, SCAN_W), tok2)
    return pl.pallas_call(
        functools.partial(_scan_kernel, tb=tb, segs=segs, nblk=nblk),
        grid=(2, dr // SCAN_W, nblk),
        in_specs=[one, one, one, two, two, two],
        out_specs=two,
        out_shape=jax.ShapeDtypeStruct((2, t, dr), F32),
        scratch_shapes=[pltpu.VMEM((SCAN_W, SCAN_W), F32)],
        compiler_params=_params("parallel", "parallel", "arbitrary"),
        name="rwkv_scan",
    )(r, v, kk, lw, k, b)


def _post_kernel(y_ref, g_ref, bonus_ref, gng_ref, gnb_ref, ones_ref, o_ref):
    y = y_ref[0] + y_ref[1]
    ones_bd = ones_ref[...]
    mu = _head_sum(y, ones_bd) * (1.0 / RWKV_HEAD)
    cen = y - mu
    var = _head_sum(cen * cen, ones_bd) * (1.0 / RWKV_HEAD)
    yn = cen * lax.rsqrt(var + GN_EPS) * gng_ref[...] + gnb_ref[...]
    o_ref[...] = ((yn + bonus_ref[...]) * g_ref[...]).astype(o_ref.dtype)


def _rwkv_post(y2, g, bonus, gn_g, gn_b, ones_bd, tb=256):
    _, t, dr = y2.shape
    tb = min(tb, t)
    tok = pl.BlockSpec((tb, dr), lambda i: (i, 0))
    vec = pl.BlockSpec((1, dr), lambda i: (0, 0))
    return pl.pallas_call(
        _post_kernel,
        grid=(t // tb,),
        in_specs=[pl.BlockSpec((2, tb, dr), lambda i: (0, i, 0)), tok, tok, vec, vec,
                  pl.BlockSpec((SCAN_W, SCAN_W), lambda i: (0, 0))],
        out_specs=tok,
        out_shape=jax.ShapeDtypeStruct((t, dr), BF16),
        compiler_params=_params("parallel"),
        name="rwkv_post",
    )(y2, g, bonus, gn_g.reshape(1, dr), gn_b.reshape(1, dr), ones_bd)


def _ln_kernel(x_ref, m_ref, g_ref, b_ref, o_ref, ob_ref, *, alpha):
    h = alpha * x_ref[...] + m_ref[...]
    mu = jnp.mean(h, axis=-1, keepdims=True)
    cen = h - mu
    var = jnp.mean(cen * cen, axis=-1, keepdims=True)
    y = cen * lax.rsqrt(var + LN_EPS) * g_ref[...] + b_ref[...]
    o_ref[...] = y
    ob_ref[...] = y.astype(BF16)


def _residual_ln(x, mix, g, b, alpha, tb=256):
    t, dm = x.shape
    tb = min(tb, t)
    tok = pl.BlockSpec((tb, dm), lambda i: (i, 0))
    vec = pl.BlockSpec((1, dm), lambda i: (0, 0))
    return pl.pallas_call(
        functools.partial(_ln_kernel, alpha=alpha),
        grid=(t // tb,),
        in_specs=[tok, tok, vec, vec],
        out_specs=[tok, tok],
        out_shape=[jax.ShapeDtypeStruct((t, dm), F32), jax.ShapeDtypeStruct((t, dm), BF16)],
        compiler_params=_params("parallel"),
        name="residual_ln",
    )(x, mix, g.reshape(1, dm), b.reshape(1, dm))


def _ffn_up_kernel(x_ref, wg_ref, wu_ref, *rest, gated, blocks_per_expert):
    x = x_ref[...]
    a = jnp.dot(x, wg_ref[...], preferred_element_type=F32)
    u = jnp.dot(x, wu_ref[...], preferred_element_type=F32)
    h = a * jax.nn.sigmoid(a) * u
    if gated:
        gate_ref, h_ref = rest
        e = pl.program_id(1) // blocks_per_expert
        lane = lax.broadcasted_iota(jnp.int32, gate_ref.shape, 1)
        h = h * jnp.sum(jnp.where(lane == e, gate_ref[...], 0.0), axis=1, keepdims=True)
    else:
        h_ref, = rest
    h_ref[...] = h.astype(h_ref.dtype)


def _ffn_up(xb, wg, wu, tm=1024, tn=512):
    t, dm = xb.shape
    f = wg.shape[1]
    tm = min(tm, t)
    wspec = pl.BlockSpec((dm, tn), lambda i, j: (0, j))
    return pl.pallas_call(
        functools.partial(_ffn_up_kernel, gated=False, blocks_per_expert=1),
        grid=(t // tm, f // tn),
        in_specs=[pl.BlockSpec((tm, dm), lambda i, j: (i, 0)), wspec, wspec],
        out_specs=pl.BlockSpec((tm, tn), lambda i, j: (i, j)),
        out_shape=jax.ShapeDtypeStruct((t, f), BF16),
        compiler_params=_params("parallel", "parallel"),
        name="ffn_up",
    )(xb, wg, wu)


def _moe_up(xb, we_gate, we_up, gates, tm=1024, tn=512):
    t, dm = xb.shape
    n_e, _, fe = we_gate.shape
    tm = min(tm, t)
    bpe = fe // tn
    wspec = pl.BlockSpec((None, dm, tn), lambda i, j: (j // bpe, 0, j % bpe))
    return pl.pallas_call(
        functools.partial(_ffn_up_kernel, gated=True, blocks_per_expert=bpe),
        grid=(t // tm, n_e * bpe),
        in_specs=[pl.BlockSpec((tm, dm), lambda i, j: (i, 0)), wspec, wspec,
                  pl.BlockSpec((tm, LANES), lambda i, j: (i, 0))],
        out_specs=pl.BlockSpec((tm, tn), lambda i, j: (i, j)),
        out_shape=jax.ShapeDtypeStruct((t, n_e * fe), BF16),
        compiler_params=_params("parallel", "parallel"),
        name="moe_up",
    )(xb, we_gate, we_up, gates)


def _router_kernel(x_ref, w_ref, o_ref):
    logits = jnp.dot(x_ref[...], w_ref[...], precision=HIGHEST, preferred_element_type=F32)
    lane = lax.broadcasted_iota(jnp.int32, logits.shape, 1)
    neg = jnp.float32(-jnp.inf)
    logits = jnp.where(lane < N_EXPERTS, logits, neg)
    m1 = jnp.max(logits, axis=1, keepdims=True)
    i1 = jnp.min(jnp.where(logits == m1, lane, LANES), axis=1, keepdims=True)
    rest = jnp.where(lane == i1, neg, logits)
    m2 = jnp.max(rest, axis=1, keepdims=True)
    i2 = jnp.min(jnp.where(rest == m2, lane, LANES), axis=1, keepdims=True)
    e2 = jnp.exp(m2 - m1)
    w1 = 1.0 / (1.0 + e2)
    w2 = e2 / (1.0 + e2)
    o_ref[...] = jnp.where(lane == i1, w1, 0.0) + jnp.where(lane == i2, w2, 0.0)


def _router(x, router_w, tb=512):
    t, dm = x.shape
    tb = min(tb, t)
    w = jnp.pad(router_w, ((0, 0), (0, LANES - router_w.shape[1])))
    return pl.pallas_call(
        _router_kernel,
        grid=(t // tb,),
        in_specs=[pl.BlockSpec((tb, dm), lambda i: (i, 0)), pl.BlockSpec((dm, LANES), lambda i: (0, 0))],
        out_specs=pl.BlockSpec((tb, LANES), lambda i: (i, 0)),
        out_shape=jax.ShapeDtypeStruct((t, LANES), F32),
        compiler_params=_params("parallel"),
        name="router",
    )(x, w)


def _ple_kernel(xb_ref, wg_ref, p_ref, wp_ref, x_ref, o_ref, ob_ref):
    gate = jax.nn.sigmoid(jnp.dot(xb_ref[...], wg_ref[...], preferred_element_type=F32))
    emb = jnp.dot(p_ref[...], wp_ref[...], preferred_element_type=F32)
    y = x_ref[...] + gate * emb
    o_ref[...] = y
    ob_ref[...] = y.astype(BF16)


def _ple(xb, x, w_pgate, p, w_pproj, tm=1024, tn=512):
    t, dm = xb.shape
    pd = p.shape[1]
    tm = min(tm, t)
    out = pl.BlockSpec((tm, tn), lambda i, j: (i, j))
    return pl.pallas_call(
        _ple_kernel,
        grid=(t // tm, dm // tn),
        in_specs=[pl.BlockSpec((tm, dm), lambda i, j: (i, 0)),
                  pl.BlockSpec((dm, tn), lambda i, j: (0, j)),
                  pl.BlockSpec((tm, pd), lambda i, j: (i, 0)),
                  pl.BlockSpec((pd, tn), lambda i, j: (0, j)),
                  out],
        out_specs=[out, out],
        out_shape=[jax.ShapeDtypeStruct((t, dm), F32), jax.ShapeDtypeStruct((t, dm), BF16)],
        compiler_params=_params("parallel", "parallel"),
        name="ple_gate",
    )(xb, w_pgate, p, w_pproj, x)


def _pad_cols(w, width):
    return jnp.pad(w, ((0, 0), (0, width - w.shape[1])))


def _pad_rows(w, rows):
    pad = [(0, 0)] * w.ndim
    pad[-2] = (0, rows - w.shape[-2])
    return jnp.pad(w, pad)


def _split_rwkv_cols(w, dg, dr, has_mv):
    base = 2 * dg
    rkv = w[:, base:base + 3 * dr]
    o = base + 3 * dr
    sizes = [DECAY_LORA, DECAY_LORA, AAA_LORA, AAA_LORA, GATE_LORA] + ([MV_LORA] if has_mv else [])
    slots = [LORA_SLOT, LORA_SLOT, LORA_SLOT, LORA_SLOT, GATE_LORA, LORA_SLOT]
    parts = []
    for n, slot in zip(sizes, slots):
        parts.append(_pad_cols(w[:, o:o + n], slot))
        o += n
    lora = _pad_cols(jnp.concatenate(parts, axis=1), LORA_COLS)
    return jnp.concatenate([rkv, lora], axis=1)


def _trunk(x, p, P, segs):
    t, dm = x.shape
    depth = P['w_out'].shape[0]
    alpha = (2 * depth) ** 0.25
    dg = P['sgu_ln_g'].shape[1]
    dr = P['k_k'].shape[1]
    hb = lax.broadcasted_iota(jnp.int32, (SCAN_W, SCAN_W), 0) // RWKV_HEAD
    hc = lax.broadcasted_iota(jnp.int32, (SCAN_W, SCAN_W), 1) // RWKV_HEAD
    ones_bd = (hb == hc).astype(BF16)
    xb = x.astype(BF16)
    v_first = None
    for l in range(depth):
        if l == 0:
            w_in_l, conv_l, v0_l, v2_l = P['w_in0'], P['conv0'], None, None
        else:
            w_in_l, conv_l, v0_l, v2_l = P['w_in'][l - 1], P['conv'][l - 1], P['v0'][l - 1], P['v2'][l - 1]
        uv = _matmul(xb, w_in_l[:, :2 * dg].astype(BF16), BF16)
        zr = _matmul(xb, _split_rwkv_cols(w_in_l, dg, dr, l > 0).astype(BF16), F32)
        y_g = _spatial_gating(uv, P['sgu_ln_g'][l], P['sgu_ln_b'][l], P['w_s'][l], P['b_s'][l])
        conv_r = _split_rwkv_cols(jnp.pad(conv_l, ((0, 0), (2 * dg, 0))), dg, dr, l > 0)
        prep_args = (zr, conv_r, P['w0'][l], _pad_rows(P['w2'][l], LORA_SLOT).astype(BF16), P['a0'][l],
                     _pad_rows(P['a2'][l], LORA_SLOT).astype(BF16), P['g2'][l].astype(BF16),
                     P['k_k'][l], P['k_a'][l], P['r_k'][l].reshape(-1), ones_bd, segs)
        if l == 0:
            r, v, kk, g, bonus, lw, k, b = _rwkv_prep(*prep_args)
            v_first = v
        else:
            r, v, kk, g, bonus, lw, k, b = _rwkv_prep(*prep_args, v_first=v_first, v0=v0_l,
                                                      v2=_pad_rows(v2_l, LORA_SLOT).astype(BF16))
        y2 = _rwkv_scan(r, v, kk, lw, k, b, segs)
        y_r = _rwkv_post(y2, g, bonus, P['gn_g'][l], P['gn_b'][l], ones_bd)
        mixed = jnp.concatenate([y_g, y_r], axis=1)
        mix = _matmul(mixed, P['w_out'][l].astype(BF16), F32)
        x, xb = _residual_ln(x, mix, P['ln1_g'][l], P['ln1_b'][l], alpha)
        j = l // 2
        if l % 2 == 0:
            h = _ffn_up(xb, P['w_ff_gate'][j].astype(BF16), P['w_ff_up'][j].astype(BF16))
            ff = _matmul(h, P['w_ff_down'][j].astype(BF16), F32, tk=2048)
        else:
            gates = _router(x, P['router'][j])
            h = _moe_up(xb, P['we_gate'][j].astype(BF16), P['we_up'][j].astype(BF16), gates)
            wd = P['we_down'][j]
            ff = _matmul(h, wd.reshape(-1, wd.shape[-1]).astype(BF16), F32, tk=2048)
        x, xb = _residual_ln(x, ff, P['ln2_g'][l], P['ln2_b'][l], alpha)
        x, xb = _ple(xb, x, P['w_pgate'][l].astype(BF16), p[l].astype(BF16), P['w_pproj'][l].astype(BF16))
    return x


def kernel(x_prompt, x_sample, p_prompt, p_sample, w_in0, conv0, w_in, conv, sgu_ln_g, sgu_ln_b, w_s, b_s, w0, w2, a0, a2, g2, k_k, k_a, r_k, gn_g, gn_b, v0, v2, w_out, ln1_g, ln1_b, ln2_g, ln2_b, w_ff_gate, w_ff_up, w_ff_down, router, we_gate, we_up, we_down, w_pproj, w_pgate):
    P = dict(w_in0=w_in0, conv0=conv0, w_in=w_in, conv=conv, sgu_ln_g=sgu_ln_g, sgu_ln_b=sgu_ln_b,
             w_s=w_s, b_s=b_s, w0=w0, w2=w2, a0=a0, a2=a2, g2=g2, k_k=k_k, k_a=k_a, r_k=r_k,
             gn_g=gn_g, gn_b=gn_b, v0=v0, v2=v2, w_out=w_out, ln1_g=ln1_g, ln1_b=ln1_b,
             ln2_g=ln2_g, ln2_b=ln2_b, w_ff_gate=w_ff_gate, w_ff_up=w_ff_up, w_ff_down=w_ff_down,
             router=router, we_gate=we_gate, we_up=we_up, we_down=we_down,
             w_pproj=w_pproj, w_pgate=w_pgate)
    b1, s1, dm = x_prompt.shape
    b2, s2, _ = x_sample.shape
    depth, pd = p_prompt.shape[0], p_prompt.shape[-1]
    t1, t2 = b1 * s1, b2 * s2
    segs = ((t1, s1), (t2, s2))
    x = jnp.concatenate([x_prompt.reshape(t1, dm), x_sample.reshape(t2, dm)], axis=0)
    p = jnp.concatenate([p_prompt.reshape(depth, t1, pd), p_sample.reshape(depth, t2, pd)], axis=1)
    y = _trunk(x, p, P, segs)
    return y[:t1].reshape(b1, s1, dm), y[t1:].reshape(b2, s2, dm)
```

```python
import functools

import jax
import jax.numpy as jnp
from jax import lax
from jax.experimental import pallas as pl
from jax.experimental.pallas import tpu as pltpu

F32 = jnp.float32
BF16 = jnp.bfloat16
HIGHEST = lax.Precision.HIGHEST

LANES = 128
SUBLANES = 8
VMEM_LIMIT_BYTES = 56 * 1024 * 1024

CHUNK = 128
GMLP_HEAD = 128
RWKV_HEAD = 64
DECAY_LORA = 96
AAA_LORA = 96
MV_LORA = 64
GATE_LORA = 256
N_EXPERTS = 8
LN_EPS = 1e-5
GN_EPS = 64e-5
L2_EPS = 1e-12

SCAN_L = 64
SCAN_W = 256
HEADS_PER_GROUP = SCAN_W // RWKV_HEAD
LORA_SLOT = 128
LORA_COLS = 1024


def _params(*sem):
    return pltpu.CompilerParams(dimension_semantics=sem, vmem_limit_bytes=VMEM_LIMIT_BYTES)


def _is_seq_start(pos, segs):
    (t1, s1), (_, s2) = segs
    return jnp.where(pos < t1, pos % s1 == 0, (pos - t1) % s2 == 0)


def _mm_kernel(x_ref, w_ref, o_ref, *scratch, nk):
    if nk == 1:
        o_ref[...] = jnp.dot(x_ref[...], w_ref[...], preferred_element_type=F32).astype(o_ref.dtype)
        return
    acc_ref, = scratch
    k = pl.program_id(2)

    @pl.when(k == 0)
    def _():
        acc_ref[...] = jnp.zeros_like(acc_ref)

    acc_ref[...] += jnp.dot(x_ref[...], w_ref[...], preferred_element_type=F32)

    @pl.when(k == nk - 1)
    def _():
        o_ref[...] = acc_ref[...].astype(o_ref.dtype)


def _matmul(x, w, out_dtype, tm=1024, tn=1024, tk=None):
    m, kdim = x.shape
    n = w.shape[1]
    tm, tn = min(tm, m), min(tn, n)
    tk = kdim if tk is None else min(tk, kdim)
    nk = kdim // tk
    assert m % tm == 0 and n % tn == 0 and kdim % tk == 0
    scratch = [] if nk == 1 else [pltpu.VMEM((tm, tn), F32)]
    return pl.pallas_call(
        functools.partial(_mm_kernel, nk=nk),
        grid=(m // tm, n // tn, nk),
        in_specs=[pl.BlockSpec((tm, tk), lambda i, j, k: (i, k)),
                  pl.BlockSpec((tk, tn), lambda i, j, k: (k, j))],
        out_specs=pl.BlockSpec((tm, tn), lambda i, j, k: (i, j)),
        out_shape=jax.ShapeDtypeStruct((m, n), out_dtype),
        scratch_shapes=scratch,
        compiler_params=_params("parallel", "parallel", "arbitrary"),
        name="matmul",
    )(x, w)


def _sgu_kernel(u_ref, v_ref, g_ref, b_ref, ws_ref, bs_ref, o_ref, *, n_chunks):
    gv = jax.nn.gelu(v_ref[...].astype(F32))
    mu = jnp.mean(gv, axis=-1, keepdims=True)
    cen = gv - mu
    var = jnp.mean(cen * cen, axis=-1, keepdims=True)
    vn = (cen * lax.rsqrt(var + LN_EPS) * g_ref[...] + b_ref[...]).astype(BF16)
    n_heads = ws_ref.shape[0]
    for c in range(n_chunks):
        rows = slice(c * CHUNK, (c + 1) * CHUNK)
        for h in range(n_heads):
            cols = slice(h * GMLP_HEAD, (h + 1) * GMLP_HEAD)
            sv = jnp.dot(ws_ref[h], vn[rows, cols], preferred_element_type=F32) + bs_ref[:, cols]
            gu = jax.nn.gelu(u_ref[rows, cols].astype(F32))
            o_ref[rows, cols] = (gu * sv).astype(o_ref.dtype)


def _spatial_gating(uv, ln_g, ln_b, w_s, b_s, tb=512):
    t = uv.shape[0]
    dg = uv.shape[1] // 2
    tb = min(tb, t)
    n_heads = w_s.shape[0]
    bs_full = jnp.repeat(b_s.T, GMLP_HEAD, axis=1)
    return pl.pallas_call(
        functools.partial(_sgu_kernel, n_chunks=tb // CHUNK),
        grid=(t // tb,),
        in_specs=[pl.BlockSpec((tb, dg), lambda i: (i, 0)),
                  pl.BlockSpec((tb, dg), lambda i: (i, 1)),
                  pl.BlockSpec((1, dg), lambda i: (0, 0)),
                  pl.BlockSpec((1, dg), lambda i: (0, 0)),
                  pl.BlockSpec((n_heads, CHUNK, CHUNK), lambda i: (0, 0, 0)),
                  pl.BlockSpec((CHUNK, dg), lambda i: (0, 0))],
        out_specs=pl.BlockSpec((tb, dg), lambda i: (i, 0)),
        out_shape=jax.ShapeDtypeStruct((t, dg), BF16),
        compiler_params=_params("parallel"),
        name="spatial_gating",
    )(uv, uv, ln_g.reshape(1, dg), ln_b.reshape(1, dg), w_s.astype(BF16), bs_full)


def _head_sum(x, ones_bd):
    parts = []
    for g in range(x.shape[1] // SCAN_W):
        xs = x[:, g * SCAN_W:(g + 1) * SCAN_W]
        hi = xs.astype(BF16)
        lo = (xs - hi.astype(F32)).astype(BF16)
        parts.append(jnp.dot(hi, ones_bd, preferred_element_type=F32)
                     + jnp.dot(lo, ones_bd, preferred_element_type=F32))
    return jnp.concatenate(parts, axis=1)


def _prep_kernel(*refs, layer1, tb, segs, dr):
    if layer1:
        (z_ref, zp_ref, zn_ref, conv_ref, w0_ref, w2_ref, a0_ref, a2_ref, g2_ref, kk_ref, ka_ref, rk_ref,
         ones_ref, vf_ref, v0_ref, v2_ref, r_o, v_o, kkn_o, g_o, bonus_o, lw_o, k_o, b_o) = refs
    else:
        (z_ref, zp_ref, zn_ref, conv_ref, w0_ref, w2_ref, a0_ref, a2_ref, g2_ref, kk_ref, ka_ref, rk_ref,
         ones_ref, r_o, v_o, kkn_o, g_o, bonus_o, lw_o, k_o, b_o) = refs
    pos0 = pl.program_id(0) * tb
    first = _is_seq_start(pos0, segs)
    last = _is_seq_start(pos0 + tb, segs)
    row = lax.broadcasted_iota(jnp.int32, (tb, 1), 0)

    def conv(c0, width):
        cols = slice(c0, c0 + width)
        z = z_ref[:, cols]
        prev_row = jnp.where(first, 0.0, zp_ref[SUBLANES - 1:SUBLANES, cols])
        next_row = jnp.where(last, 0.0, zn_ref[0:1, cols])
        zprev = jnp.where(row == 0, prev_row, pltpu.roll(z, 1, 0))
        znext = jnp.where(row == tb - 1, next_row, pltpu.roll(z, tb - 1, 0))
        return zprev * conv_ref[0:1, cols] + z * conv_ref[1:2, cols] + znext * conv_ref[2:3, cols]

    lo = conv(3 * dr, LORA_COLS)
    s = LORA_SLOT
    wd = (lo[:, 0:s], lo[:, s:2 * s])
    ad = (lo[:, 2 * s:3 * s], lo[:, 3 * s:4 * s])
    gd = lo[:, 4 * s:6 * s]

    def lora(x, w):
        return jnp.dot(x.astype(BF16), w, preferred_element_type=F32)

    g_o[...] = lora(jax.nn.sigmoid(gd), g2_ref[...])
    v = conv(2 * dr, dr)
    if layer1:
        mv = lo[:, 6 * s:7 * s]
        v = v + (vf_ref[...] - v) * jax.nn.sigmoid(v0_ref[...] + lora(mv, v2_ref[...]))
    v_o[...] = v
    r = conv(0, dr)
    r_o[...] = r
    k = conv(dr, dr)
    ones_bd = ones_ref[...]
    kk = k * kk_ref[...]
    kk = kk * lax.rsqrt(_head_sum(kk * kk, ones_bd) + L2_EPS)
    kkn_o[...] = kk
    for d in range(2):
        zw = w0_ref[d:d + 1, :] + lora(jnp.tanh(wd[d]), w2_ref[d])
        w_log = -(jnp.maximum(-zw, 0.0) + jnp.log(1.0 + jnp.exp(-jnp.abs(zw)))) - 0.5
        lw_o[d] = -jnp.exp(w_log)
        a = jax.nn.sigmoid(a0_ref[d:d + 1, :] + lora(ad[d], a2_ref[d]))
        k_d = k * (1.0 + (a - 1.0) * ka_ref[...])
        k_o[d] = k_d
        b_o[d] = kk * a
        if d == 0:
            bonus_o[...] = _head_sum(r * k_d * rk_ref[...], ones_bd) * v


def _rwkv_prep(zr, conv_w, w0, w2, a0, a2, g2, k_k, k_a, r_k, ones_bd, segs, v_first=None, v0=None, v2=None,
               tb=128):
    t, c = zr.shape
    dr = k_k.shape[-1]
    tb = min(tb, t)
    layer1 = v_first is not None
    hb = tb // SUBLANES
    n_halo = t // SUBLANES
    const2 = lambda i: (0, 0)
    const3 = lambda i: (0, 0, 0)
    in_specs = [pl.BlockSpec((tb, c), lambda i: (i, 0)),
                pl.BlockSpec((SUBLANES, c), lambda i: (jnp.maximum(i * hb - 1, 0), 0)),
                pl.BlockSpec((SUBLANES, c), lambda i: (jnp.minimum((i + 1) * hb, n_halo - 1), 0)),
                pl.BlockSpec((3, c), const2),
                pl.BlockSpec((2, dr), const2),
                pl.BlockSpec((2, LORA_SLOT, dr), const3),
                pl.BlockSpec((2, dr), const2),
                pl.BlockSpec((2, LORA_SLOT, dr), const3),
                pl.BlockSpec((GATE_LORA, dr), const2),
                pl.BlockSpec((1, dr), const2),
                pl.BlockSpec((1, dr), const2),
                pl.BlockSpec((1, dr), const2),
                pl.BlockSpec((SCAN_W, SCAN_W), const2)]
    args = [zr, zr, zr, conv_w, w0, w2, a0, a2, g2, k_k.reshape(1, dr), k_a.reshape(1, dr), r_k.reshape(1, dr),
            ones_bd]
    if layer1:
        in_specs += [pl.BlockSpec((tb, dr), lambda i: (i, 0)),
                     pl.BlockSpec((1, dr), const2),
                     pl.BlockSpec((LORA_SLOT, dr), const2)]
        args += [v_first, v0.reshape(1, dr), v2]
    tok = pl.BlockSpec((tb, dr), lambda i: (i, 0))
    tok2 = pl.BlockSpec((2, tb, dr), lambda i: (0, i, 0))
    one = jax.ShapeDtypeStruct((t, dr), F32)
    two = jax.ShapeDtypeStruct((2, t, dr), F32)
    return pl.pallas_call(
        functools.partial(_prep_kernel, layer1=layer1, tb=tb, segs=segs, dr=dr),
        grid=(t // tb,),
        in_specs=in_specs,
        out_specs=[tok, tok, tok, tok, tok, tok2, tok2, tok2],
        out_shape=[one, one, one, one, one, two, two, two],
        compiler_params=_params("parallel"),
        name="rwkv_prep",
    )(*args)


def _scan_consts(reverse):
    L, W, G = SCAN_L, SCAN_W, HEADS_PER_GROUP
    sgn = -1 if reverse else 1
    t_i = lax.broadcasted_iota(jnp.int32, (L, 3 * L), 0)
    s_i = lax.broadcasted_iota(jnp.int32, (L, 3 * L), 1) % L
    tri3 = jnp.where(sgn * (t_i - s_i) >= 0, 1.0, 0.0).astype(BF16)
    tc = lax.broadcasted_iota(jnp.int32, (L, G * L), 0)
    sc = lax.broadcasted_iota(jnp.int32, (L, G * L), 1) % L
    before = sgn * (tc - sc)
    strict = before > 0
    rmask = before > 0 if reverse else before >= 0
    eye = jnp.where(sc == tc, 1.0, 0.0).astype(F32)
    bi = lax.broadcasted_iota(jnp.int32, (W, W), 0) // RWKV_HEAD
    bj = lax.broadcasted_iota(jnp.int32, (W, W), 1) // RWKV_HEAD
    return tri3, strict, rmask, eye, bi == bj


def _bd(x, bdmask):
    return jnp.where(bdmask, jnp.tile(x, (HEADS_PER_GROUP, 1)), jnp.zeros((), x.dtype))


_NT = (((1,), (1,)), ((), ()))
_TN = (((0,), (0,)), ((), ()))


def _chunk_operators(r, v, kk, lw, k, b, consts, reverse):
    L, W, G = SCAN_L, SCAN_W, HEADS_PER_GROUP
    tri3, strict, rmask, eye, bdmask = consts
    bd = functools.partial(_bd, bdmask=bdmask)

    def dot(a_, b_):
        return jnp.dot(a_, b_, preferred_element_type=F32)

    hi = lw.astype(BF16)
    rem = lw - hi.astype(F32)
    mid = rem.astype(BF16)
    low = (rem - mid.astype(F32)).astype(BF16)
    cum = dot(tri3, jnp.concatenate([hi, mid, low], axis=0))
    yield
    tot = jnp.sum(lw, axis=0, keepdims=True)
    e_ex = jnp.exp(cum - lw)
    e_neg = jnp.exp(-cum)
    e_rem = jnp.exp(tot - cum)
    kq = kk * e_ex
    rq = r * (e_ex if reverse else jnp.exp(cum))
    vb = v.astype(BF16)
    q = jnp.concatenate([kq, rq], axis=0).astype(BF16)
    keys = jnp.concatenate([bd((k * e_neg).astype(BF16)), bd((b * e_neg).astype(BF16))], axis=0)
    a = lax.dot_general(q, keys, _NT, preferred_element_type=F32)
    yield
    a_kk = jnp.where(strict, a[:L, :G * L], 0.0)
    a_kb = jnp.where(strict, a[:L, G * L:], 0.0)
    a_rk = jnp.where(rmask, a[L:, :G * L], 0.0)
    a_rb = jnp.where(rmask, a[L:, G * L:], 0.0).astype(BF16)
    p = -a_kb
    tinv = eye + p
    av = dot(jnp.concatenate([a_kk, a_rk], axis=0).astype(BF16), bd(vb))
    p = dot(p.astype(BF16), bd(p.astype(BF16)))
    yield
    levels = L.bit_length() - 2
    for lvl in range(levels):
        wt = bd(p.astype(BF16))
        if lvl < levels - 1:
            tp = dot(jnp.concatenate([tinv, p], axis=0).astype(BF16), wt)
            tinv = tinv + tp[:L]
            p = tp[L:]
        else:
            tinv = tinv + dot(tinv.astype(BF16), wt)
        yield
    tinv = tinv.astype(BF16)
    tk = dot(tinv, jnp.concatenate([bd(kq.astype(BF16)), bd(av[:L].astype(BF16))], axis=1))
    yield
    kq2 = tk[:, :W].astype(BF16)
    cu = tk[:, W:].astype(BF16)
    ar = dot(a_rb, jnp.concatenate([bd(kq2), bd(cu)], axis=1))
    yield
    rq2 = (rq - ar[:, :W]).astype(BF16)
    yc = av[L:] - ar[:, W:]
    keb = (k * e_rem).astype(BF16)
    beb = (b * e_rem).astype(BF16)
    g = jnp.where(bdmask, lax.dot_general(kq2, beb, _TN, preferred_element_type=F32), 0.0).astype(BF16)
    c = jnp.where(bdmask, lax.dot_general(jnp.concatenate([vb, -cu], axis=0),
                                          jnp.concatenate([keb, beb], axis=0), _TN,
                                          preferred_element_type=F32), 0.0)
    return rq2, yc, jnp.exp(tot), g, c


def _run_in_lockstep(gens):
    results = [None] * len(gens)
    active = list(enumerate(gens))
    while active:
        still = []
        for i, gen in active:
            try:
                next(gen)
                still.append((i, gen))
            except StopIteration as done:
                results[i] = done.value
        active = still
    return results


def _scan_kernel(rf_ref, vf_ref, kkf_ref, lwf_ref, kf_ref, bf_ref, rb_ref, vb_ref, kkb_ref, lwb_ref, kb_ref,
                 bb_ref, yf_ref, yb_ref, s_ref, *, tb, segs, nblk, ng):
    L, W = SCAN_L, SCAN_W
    c = pl.program_id(1)
    start_f = _is_seq_start(c * tb, segs)
    start_b = _is_seq_start((nblk - c) * tb, segs)

    @pl.when(start_f)
    def _():
        s_ref[0] = jnp.zeros(s_ref.shape[1:], F32)

    @pl.when(start_b)
    def _():
        s_ref[1] = jnp.zeros(s_ref.shape[1:], F32)

    n_chunks = tb // L
    dirs = ((rf_ref, vf_ref, kkf_ref, lwf_ref, kf_ref, bf_ref, yf_ref),
            (rb_ref, vb_ref, kkb_ref, lwb_ref, kb_ref, bb_ref, yb_ref))
    streams = [(d, g) for d in range(2) for g in range(ng)]
    order = {d: [n_chunks - 1 - j if d == 1 else j for j in range(n_chunks)] for d in range(2)}
    consts = {d: _scan_consts(reverse=d == 1) for d in range(2)}
    gens = []
    for d, g in streams:
        r_ref, v_ref, kk_ref, lw_ref, k_ref, b_ref, _ = dirs[d]
        cols = slice(g * W, (g + 1) * W)
        for ci in order[d]:
            rows = slice(ci * L, (ci + 1) * L)
            gens.append(_chunk_operators(r_ref[rows, cols], v_ref[rows, cols], kk_ref[rows, cols],
                                         lw_ref[rows, cols], k_ref[rows, cols], b_ref[rows, cols],
                                         consts[d], reverse=d == 1))
    ops = _run_in_lockstep(gens)
    states = [s_ref[d, g] for d, g in streams]
    for j in range(n_chunks):
        for si, (d, g) in enumerate(streams):
            rq2, yc, decay, gmat, cmat = ops[si * n_chunks + j]
            ci = order[d][j]
            sb = states[si].astype(BF16)
            dirs[d][6][ci * L:(ci + 1) * L, g * W:(g + 1) * W] = (
                lax.dot_general(rq2, sb, _NT, preferred_element_type=F32) + yc)
            states[si] = states[si] * decay - jnp.dot(sb, gmat, preferred_element_type=F32) + cmat
    for si, (d, g) in enumerate(streams):
        s_ref[d, g] = states[si]


def _rwkv_scan(r, v, kk, lw, k, b, segs, tb=256, ng=1):
    t, dr = r.shape
    tb = min(tb, t)
    nblk = t // tb
    w = ng * SCAN_W
    fwd = pl.BlockSpec((tb, w), lambda g, c: (c, g))
    bwd = pl.BlockSpec((tb, w), lambda g, c: (nblk - 1 - c, g))
    fwd2 = pl.BlockSpec((None, tb, w), lambda g, c: (0, c, g))
    bwd2 = pl.BlockSpec((None, tb, w), lambda g, c: (1, nblk - 1 - c, g))
    out = jax.ShapeDtypeStruct((t, dr), F32)
    return pl.pallas_call(
        functools.partial(_scan_kernel, tb=tb, segs=segs, nblk=nblk, ng=ng),
        grid=(dr // w, nblk),
        in_specs=[fwd, fwd, fwd, fwd2, fwd2, fwd2, bwd, bwd, bwd, bwd2, bwd2, bwd2],
        out_specs=[fwd, bwd],
        out_shape=[out, out],
        scratch_shapes=[pltpu.VMEM((2, ng, SCAN_W, SCAN_W), F32)],
        compiler_params=_params("parallel", "arbitrary"),
        name="rwkv_scan",
    )(r, v, kk, lw, k, b, r, v, kk, lw, k, b)


def _post_kernel(yf_ref, yb_ref, g_ref, bonus_ref, gng_ref, gnb_ref, ones_ref, o_ref):
    y = yf_ref[...] + yb_ref[...]
    ones_bd = ones_ref[...]
    mu = _head_sum(y, ones_bd) * (1.0 / RWKV_HEAD)
    cen = y - mu
    var = _head_sum(cen * cen, ones_bd) * (1.0 / RWKV_HEAD)
    yn = cen * lax.rsqrt(var + GN_EPS) * gng_ref[...] + gnb_ref[...]
    o_ref[...] = ((yn + bonus_ref[...]) * g_ref[...]).astype(o_ref.dtype)


def _rwkv_post(y_f, y_b, g, bonus, gn_g, gn_b, ones_bd, tb=256):
    t, dr = y_f.shape
    tb = min(tb, t)
    tok = pl.BlockSpec((tb, dr), lambda i: (i, 0))
    vec = pl.BlockSpec((1, dr), lambda i: (0, 0))
    return pl.pallas_call(
        _post_kernel,
        grid=(t // tb,),
        in_specs=[tok, tok, tok, tok, vec, vec, pl.BlockSpec((SCAN_W, SCAN_W), lambda i: (0, 0))],
        out_specs=tok,
        out_shape=jax.ShapeDtypeStruct((t, dr), BF16),
        compiler_params=_params("parallel"),
        name="rwkv_post",
    )(y_f, y_b, g, bonus, gn_g.reshape(1, dr), gn_b.reshape(1, dr), ones_bd)


def _ln_kernel(x_ref, m_ref, g_ref, b_ref, o_ref, ob_ref, *, alpha):
    h = alpha * x_ref[...] + m_ref[...]
    mu = jnp.mean(h, axis=-1, keepdims=True)
    cen = h - mu
    var = jnp.mean(cen * cen, axis=-1, keepdims=True)
    y = cen * lax.rsqrt(var + LN_EPS) * g_ref[...] + b_ref[...]
    o_ref[...] = y
    ob_ref[...] = y.astype(BF16)


def _residual_ln(x, mix, g, b, alpha, tb=256):
    t, dm = x.shape
    tb = min(tb, t)
    tok = pl.BlockSpec((tb, dm), lambda i: (i, 0))
    vec = pl.BlockSpec((1, dm), lambda i: (0, 0))
    return pl.pallas_call(
        functools.partial(_ln_kernel, alpha=alpha),
        grid=(t // tb,),
        in_specs=[tok, tok, vec, vec],
        out_specs=[tok, tok],
        out_shape=[jax.ShapeDtypeStruct((t, dm), F32), jax.ShapeDtypeStruct((t, dm), BF16)],
        compiler_params=_params("parallel"),
        name="residual_ln",
    )(x, mix, g.reshape(1, dm), b.reshape(1, dm))


def _ffn_up_kernel(x_ref, wg_ref, wu_ref, *rest, gated, blocks_per_expert):
    x = x_ref[...]
    a = jnp.dot(x, wg_ref[...], preferred_element_type=F32)
    u = jnp.dot(x, wu_ref[...], preferred_element_type=F32)
    h = a * jax.nn.sigmoid(a) * u
    if gated:
        gate_ref, h_ref = rest
        e = pl.program_id(1) // blocks_per_expert
        lane = lax.broadcasted_iota(jnp.int32, gate_ref.shape, 1)
        h = h * jnp.sum(jnp.where(lane == e, gate_ref[...], 0.0), axis=1, keepdims=True)
    else:
        h_ref, = rest
    h_ref[...] = h.astype(h_ref.dtype)


def _ffn_up(xb, wg, wu, tm=1024, tn=512):
    t, dm = xb.shape
    f = wg.shape[1]
    tm = min(tm, t)
    wspec = pl.BlockSpec((dm, tn), lambda i, j: (0, j))
    return pl.pallas_call(
        functools.partial(_ffn_up_kernel, gated=False, blocks_per_expert=1),
        grid=(t // tm, f // tn),
        in_specs=[pl.BlockSpec((tm, dm), lambda i, j: (i, 0)), wspec, wspec],
        out_specs=pl.BlockSpec((tm, tn), lambda i, j: (i, j)),
        out_shape=jax.ShapeDtypeStruct((t, f), BF16),
        compiler_params=_params("parallel", "parallel"),
        name="ffn_up",
    )(xb, wg, wu)


def _moe_up(xb, we_gate, we_up, gates, tm=1024, tn=512):
    t, dm = xb.shape
    n_e, _, fe = we_gate.shape
    tm = min(tm, t)
    bpe = fe // tn
    wspec = pl.BlockSpec((None, dm, tn), lambda i, j: (j // bpe, 0, j % bpe))
    return pl.pallas_call(
        functools.partial(_ffn_up_kernel, gated=True, blocks_per_expert=bpe),
        grid=(t // tm, n_e * bpe),
        in_specs=[pl.BlockSpec((tm, dm), lambda i, j: (i, 0)), wspec, wspec,
                  pl.BlockSpec((tm, LANES), lambda i, j: (i, 0))],
        out_specs=pl.BlockSpec((tm, tn), lambda i, j: (i, j)),
        out_shape=jax.ShapeDtypeStruct((t, n_e * fe), BF16),
        compiler_params=_params("parallel", "parallel"),
        name="moe_up",
    )(xb, we_gate, we_up, gates)


def _router_kernel(x_ref, w_ref, o_ref):
    logits = jnp.dot(x_ref[...], w_ref[...], precision=HIGHEST, preferred_element_type=F32)
    lane = lax.broadcasted_iota(jnp.int32, logits.shape, 1)
    neg = jnp.float32(-jnp.inf)
    logits = jnp.where(lane < N_EXPERTS, logits, neg)
    m1 = jnp.max(logits, axis=1, keepdims=True)
    i1 = jnp.min(jnp.where(logits == m1, lane, LANES), axis=1, keepdims=True)
    rest = jnp.where(lane == i1, neg, logits)
    m2 = jnp.max(rest, axis=1, keepdims=True)
    i2 = jnp.min(jnp.where(rest == m2, lane, LANES), axis=1, keepdims=True)
    e2 = jnp.exp(m2 - m1)
    w1 = 1.0 / (1.0 + e2)
    w2 = e2 / (1.0 + e2)
    o_ref[...] = jnp.where(lane == i1, w1, 0.0) + jnp.where(lane == i2, w2, 0.0)


def _router(x, router_w, tb=512):
    t, dm = x.shape
    tb = min(tb, t)
    w = jnp.pad(router_w, ((0, 0), (0, LANES - router_w.shape[1])))
    return pl.pallas_call(
        _router_kernel,
        grid=(t // tb,),
        in_specs=[pl.BlockSpec((tb, dm), lambda i: (i, 0)), pl.BlockSpec((dm, LANES), lambda i: (0, 0))],
        out_specs=pl.BlockSpec((tb, LANES), lambda i: (i, 0)),
        out_shape=jax.ShapeDtypeStruct((t, LANES), F32),
        compiler_params=_params("parallel"),
        name="router",
    )(x, w)


def _ple_kernel(xb_ref, wg_ref, p_ref, wp_ref, x_ref, o_ref, ob_ref):
    gate = jax.nn.sigmoid(jnp.dot(xb_ref[...], wg_ref[...], preferred_element_type=F32))
    emb = jnp.dot(p_ref[...], wp_ref[...], preferred_element_type=F32)
    y = x_ref[...] + gate * emb
    o_ref[...] = y
    ob_ref[...] = y.astype(BF16)


def _ple(xb, x, w_pgate, p, w_pproj, tm=1024, tn=512):
    t, dm = xb.shape
    pd = p.shape[1]
    tm = min(tm, t)
    out = pl.BlockSpec((tm, tn), lambda i, j: (i, j))
    return pl.pallas_call(
        _ple_kernel,
        grid=(t // tm, dm // tn),
        in_specs=[pl.BlockSpec((tm, dm), lambda i, j: (i, 0)),
                  pl.BlockSpec((dm, tn), lambda i, j: (0, j)),
                  pl.BlockSpec((tm, pd), lambda i, j: (i, 0)),
                  pl.BlockSpec((pd, tn), lambda i, j: (0, j)),
                  out],
        out_specs=[out, out],
        out_shape=[jax.ShapeDtypeStruct((t, dm), F32), jax.ShapeDtypeStruct((t, dm), BF16)],
        compiler_params=_params("parallel", "parallel"),
        name="ple_gate",
    )(xb, w_pgate, p, w_pproj, x)


def _pad_cols(w, width):
    return jnp.pad(w, ((0, 0), (0, width - w.shape[1])))


def _pad_rows(w, rows):
    pad = [(0, 0)] * w.ndim
    pad[-2] = (0, rows - w.shape[-2])
    return jnp.pad(w, pad)


def _split_rwkv_cols(w, dg, dr, has_mv):
    base = 2 * dg
    rkv = w[:, base:base + 3 * dr]
    o = base + 3 * dr
    sizes = [DECAY_LORA, DECAY_LORA, AAA_LORA, AAA_LORA, GATE_LORA] + ([MV_LORA] if has_mv else [])
    slots = [LORA_SLOT, LORA_SLOT, LORA_SLOT, LORA_SLOT, GATE_LORA, LORA_SLOT]
    parts = []
    for n, slot in zip(sizes, slots):
        parts.append(_pad_cols(w[:, o:o + n], slot))
        o += n
    lora = _pad_cols(jnp.concatenate(parts, axis=1), LORA_COLS)
    return jnp.concatenate([rkv, lora], axis=1)


def _trunk(x, p, P, segs):
    t, dm = x.shape
    depth = P['w_out'].shape[0]
    alpha = (2 * depth) ** 0.25
    dg = P['sgu_ln_g'].shape[1]
    dr = P['k_k'].shape[1]
    hb = lax.broadcasted_iota(jnp.int32, (SCAN_W, SCAN_W), 0) // RWKV_HEAD
    hc = lax.broadcasted_iota(jnp.int32, (SCAN_W, SCAN_W), 1) // RWKV_HEAD
    ones_bd = (hb == hc).astype(BF16)
    xb = x.astype(BF16)
    v_first = None
    for l in range(depth):
        if l == 0:
            w_in_l, conv_l, v0_l, v2_l = P['w_in0'], P['conv0'], None, None
        else:
            w_in_l, conv_l, v0_l, v2_l = P['w_in'][l - 1], P['conv'][l - 1], P['v0'][l - 1], P['v2'][l - 1]
        uv = _matmul(xb, w_in_l[:, :2 * dg].astype(BF16), BF16)
        zr = _matmul(xb, _split_rwkv_cols(w_in_l, dg, dr, l > 0).astype(BF16), F32)
        y_g = _spatial_gating(uv, P['sgu_ln_g'][l], P['sgu_ln_b'][l], P['w_s'][l], P['b_s'][l])
        conv_r = _split_rwkv_cols(jnp.pad(conv_l, ((0, 0), (2 * dg, 0))), dg, dr, l > 0)
        prep_args = (zr, conv_r, P['w0'][l], _pad_rows(P['w2'][l], LORA_SLOT).astype(BF16), P['a0'][l],
                     _pad_rows(P['a2'][l], LORA_SLOT).astype(BF16), P['g2'][l].astype(BF16),
                     P['k_k'][l], P['k_a'][l], P['r_k'][l].reshape(-1), ones_bd, segs)
        if l == 0:
            r, v, kk, g, bonus, lw, k, b = _rwkv_prep(*prep_args)
            v_first = v
        else:
            r, v, kk, g, bonus, lw, k, b = _rwkv_prep(*prep_args, v_first=v_first, v0=v0_l,
                                                      v2=_pad_rows(v2_l, LORA_SLOT).astype(BF16))
        y_f, y_b = _rwkv_scan(r, v, kk, lw, k, b, segs)
        y_r = _rwkv_post(y_f, y_b, g, bonus, P['gn_g'][l], P['gn_b'][l], ones_bd)
        mixed = jnp.concatenate([y_g, y_r], axis=1)
        mix = _matmul(mixed, P['w_out'][l].astype(BF16), F32)
        x, xb = _residual_ln(x, mix, P['ln1_g'][l], P['ln1_b'][l], alpha)
        j = l // 2
        if l % 2 == 0:
            h = _ffn_up(xb, P['w_ff_gate'][j].astype(BF16), P['w_ff_up'][j].astype(BF16))
            ff = _matmul(h, P['w_ff_down'][j].astype(BF16), F32, tk=2048)
        else:
            gates = _router(x, P['router'][j])
            h = _moe_up(xb, P['we_gate'][j].astype(BF16), P['we_up'][j].astype(BF16), gates)
            wd = P['we_down'][j]
            ff = _matmul(h, wd.reshape(-1, wd.shape[-1]).astype(BF16), F32, tk=2048)
        x, xb = _residual_ln(x, ff, P['ln2_g'][l], P['ln2_b'][l], alpha)
        x, xb = _ple(xb, x, P['w_pgate'][l].astype(BF16), p[l].astype(BF16), P['w_pproj'][l].astype(BF16))
    return x


def kernel(x_prompt, x_sample, p_prompt, p_sample, w_in0, conv0, w_in, conv, sgu_ln_g, sgu_ln_b, w_s, b_s, w0, w2, a0, a2, g2, k_k, k_a, r_k, gn_g, gn_b, v0, v2, w_out, ln1_g, ln1_b, ln2_g, ln2_b, w_ff_gate, w_ff_up, w_ff_down, router, we_gate, we_up, we_down, w_pproj, w_pgate):
    P = dict(w_in0=w_in0, conv0=conv0, w_in=w_in, conv=conv, sgu_ln_g=sgu_ln_g, sgu_ln_b=sgu_ln_b,
             w_s=w_s, b_s=b_s, w0=w0, w2=w2, a0=a0, a2=a2, g2=g2, k_k=k_k, k_a=k_a, r_k=r_k,
             gn_g=gn_g, gn_b=gn_b, v0=v0, v2=v2, w_out=w_out, ln1_g=ln1_g, ln1_b=ln1_b,
             ln2_g=ln2_g, ln2_b=ln2_b, w_ff_gate=w_ff_gate, w_ff_up=w_ff_up, w_ff_down=w_ff_down,
             router=router, we_gate=we_gate, we_up=we_up, we_down=we_down,
             w_pproj=w_pproj, w_pgate=w_pgate)
    b1, s1, dm = x_prompt.shape
    b2, s2, _ = x_sample.shape
    depth, pd = p_prompt.shape[0], p_prompt.shape[-1]
    t1, t2 = b1 * s1, b2 * s2
    segs = ((t1, s1), (t2, s2))
    x = jnp.concatenate([x_prompt.reshape(t1, dm), x_sample.reshape(t2, dm)], axis=0)
    p = jnp.concatenate([p_prompt.reshape(depth, t1, pd), p_sample.reshape(depth, t2, pd)], axis=1)
    y = _trunk(x, p, P, segs)
    return y[:t1].reshape(b1, s1, dm), y[t1:].reshape(b2, s2, dm)
```

```python
import functools

import jax
import jax.numpy as jnp
from jax import lax
from jax.experimental import pallas as pl
from jax.experimental.pallas import tpu as pltpu

F32 = jnp.float32
BF16 = jnp.bfloat16
HIGHEST = lax.Precision.HIGHEST

LANES = 128
SUBLANES = 8
VMEM_LIMIT_BYTES = 56 * 1024 * 1024

CHUNK = 128
GMLP_HEAD = 128
RWKV_HEAD = 64
DECAY_LORA = 96
AAA_LORA = 96
MV_LORA = 64
GATE_LORA = 256
N_EXPERTS = 8
LN_EPS = 1e-5
GN_EPS = 64e-5
L2_EPS = 1e-12

SCAN_L = 64
SCAN_W = 256
HEADS_PER_GROUP = SCAN_W // RWKV_HEAD
LORA_SLOT = 128
LORA_COLS = 1024


def _params(*sem):
    return pltpu.CompilerParams(dimension_semantics=sem, vmem_limit_bytes=VMEM_LIMIT_BYTES)


def _is_seq_start(pos, segs):
    (t1, s1), (_, s2) = segs
    return jnp.where(pos < t1, pos % s1 == 0, (pos - t1) % s2 == 0)


def _mm_kernel(x_ref, w_ref, o_ref, *scratch, nk):
    if nk == 1:
        o_ref[...] = jnp.dot(x_ref[...], w_ref[...], preferred_element_type=F32).astype(o_ref.dtype)
        return
    acc_ref, = scratch
    k = pl.program_id(2)

    @pl.when(k == 0)
    def _():
        acc_ref[...] = jnp.zeros_like(acc_ref)

    acc_ref[...] += jnp.dot(x_ref[...], w_ref[...], preferred_element_type=F32)

    @pl.when(k == nk - 1)
    def _():
        o_ref[...] = acc_ref[...].astype(o_ref.dtype)


def _matmul(x, w, out_dtype, tm=1024, tn=1024, tk=None):
    m, kdim = x.shape
    n = w.shape[1]
    tm, tn = min(tm, m), min(tn, n)
    tk = kdim if tk is None else min(tk, kdim)
    nk = kdim // tk
    assert m % tm == 0 and n % tn == 0 and kdim % tk == 0
    scratch = [] if nk == 1 else [pltpu.VMEM((tm, tn), F32)]
    return pl.pallas_call(
        functools.partial(_mm_kernel, nk=nk),
        grid=(m // tm, n // tn, nk),
        in_specs=[pl.BlockSpec((tm, tk), lambda i, j, k: (i, k)),
                  pl.BlockSpec((tk, tn), lambda i, j, k: (k, j))],
        out_specs=pl.BlockSpec((tm, tn), lambda i, j, k: (i, j)),
        out_shape=jax.ShapeDtypeStruct((m, n), out_dtype),
        scratch_shapes=scratch,
        compiler_params=_params("parallel", "parallel", "arbitrary"),
        name="matmul",
    )(x, w)


def _sgu_kernel(u_ref, v_ref, g_ref, b_ref, ws_ref, bs_ref, o_ref, *, n_chunks):
    gv = jax.nn.gelu(v_ref[...].astype(F32))
    mu = jnp.mean(gv, axis=-1, keepdims=True)
    cen = gv - mu
    var = jnp.mean(cen * cen, axis=-1, keepdims=True)
    vn = (cen * lax.rsqrt(var + LN_EPS) * g_ref[...] + b_ref[...]).astype(BF16)
    n_heads = ws_ref.shape[0]
    for c in range(n_chunks):
        rows = slice(c * CHUNK, (c + 1) * CHUNK)
        for h in range(n_heads):
            cols = slice(h * GMLP_HEAD, (h + 1) * GMLP_HEAD)
            sv = jnp.dot(ws_ref[h], vn[rows, cols], preferred_element_type=F32) + bs_ref[:, cols]
            gu = jax.nn.gelu(u_ref[rows, cols].astype(F32))
            o_ref[rows, cols] = (gu * sv).astype(o_ref.dtype)


def _spatial_gating(uv, ln_g, ln_b, w_s, b_s, tb=512):
    t = uv.shape[0]
    dg = uv.shape[1] // 2
    tb = min(tb, t)
    n_heads = w_s.shape[0]
    bs_full = jnp.repeat(b_s.T, GMLP_HEAD, axis=1)
    return pl.pallas_call(
        functools.partial(_sgu_kernel, n_chunks=tb // CHUNK),
        grid=(t // tb,),
        in_specs=[pl.BlockSpec((tb, dg), lambda i: (i, 0)),
                  pl.BlockSpec((tb, dg), lambda i: (i, 1)),
                  pl.BlockSpec((1, dg), lambda i: (0, 0)),
                  pl.BlockSpec((1, dg), lambda i: (0, 0)),
                  pl.BlockSpec((n_heads, CHUNK, CHUNK), lambda i: (0, 0, 0)),
                  pl.BlockSpec((CHUNK, dg), lambda i: (0, 0))],
        out_specs=pl.BlockSpec((tb, dg), lambda i: (i, 0)),
        out_shape=jax.ShapeDtypeStruct((t, dg), BF16),
        compiler_params=_params("parallel"),
        name="spatial_gating",
    )(uv, uv, ln_g.reshape(1, dg), ln_b.reshape(1, dg), w_s.astype(BF16), bs_full)


def _head_sum(x, ones_bd):
    parts = []
    for g in range(x.shape[1] // SCAN_W):
        xs = x[:, g * SCAN_W:(g + 1) * SCAN_W]
        hi = xs.astype(BF16)
        lo = (xs - hi.astype(F32)).astype(BF16)
        parts.append(jnp.dot(hi, ones_bd, preferred_element_type=F32)
                     + jnp.dot(lo, ones_bd, preferred_element_type=F32))
    return jnp.concatenate(parts, axis=1)


def _prep_kernel(*refs, layer1, tb, segs, dr):
    if layer1:
        (z_ref, zp_ref, zn_ref, conv_ref, w0_ref, w2_ref, a0_ref, a2_ref, g2_ref, kk_ref, ka_ref, rk_ref,
         ones_ref, vf_ref, v0_ref, v2_ref, r_o, v_o, kkn_o, g_o, bonus_o, lw_o, k_o, b_o) = refs
    else:
        (z_ref, zp_ref, zn_ref, conv_ref, w0_ref, w2_ref, a0_ref, a2_ref, g2_ref, kk_ref, ka_ref, rk_ref,
         ones_ref, r_o, v_o, kkn_o, g_o, bonus_o, lw_o, k_o, b_o) = refs
    pos0 = pl.program_id(0) * tb
    first = _is_seq_start(pos0, segs)
    last = _is_seq_start(pos0 + tb, segs)
    row = lax.broadcasted_iota(jnp.int32, (tb, 1), 0)

    def conv(c0, width):
        cols = slice(c0, c0 + width)
        z = z_ref[:, cols]
        prev_row = jnp.where(first, 0.0, zp_ref[SUBLANES - 1:SUBLANES, cols])
        next_row = jnp.where(last, 0.0, zn_ref[0:1, cols])
        zprev = jnp.where(row == 0, prev_row, pltpu.roll(z, 1, 0))
        znext = jnp.where(row == tb - 1, next_row, pltpu.roll(z, tb - 1, 0))
        return zprev * conv_ref[0:1, cols] + z * conv_ref[1:2, cols] + znext * conv_ref[2:3, cols]

    lo = conv(3 * dr, LORA_COLS)
    s = LORA_SLOT
    wd = (lo[:, 0:s], lo[:, s:2 * s])
    ad = (lo[:, 2 * s:3 * s], lo[:, 3 * s:4 * s])
    gd = lo[:, 4 * s:6 * s]

    def lora(x, w):
        return jnp.dot(x.astype(BF16), w, preferred_element_type=F32)

    g_o[...] = lora(jax.nn.sigmoid(gd), g2_ref[...])
    v = conv(2 * dr, dr)
    if layer1:
        mv = lo[:, 6 * s:7 * s]
        v = v + (vf_ref[...] - v) * jax.nn.sigmoid(v0_ref[...] + lora(mv, v2_ref[...]))
    v_o[...] = v
    r = conv(0, dr)
    r_o[...] = r
    k = conv(dr, dr)
    ones_bd = ones_ref[...]
    kk = k * kk_ref[...]
    kk = kk * lax.rsqrt(_head_sum(kk * kk, ones_bd) + L2_EPS)
    kkn_o[...] = kk
    for d in range(2):
        zw = w0_ref[d:d + 1, :] + lora(jnp.tanh(wd[d]), w2_ref[d])
        w_log = -(jnp.maximum(-zw, 0.0) + jnp.log(1.0 + jnp.exp(-jnp.abs(zw)))) - 0.5
        lw_o[d] = -jnp.exp(w_log)
        a = jax.nn.sigmoid(a0_ref[d:d + 1, :] + lora(ad[d], a2_ref[d]))
        k_d = k * (1.0 + (a - 1.0) * ka_ref[...])
        k_o[d] = k_d
        b_o[d] = kk * a
        if d == 0:
            bonus_o[...] = _head_sum(r * k_d * rk_ref[...], ones_bd) * v


def _rwkv_prep(zr, conv_w, w0, w2, a0, a2, g2, k_k, k_a, r_k, ones_bd, segs, v_first=None, v0=None, v2=None,
               tb=128):
    t, c = zr.shape
    dr = k_k.shape[-1]
    tb = min(tb, t)
    layer1 = v_first is not None
    hb = tb // SUBLANES
    n_halo = t // SUBLANES
    const2 = lambda i: (0, 0)
    const3 = lambda i: (0, 0, 0)
    in_specs = [pl.BlockSpec((tb, c), lambda i: (i, 0)),
                pl.BlockSpec((SUBLANES, c), lambda i: (jnp.maximum(i * hb - 1, 0), 0)),
                pl.BlockSpec((SUBLANES, c), lambda i: (jnp.minimum((i + 1) * hb, n_halo - 1), 0)),
                pl.BlockSpec((3, c), const2),
                pl.BlockSpec((2, dr), const2),
                pl.BlockSpec((2, LORA_SLOT, dr), const3),
                pl.BlockSpec((2, dr), const2),
                pl.BlockSpec((2, LORA_SLOT, dr), const3),
                pl.BlockSpec((GATE_LORA, dr), const2),
                pl.BlockSpec((1, dr), const2),
                pl.BlockSpec((1, dr), const2),
                pl.BlockSpec((1, dr), const2),
                pl.BlockSpec((SCAN_W, SCAN_W), const2)]
    args = [zr, zr, zr, conv_w, w0, w2, a0, a2, g2, k_k.reshape(1, dr), k_a.reshape(1, dr), r_k.reshape(1, dr),
            ones_bd]
    if layer1:
        in_specs += [pl.BlockSpec((tb, dr), lambda i: (i, 0)),
                     pl.BlockSpec((1, dr), const2),
                     pl.BlockSpec((LORA_SLOT, dr), const2)]
        args += [v_first, v0.reshape(1, dr), v2]
    tok = pl.BlockSpec((tb, dr), lambda i: (i, 0))
    tok2 = pl.BlockSpec((2, tb, dr), lambda i: (0, i, 0))
    one = jax.ShapeDtypeStruct((t, dr), F32)
    two = jax.ShapeDtypeStruct((2, t, dr), F32)
    return pl.pallas_call(
        functools.partial(_prep_kernel, layer1=layer1, tb=tb, segs=segs, dr=dr),
        grid=(t // tb,),
        in_specs=in_specs,
        out_specs=[tok, tok, tok, tok, tok, tok2, tok2, tok2],
        out_shape=[one, one, one, one, one, two, two, two],
        compiler_params=_params("parallel"),
        name="rwkv_prep",
    )(*args)


def _scan_consts(reverse):
    L, W, G = SCAN_L, SCAN_W, HEADS_PER_GROUP
    sgn = -1 if reverse else 1
    t_i = lax.broadcasted_iota(jnp.int32, (L, 3 * L), 0)
    s_i = lax.broadcasted_iota(jnp.int32, (L, 3 * L), 1) % L
    tri3 = jnp.where(sgn * (t_i - s_i) >= 0, 1.0, 0.0).astype(BF16)
    tc = lax.broadcasted_iota(jnp.int32, (L, G * L), 0)
    sc = lax.broadcasted_iota(jnp.int32, (L, G * L), 1) % L
    before = sgn * (tc - sc)
    strict = before > 0
    rmask = before > 0 if reverse else before >= 0
    eye = jnp.where(sc == tc, 1.0, 0.0).astype(F32)
    bi = lax.broadcasted_iota(jnp.int32, (W, W), 0) // RWKV_HEAD
    bj = lax.broadcasted_iota(jnp.int32, (W, W), 1) // RWKV_HEAD
    return tri3, strict, rmask, eye, bi == bj


def _bd(x, bdmask):
    return jnp.where(bdmask, jnp.tile(x, (HEADS_PER_GROUP, 1)), jnp.zeros((), x.dtype))


_NT = (((1,), (1,)), ((), ()))
_TN = (((0,), (0,)), ((), ()))


def _chunk_operators(r, v, kk, lw, k, b, consts, reverse):
    L, W, G = SCAN_L, SCAN_W, HEADS_PER_GROUP
    tri3, strict, rmask, eye, bdmask = consts
    bd = functools.partial(_bd, bdmask=bdmask)

    def dot(a_, b_):
        return jnp.dot(a_, b_, preferred_element_type=F32)

    hi = lw.astype(BF16)
    rem = lw - hi.astype(F32)
    mid = rem.astype(BF16)
    low = (rem - mid.astype(F32)).astype(BF16)
    cum = dot(tri3, jnp.concatenate([hi, mid, low], axis=0))
    yield
    tot = jnp.sum(lw, axis=0, keepdims=True)
    e_ex = jnp.exp(cum - lw)
    e_neg = jnp.exp(-cum)
    e_rem = jnp.exp(tot - cum)
    kq = kk * e_ex
    rq = r * (e_ex if reverse else jnp.exp(cum))
    vb = v.astype(BF16)
    q = jnp.concatenate([kq, rq], axis=0).astype(BF16)
    keys = jnp.concatenate([bd((k * e_neg).astype(BF16)), bd((b * e_neg).astype(BF16))], axis=0)
    a = lax.dot_general(q, keys, _NT, preferred_element_type=F32)
    yield
    a_kk = jnp.where(strict, a[:L, :G * L], 0.0)
    a_kb = jnp.where(strict, a[:L, G * L:], 0.0)
    a_rk = jnp.where(rmask, a[L:, :G * L], 0.0)
    a_rb = jnp.where(rmask, a[L:, G * L:], 0.0).astype(BF16)
    p = -a_kb
    tinv = eye + p
    av = dot(jnp.concatenate([a_kk, a_rk], axis=0).astype(BF16), bd(vb))
    p = dot(p.astype(BF16), bd(p.astype(BF16)))
    yield
    levels = L.bit_length() - 2
    for lvl in range(levels):
        wt = bd(p.astype(BF16))
        if lvl < levels - 1:
            tp = dot(jnp.concatenate([tinv, p], axis=0).astype(BF16), wt)
            tinv = tinv + tp[:L]
            p = tp[L:]
        else:
            tinv = tinv + dot(tinv.astype(BF16), wt)
        yield
    tinv = tinv.astype(BF16)
    tk = dot(tinv, jnp.concatenate([bd(kq.astype(BF16)), bd(av[:L].astype(BF16))], axis=1))
    yield
    kq2 = tk[:, :W].astype(BF16)
    cu = tk[:, W:].astype(BF16)
    ar = dot(a_rb, jnp.concatenate([bd(kq2), bd(cu)], axis=1))
    yield
    rq2 = (rq - ar[:, :W]).astype(BF16)
    yc = av[L:] - ar[:, W:]
    keb = (k * e_rem).astype(BF16)
    beb = (b * e_rem).astype(BF16)
    g = jnp.where(bdmask, lax.dot_general(kq2, beb, _TN, preferred_element_type=F32), 0.0).astype(BF16)
    c = jnp.where(bdmask, lax.dot_general(jnp.concatenate([vb, -cu], axis=0),
                                          jnp.concatenate([keb, beb], axis=0), _TN,
                                          preferred_element_type=F32), 0.0)
    return rq2, yc, jnp.exp(tot), g, c


def _run_in_lockstep(gens):
    results = [None] * len(gens)
    active = list(enumerate(gens))
    while active:
        still = []
        for i, gen in active:
            try:
                next(gen)
                still.append((i, gen))
            except StopIteration as done:
                results[i] = done.value
        active = still
    return results


def _scan_kernel(rf_ref, vf_ref, kkf_ref, lwf_ref, kf_ref, bf_ref, rb_ref, vb_ref, kkb_ref, lwb_ref, kb_ref,
                 bb_ref, yf_ref, yb_ref, s_ref, *, tb, segs, nblk, ng):
    L, W = SCAN_L, SCAN_W
    c = pl.program_id(1)
    start_f = _is_seq_start(c * tb, segs)
    start_b = _is_seq_start((nblk - c) * tb, segs)

    @pl.when(start_f)
    def _():
        s_ref[0] = jnp.zeros(s_ref.shape[1:], F32)

    @pl.when(start_b)
    def _():
        s_ref[1] = jnp.zeros(s_ref.shape[1:], F32)

    n_chunks = tb // L
    dirs = ((rf_ref, vf_ref, kkf_ref, lwf_ref, kf_ref, bf_ref, yf_ref),
            (rb_ref, vb_ref, kkb_ref, lwb_ref, kb_ref, bb_ref, yb_ref))
    streams = [(d, g) for d in range(2) for g in range(ng)]
    order = {d: [n_chunks - 1 - j if d == 1 else j for j in range(n_chunks)] for d in range(2)}
    consts = {d: _scan_consts(reverse=d == 1) for d in range(2)}
    gens = []
    for d, g in streams:
        r_ref, v_ref, kk_ref, lw_ref, k_ref, b_ref, _ = dirs[d]
        cols = slice(g * W, (g + 1) * W)
        for ci in order[d]:
            rows = slice(ci * L, (ci + 1) * L)
            gens.append(_chunk_operators(r_ref[rows, cols], v_ref[rows, cols], kk_ref[rows, cols],
                                         lw_ref[rows, cols], k_ref[rows, cols], b_ref[rows, cols],
                                         consts[d], reverse=d == 1))
    ops = _run_in_lockstep(gens)
    states = [s_ref[d, g] for d, g in streams]
    for j in range(n_chunks):
        for si, (d, g) in enumerate(streams):
            rq2, yc, decay, gmat, cmat = ops[si * n_chunks + j]
            ci = order[d][j]
            sb = states[si].astype(BF16)
            dirs[d][6][ci * L:(ci + 1) * L, g * W:(g + 1) * W] = (
                lax.dot_general(rq2, sb, _NT, preferred_element_type=F32) + yc)
            states[si] = states[si] * decay - jnp.dot(sb, gmat, preferred_element_type=F32) + cmat
    for si, (d, g) in enumerate(streams):
        s_ref[d, g] = states[si]


def _rwkv_scan(r, v, kk, lw, k, b, segs, tb=256, ng=1):
    t, dr = r.shape
    tb = min(tb, t)
    nblk = t // tb
    w = ng * SCAN_W
    fwd = pl.BlockSpec((tb, w), lambda g, c: (c, g))
    bwd = pl.BlockSpec((tb, w), lambda g, c: (nblk - 1 - c, g))
    fwd2 = pl.BlockSpec((None, tb, w), lambda g, c: (0, c, g))
    bwd2 = pl.BlockSpec((None, tb, w), lambda g, c: (1, nblk - 1 - c, g))
    out = jax.ShapeDtypeStruct((t, dr), F32)
    return pl.pallas_call(
        functools.partial(_scan_kernel, tb=tb, segs=segs, nblk=nblk, ng=ng),
        grid=(dr // w, nblk),
        in_specs=[fwd, fwd, fwd, fwd2, fwd2, fwd2, bwd, bwd, bwd, bwd2, bwd2, bwd2],
        out_specs=[fwd, bwd],
        out_shape=[out, out],
        scratch_shapes=[pltpu.VMEM((2, ng, SCAN_W, SCAN_W), F32)],
        compiler_params=_params("parallel", "arbitrary"),
        name="rwkv_scan",
    )(r, v, kk, lw, k, b, r, v, kk, lw, k, b)


def _post_kernel(yf_ref, yb_ref, g_ref, bonus_ref, gng_ref, gnb_ref, ones_ref, o_ref):
    y = yf_ref[...] + yb_ref[...]
    ones_bd = ones_ref[...]
    mu = _head_sum(y, ones_bd) * (1.0 / RWKV_HEAD)
    cen = y - mu
    var = _head_sum(cen * cen, ones_bd) * (1.0 / RWKV_HEAD)
    yn = cen * lax.rsqrt(var + GN_EPS) * gng_ref[...] + gnb_ref[...]
    o_ref[...] = ((yn + bonus_ref[...]) * g_ref[...]).astype(o_ref.dtype)


def _rwkv_post(y_f, y_b, g, bonus, gn_g, gn_b, ones_bd, tb=256):
    t, dr = y_f.shape
    tb = min(tb, t)
    tok = pl.BlockSpec((tb, dr), lambda i: (i, 0))
    vec = pl.BlockSpec((1, dr), lambda i: (0, 0))
    return pl.pallas_call(
        _post_kernel,
        grid=(t // tb,),
        in_specs=[tok, tok, tok, tok, vec, vec, pl.BlockSpec((SCAN_W, SCAN_W), lambda i: (0, 0))],
        out_specs=tok,
        out_shape=jax.ShapeDtypeStruct((t, dr), BF16),
        compiler_params=_params("parallel"),
        name="rwkv_post",
    )(y_f, y_b, g, bonus, gn_g.reshape(1, dr), gn_b.reshape(1, dr), ones_bd)


def _ln_kernel(x_ref, m_ref, g_ref, b_ref, o_ref, ob_ref, *, alpha):
    h = alpha * x_ref[...] + m_ref[...]
    mu = jnp.mean(h, axis=-1, keepdims=True)
    cen = h - mu
    var = jnp.mean(cen * cen, axis=-1, keepdims=True)
    y = cen * lax.rsqrt(var + LN_EPS) * g_ref[...] + b_ref[...]
    o_ref[...] = y
    ob_ref[...] = y.astype(BF16)


def _residual_ln(x, mix, g, b, alpha, tb=256):
    t, dm = x.shape
    tb = min(tb, t)
    tok = pl.BlockSpec((tb, dm), lambda i: (i, 0))
    vec = pl.BlockSpec((1, dm), lambda i: (0, 0))
    return pl.pallas_call(
        functools.partial(_ln_kernel, alpha=alpha),
        grid=(t // tb,),
        in_specs=[tok, tok, vec, vec],
        out_specs=[tok, tok],
        out_shape=[jax.ShapeDtypeStruct((t, dm), F32), jax.ShapeDtypeStruct((t, dm), BF16)],
        compiler_params=_params("parallel"),
        name="residual_ln",
    )(x, mix, g.reshape(1, dm), b.reshape(1, dm))


def _ffn_up_kernel(x_ref, wg_ref, wu_ref, h_ref):
    x = x_ref[...]
    a = jnp.dot(x, wg_ref[...], preferred_element_type=F32)
    u = jnp.dot(x, wu_ref[...], preferred_element_type=F32)
    h_ref[...] = (a * jax.nn.sigmoid(a) * u).astype(h_ref.dtype)


def _ffn_up(xb, wg, wu, tm=1024, tn=512):
    t, dm = xb.shape
    f = wg.shape[1]
    tm = min(tm, t)
    wspec = pl.BlockSpec((dm, tn), lambda i, j: (0, j))
    return pl.pallas_call(
        _ffn_up_kernel,
        grid=(t // tm, f // tn),
        in_specs=[pl.BlockSpec((tm, dm), lambda i, j: (i, 0)), wspec, wspec],
        out_specs=pl.BlockSpec((tm, tn), lambda i, j: (i, j)),
        out_shape=jax.ShapeDtypeStruct((t, f), BF16),
        compiler_params=_params("parallel", "parallel"),
        name="ffn_up",
    )(xb, wg, wu)


def _router_kernel(x_ref, w_ref, idx_ref, wt_ref):
    logits = jnp.dot(x_ref[...], w_ref[...], precision=HIGHEST, preferred_element_type=F32)
    lane = lax.broadcasted_iota(jnp.int32, logits.shape, 1)
    neg = jnp.float32(-jnp.inf)
    logits = jnp.where(lane < N_EXPERTS, logits, neg)
    m1 = jnp.max(logits, axis=1, keepdims=True)
    i1 = jnp.min(jnp.where(logits == m1, lane, LANES), axis=1, keepdims=True)
    rest = jnp.where(lane == i1, neg, logits)
    m2 = jnp.max(rest, axis=1, keepdims=True)
    i2 = jnp.min(jnp.where(rest == m2, lane, LANES), axis=1, keepdims=True)
    e2 = jnp.exp(m2 - m1)
    w1 = 1.0 / (1.0 + e2)
    w2 = e2 / (1.0 + e2)
    idx_ref[...] = jnp.where(lane == 0, i1, jnp.where(lane == 1, i2, 0))
    wt_ref[...] = jnp.where(lane == 0, w1, jnp.where(lane == 1, w2, 0.0))


def _router(x, router_w, tb=512):
    t, dm = x.shape
    tb = min(tb, t)
    w = jnp.pad(router_w, ((0, 0), (0, LANES - router_w.shape[1])))
    out = pl.BlockSpec((tb, LANES), lambda i: (i, 0))
    return pl.pallas_call(
        _router_kernel,
        grid=(t // tb,),
        in_specs=[pl.BlockSpec((tb, dm), lambda i: (i, 0)), pl.BlockSpec((dm, LANES), lambda i: (0, 0))],
        out_specs=[out, out],
        out_shape=[jax.ShapeDtypeStruct((t, LANES), jnp.int32), jax.ShapeDtypeStruct((t, LANES), F32)],
        compiler_params=_params("parallel"),
        name="router",
    )(x, w)


MOE_TM = 512


def _moe_dispatch(idx, wts, tm):
    t = idx.shape[0]
    n_rows = 2 * t + N_EXPERTS * tm
    n_tiles = n_rows // tm
    e_ids = idx[:, :2]
    per_tok = (e_ids[:, :, None] == jnp.arange(N_EXPERTS)).astype(jnp.int32).sum(axis=1)
    csum = jnp.cumsum(per_tok, axis=0)
    counts = csum[-1]
    rank = csum - per_tok
    padded = (counts + tm - 1) // tm * tm
    ends = jnp.cumsum(padded)
    offs = ends - padded
    pos = offs[e_ids] + jnp.take_along_axis(rank, e_ids, axis=1)
    flat = pos.reshape(-1)
    tok = jnp.repeat(jnp.arange(t, dtype=jnp.int32), 2)
    src = jnp.zeros((n_rows,), jnp.int32).at[flat].set(tok)
    row_w = jnp.zeros((n_rows,), F32).at[flat].set(wts[:, :2].reshape(-1))
    tile_start = jnp.arange(n_tiles, dtype=jnp.int32) * tm
    owner = jnp.sum(tile_start[:, None] >= ends[None, :], axis=1).astype(jnp.int32)
    tile_valid = (owner < N_EXPERTS).astype(jnp.int32)
    tile_expert = jnp.minimum(owner, N_EXPERTS - 1)
    return src, row_w, pos.astype(jnp.int32), tile_expert, tile_valid


def _gather_rows(idx_ref, n, src_hbm, dst_ref, sem):
    def issue(r, carry):
        pltpu.make_async_copy(src_hbm.at[pl.ds(idx_ref[0, r], 1), :], dst_ref.at[pl.ds(r, 1), :], sem).start()
        return carry

    lax.fori_loop(0, n, issue, 0)
    pltpu.make_async_copy(src_hbm.at[pl.ds(0, n), :], dst_ref.at[pl.ds(0, n), :], sem).wait()


def _moe_up_sparse_kernel(te_ref, tv_ref, src_ref, x_hbm, wg_ref, wu_ref, roww_ref, h_ref, xbuf, xb_buf, sem, *, tm):
    i = pl.program_id(0)
    valid = tv_ref[i] == 1

    @pl.when(jnp.logical_and(valid, pl.program_id(1) == 0))
    def _():
        _gather_rows(src_ref, tm, x_hbm, xbuf, sem)
        xb_buf[...] = xbuf[...].astype(BF16)

    @pl.when(valid)
    def _():
        x = xb_buf[...]
        a = jnp.dot(x, wg_ref[...], preferred_element_type=F32)
        u = jnp.dot(x, wu_ref[...], preferred_element_type=F32)
        h_ref[...] = (a * jax.nn.sigmoid(a) * u * roww_ref[...]).astype(h_ref.dtype)

    @pl.when(jnp.logical_not(valid))
    def _():
        h_ref[...] = jnp.zeros_like(h_ref)


def _moe_up_sparse(x, we_gate, we_up, src, row_w, tile_expert, tile_valid, tm, tn=512):
    t, dm = x.shape
    n_e, _, fe = we_gate.shape
    n_rows = src.shape[0]
    n_tiles = n_rows // tm
    wspec = pl.BlockSpec((None, dm, tn), lambda i, j, te, tv: (te[i], 0, j))
    grid_spec = pltpu.PrefetchScalarGridSpec(
        num_scalar_prefetch=2,
        grid=(n_tiles, fe // tn),
        in_specs=[pl.BlockSpec((None, 1, tm), lambda i, j, te, tv: (i, 0, 0), memory_space=pltpu.SMEM),
                  pl.BlockSpec(memory_space=pl.ANY),
                  wspec, wspec,
                  pl.BlockSpec((tm, 1), lambda i, j, te, tv: (i, 0))],
        out_specs=pl.BlockSpec((tm, tn), lambda i, j, te, tv: (i, j)),
        scratch_shapes=[pltpu.VMEM((tm, dm), F32), pltpu.VMEM((tm, dm), BF16), pltpu.SemaphoreType.DMA(())],
    )
    return pl.pallas_call(
        functools.partial(_moe_up_sparse_kernel, tm=tm),
        grid_spec=grid_spec,
        out_shape=jax.ShapeDtypeStruct((n_rows, fe), BF16),
        compiler_params=_params("arbitrary", "arbitrary"),
        name="moe_up_sparse",
    )(tile_expert, tile_valid, src.reshape(n_tiles, 1, tm), x, we_gate, we_up, row_w.reshape(n_rows, 1))


def _moe_down_kernel(te_ref, tv_ref, h_ref, w_ref, y_ref):
    valid = tv_ref[pl.program_id(0)] == 1

    @pl.when(valid)
    def _():
        y_ref[...] = jnp.dot(h_ref[...], w_ref[...], preferred_element_type=F32)

    @pl.when(jnp.logical_not(valid))
    def _():
        y_ref[...] = jnp.zeros_like(y_ref)


def _moe_down_sparse(h, we_down, tile_expert, tile_valid, tm, tn=1024):
    n_rows, fe = h.shape
    dm = we_down.shape[2]
    grid_spec = pltpu.PrefetchScalarGridSpec(
        num_scalar_prefetch=2,
        grid=(n_rows // tm, dm // tn),
        in_specs=[pl.BlockSpec((tm, fe), lambda i, j, te, tv: (i, 0)),
                  pl.BlockSpec((None, fe, tn), lambda i, j, te, tv: (te[i], 0, j))],
        out_specs=pl.BlockSpec((tm, tn), lambda i, j, te, tv: (i, j)),
    )
    return pl.pallas_call(
        _moe_down_kernel,
        grid_spec=grid_spec,
        out_shape=jax.ShapeDtypeStruct((n_rows, dm), F32),
        compiler_params=_params("parallel", "parallel"),
        name="moe_down_sparse",
    )(tile_expert, tile_valid, h, we_down)


def _moe_combine_ln_kernel(pos_ref, y_hbm, x_ref, g_ref, b_ref, o_ref, ob_ref, ybuf, sem, *, alpha, tb):
    _gather_rows(pos_ref, 2 * tb, y_hbm, ybuf, sem)
    h = alpha * x_ref[...] + (ybuf[0:tb, :] + ybuf[tb:2 * tb, :])
    mu = jnp.mean(h, axis=-1, keepdims=True)
    cen = h - mu
    var = jnp.mean(cen * cen, axis=-1, keepdims=True)
    y = cen * lax.rsqrt(var + LN_EPS) * g_ref[...] + b_ref[...]
    o_ref[...] = y
    ob_ref[...] = y.astype(BF16)


def _moe_combine_ln(x, ys, pos, g, b, alpha, tb=256):
    t, dm = x.shape
    tb = min(tb, t)
    tok = pl.BlockSpec((tb, dm), lambda i: (i, 0))
    vec = pl.BlockSpec((1, dm), lambda i: (0, 0))
    return pl.pallas_call(
        functools.partial(_moe_combine_ln_kernel, alpha=alpha, tb=tb),
        grid=(t // tb,),
        in_specs=[pl.BlockSpec((None, 1, 2 * tb), lambda i: (i, 0, 0), memory_space=pltpu.SMEM),
                  pl.BlockSpec(memory_space=pl.ANY), tok, vec, vec],
        out_specs=[tok, tok],
        out_shape=[jax.ShapeDtypeStruct((t, dm), F32), jax.ShapeDtypeStruct((t, dm), BF16)],
        scratch_shapes=[pltpu.VMEM((2 * tb, dm), F32), pltpu.SemaphoreType.DMA(())],
        compiler_params=_params("arbitrary"),
        name="moe_combine_ln",
    )(jnp.concatenate([pos[:, 0].reshape(t // tb, 1, tb), pos[:, 1].reshape(t // tb, 1, tb)], axis=2),
      ys, x, g.reshape(1, dm), b.reshape(1, dm))


def _ple_kernel(xb_ref, wg_ref, p_ref, wp_ref, x_ref, o_ref, ob_ref):
    gate = jax.nn.sigmoid(jnp.dot(xb_ref[...], wg_ref[...], preferred_element_type=F32))
    emb = jnp.dot(p_ref[...], wp_ref[...], preferred_element_type=F32)
    y = x_ref[...] + gate * emb
    o_ref[...] = y
    ob_ref[...] = y.astype(BF16)


def _ple(xb, x, w_pgate, p, w_pproj, tm=1024, tn=512):
    t, dm = xb.shape
    pd = p.shape[1]
    tm = min(tm, t)
    out = pl.BlockSpec((tm, tn), lambda i, j: (i, j))
    return pl.pallas_call(
        _ple_kernel,
        grid=(t // tm, dm // tn),
        in_specs=[pl.BlockSpec((tm, dm), lambda i, j: (i, 0)),
                  pl.BlockSpec((dm, tn), lambda i, j: (0, j)),
                  pl.BlockSpec((tm, pd), lambda i, j: (i, 0)),
                  pl.BlockSpec((pd, tn), lambda i, j: (0, j)),
                  out],
        out_specs=[out, out],
        out_shape=[jax.ShapeDtypeStruct((t, dm), F32), jax.ShapeDtypeStruct((t, dm), BF16)],
        compiler_params=_params("parallel", "parallel"),
        name="ple_gate",
    )(xb, w_pgate, p, w_pproj, x)


def _pad_cols(w, width):
    return jnp.pad(w, ((0, 0), (0, width - w.shape[1])))


def _pad_rows(w, rows):
    pad = [(0, 0)] * w.ndim
    pad[-2] = (0, rows - w.shape[-2])
    return jnp.pad(w, pad)


def _split_rwkv_cols(w, dg, dr, has_mv):
    base = 2 * dg
    rkv = w[:, base:base + 3 * dr]
    o = base + 3 * dr
    sizes = [DECAY_LORA, DECAY_LORA, AAA_LORA, AAA_LORA, GATE_LORA] + ([MV_LORA] if has_mv else [])
    slots = [LORA_SLOT, LORA_SLOT, LORA_SLOT, LORA_SLOT, GATE_LORA, LORA_SLOT]
    parts = []
    for n, slot in zip(sizes, slots):
        parts.append(_pad_cols(w[:, o:o + n], slot))
        o += n
    lora = _pad_cols(jnp.concatenate(parts, axis=1), LORA_COLS)
    return jnp.concatenate([rkv, lora], axis=1)


def _trunk(x, p, P, segs):
    t, dm = x.shape
    depth = P['w_out'].shape[0]
    alpha = (2 * depth) ** 0.25
    dg = P['sgu_ln_g'].shape[1]
    dr = P['k_k'].shape[1]
    hb = lax.broadcasted_iota(jnp.int32, (SCAN_W, SCAN_W), 0) // RWKV_HEAD
    hc = lax.broadcasted_iota(jnp.int32, (SCAN_W, SCAN_W), 1) // RWKV_HEAD
    ones_bd = (hb == hc).astype(BF16)
    xb = x.astype(BF16)
    v_first = None
    for l in range(depth):
        if l == 0:
            w_in_l, conv_l, v0_l, v2_l = P['w_in0'], P['conv0'], None, None
        else:
            w_in_l, conv_l, v0_l, v2_l = P['w_in'][l - 1], P['conv'][l - 1], P['v0'][l - 1], P['v2'][l - 1]
        uv = _matmul(xb, w_in_l[:, :2 * dg].astype(BF16), BF16)
        zr = _matmul(xb, _split_rwkv_cols(w_in_l, dg, dr, l > 0).astype(BF16), F32)
        y_g = _spatial_gating(uv, P['sgu_ln_g'][l], P['sgu_ln_b'][l], P['w_s'][l], P['b_s'][l])
        conv_r = _split_rwkv_cols(jnp.pad(conv_l, ((0, 0), (2 * dg, 0))), dg, dr, l > 0)
        prep_args = (zr, conv_r, P['w0'][l], _pad_rows(P['w2'][l], LORA_SLOT).astype(BF16), P['a0'][l],
                     _pad_rows(P['a2'][l], LORA_SLOT).astype(BF16), P['g2'][l].astype(BF16),
                     P['k_k'][l], P['k_a'][l], P['r_k'][l].reshape(-1), ones_bd, segs)
        if l == 0:
            r, v, kk, g, bonus, lw, k, b = _rwkv_prep(*prep_args)
            v_first = v
        else:
            r, v, kk, g, bonus, lw, k, b = _rwkv_prep(*prep_args, v_first=v_first, v0=v0_l,
                                                      v2=_pad_rows(v2_l, LORA_SLOT).astype(BF16))
        y_f, y_b = _rwkv_scan(r, v, kk, lw, k, b, segs)
        y_r = _rwkv_post(y_f, y_b, g, bonus, P['gn_g'][l], P['gn_b'][l], ones_bd)
        mixed = jnp.concatenate([y_g, y_r], axis=1)
        mix = _matmul(mixed, P['w_out'][l].astype(BF16), F32)
        x, xb = _residual_ln(x, mix, P['ln1_g'][l], P['ln1_b'][l], alpha)
        j = l // 2
        if l % 2 == 0:
            h = _ffn_up(xb, P['w_ff_gate'][j].astype(BF16), P['w_ff_up'][j].astype(BF16))
            ff = _matmul(h, P['w_ff_down'][j].astype(BF16), F32, tk=2048)
        else:
            idx, wts = _router(x, P['router'][j])
            tm = min(MOE_TM, t)
            src, row_w, pos, tile_expert, tile_valid = _moe_dispatch(idx, wts, tm)
            h = _moe_up_sparse(x, P['we_gate'][j].astype(BF16), P['we_up'][j].astype(BF16), src, row_w,
                               tile_expert, tile_valid, tm)
            ys = _moe_down_sparse(h, P['we_down'][j].astype(BF16), tile_expert, tile_valid, tm)
            x, xb = _moe_combine_ln(x, ys, pos, P['ln2_g'][l], P['ln2_b'][l], alpha)
        if l % 2 == 0:
            x, xb = _residual_ln(x, ff, P['ln2_g'][l], P['ln2_b'][l], alpha)
        x, xb = _ple(xb, x, P['w_pgate'][l].astype(BF16), p[l].astype(BF16), P['w_pproj'][l].astype(BF16))
    return x


def kernel(x_prompt, x_sample, p_prompt, p_sample, w_in0, conv0, w_in, conv, sgu_ln_g, sgu_ln_b, w_s, b_s, w0, w2, a0, a2, g2, k_k, k_a, r_k, gn_g, gn_b, v0, v2, w_out, ln1_g, ln1_b, ln2_g, ln2_b, w_ff_gate, w_ff_up, w_ff_down, router, we_gate, we_up, we_down, w_pproj, w_pgate):
    P = dict(w_in0=w_in0, conv0=conv0, w_in=w_in, conv=conv, sgu_ln_g=sgu_ln_g, sgu_ln_b=sgu_ln_b,
             w_s=w_s, b_s=b_s, w0=w0, w2=w2, a0=a0, a2=a2, g2=g2, k_k=k_k, k_a=k_a, r_k=r_k,
             gn_g=gn_g, gn_b=gn_b, v0=v0, v2=v2, w_out=w_out, ln1_g=ln1_g, ln1_b=ln1_b,
             ln2_g=ln2_g, ln2_b=ln2_b, w_ff_gate=w_ff_gate, w_ff_up=w_ff_up, w_ff_down=w_ff_down,
             router=router, we_gate=we_gate, we_up=we_up, we_down=we_down,
             w_pproj=w_pproj, w_pgate=w_pgate)
    b1, s1, dm = x_prompt.shape
    b2, s2, _ = x_sample.shape
    depth, pd = p_prompt.shape[0], p_prompt.shape[-1]
    t1, t2 = b1 * s1, b2 * s2
    segs = ((t1, s1), (t2, s2))
    x = jnp.concatenate([x_prompt.reshape(t1, dm), x_sample.reshape(t2, dm)], axis=0)
    p = jnp.concatenate([p_prompt.reshape(depth, t1, pd), p_sample.reshape(depth, t2, pd)], axis=1)
    y = _trunk(x, p, P, segs)
    return y[:t1].reshape(b1, s1, dm), y[t1:].reshape(b2, s2, dm)
```

```python
import functools

import jax
import jax.numpy as jnp
from jax import lax
from jax.experimental import pallas as pl
from jax.experimental.pallas import tpu as pltpu

F32 = jnp.float32
BF16 = jnp.bfloat16
HIGHEST = lax.Precision.HIGHEST

LANES = 128
BF16_SUBLANES = 16
VMEM_LIMIT_BYTES = 56 * 1024 * 1024

CHUNK = 128
GMLP_HEAD = 128
RWKV_HEAD = 64
DECAY_LORA = 96
AAA_LORA = 96
MV_LORA = 64
GATE_LORA = 256
N_EXPERTS = 8
LN_EPS = 1e-5
GN_EPS = 64e-5
L2_EPS = 1e-12

SCAN_L = 64
SCAN_W = 256
HEADS_PER_GROUP = SCAN_W // RWKV_HEAD
LORA_SLOT = 128
LORA_COLS = 1024


def _params(*sem):
    return pltpu.CompilerParams(dimension_semantics=sem, vmem_limit_bytes=VMEM_LIMIT_BYTES)


def _is_seq_start(pos, segs):
    (t1, s1), (_, s2) = segs
    return jnp.where(pos < t1, pos % s1 == 0, (pos - t1) % s2 == 0)


CAST_BLOCK_BYTES = 8 * 1024 * 1024


def _cast_kernel(x_ref, o_ref):
    o_ref[...] = x_ref[...].astype(o_ref.dtype)


def _to_bf16(w):
    w2 = w.reshape(-1, w.shape[-1])
    rows, n = w2.shape
    tb = rows
    while tb * n * w2.dtype.itemsize > CAST_BLOCK_BYTES and tb % 2 == 0 and (tb // 2) % BF16_SUBLANES == 0:
        tb //= 2
    out = pl.pallas_call(
        _cast_kernel,
        grid=(rows // tb,),
        in_specs=[pl.BlockSpec((tb, n), lambda i: (i, 0))],
        out_specs=pl.BlockSpec((tb, n), lambda i: (i, 0)),
        out_shape=jax.ShapeDtypeStruct((rows, n), BF16),
        compiler_params=_params("parallel"),
        name="cast_bf16",
    )(w2)
    return out.reshape(w.shape)


def _mm_kernel(x_ref, w_ref, o_ref, *scratch, nk):
    if nk == 1:
        o_ref[...] = jnp.dot(x_ref[...], w_ref[...], preferred_element_type=F32).astype(o_ref.dtype)
        return
    acc_ref, = scratch
    k = pl.program_id(2)

    @pl.when(k == 0)
    def _():
        acc_ref[...] = jnp.zeros_like(acc_ref)

    acc_ref[...] += jnp.dot(x_ref[...], w_ref[...], preferred_element_type=F32)

    @pl.when(k == nk - 1)
    def _():
        o_ref[...] = acc_ref[...].astype(o_ref.dtype)


def _matmul(x, w, out_dtype, tm=1024, tn=1024, tk=None):
    m, kdim = x.shape
    n = w.shape[1]
    tm, tn = min(tm, m), min(tn, n)
    tk = kdim if tk is None else min(tk, kdim)
    nk = kdim // tk
    assert m % tm == 0 and n % tn == 0 and kdim % tk == 0
    scratch = [] if nk == 1 else [pltpu.VMEM((tm, tn), F32)]
    return pl.pallas_call(
        functools.partial(_mm_kernel, nk=nk),
        grid=(m // tm, n // tn, nk),
        in_specs=[pl.BlockSpec((tm, tk), lambda i, j, k: (i, k)),
                  pl.BlockSpec((tk, tn), lambda i, j, k: (k, j))],
        out_specs=pl.BlockSpec((tm, tn), lambda i, j, k: (i, j)),
        out_shape=jax.ShapeDtypeStruct((m, n), out_dtype),
        scratch_shapes=scratch,
        compiler_params=_params("parallel", "parallel", "arbitrary"),
        name="matmul",
    )(x, w)


def _mm_pair_kernel(xa_ref, wa_ref, xb_ref, wb_ref, o_ref):
    acc = jnp.dot(xa_ref[...], wa_ref[...], preferred_element_type=F32)
    acc += jnp.dot(xb_ref[...], wb_ref[...], preferred_element_type=F32)
    o_ref[...] = acc.astype(o_ref.dtype)


def _matmul_pair(xa, wa, xb, wb, out_dtype, tm=1024, tn=1024):
    m, ka = xa.shape
    kb = xb.shape[1]
    n = wa.shape[1]
    tm, tn = min(tm, m), min(tn, n)
    return pl.pallas_call(
        _mm_pair_kernel,
        grid=(m // tm, n // tn),
        in_specs=[pl.BlockSpec((tm, ka), lambda i, j: (i, 0)),
                  pl.BlockSpec((ka, tn), lambda i, j: (0, j)),
                  pl.BlockSpec((tm, kb), lambda i, j: (i, 0)),
                  pl.BlockSpec((kb, tn), lambda i, j: (0, j))],
        out_specs=pl.BlockSpec((tm, tn), lambda i, j: (i, j)),
        out_shape=jax.ShapeDtypeStruct((m, n), out_dtype),
        compiler_params=_params("parallel", "parallel"),
        name="matmul_pair",
    )(xa, wa, xb, wb)


def _sgu_kernel(u_ref, v_ref, g_ref, b_ref, ws_ref, bs_ref, o_ref, *, n_chunks):
    gv = jax.nn.gelu(v_ref[...].astype(F32))
    mu = jnp.mean(gv, axis=-1, keepdims=True)
    cen = gv - mu
    var = jnp.mean(cen * cen, axis=-1, keepdims=True)
    vn = (cen * lax.rsqrt(var + LN_EPS) * g_ref[...] + b_ref[...]).astype(BF16)
    n_heads = ws_ref.shape[0]
    for c in range(n_chunks):
        rows = slice(c * CHUNK, (c + 1) * CHUNK)
        for h in range(n_heads):
            cols = slice(h * GMLP_HEAD, (h + 1) * GMLP_HEAD)
            sv = jnp.dot(ws_ref[h], vn[rows, cols], preferred_element_type=F32) + bs_ref[:, cols]
            gu = jax.nn.gelu(u_ref[rows, cols].astype(F32))
            o_ref[rows, cols] = (gu * sv).astype(o_ref.dtype)


def _spatial_gating(uv, ln_g, ln_b, w_s, b_s, tb=512):
    t = uv.shape[0]
    dg = uv.shape[1] // 2
    tb = min(tb, t)
    n_heads = w_s.shape[0]
    bs_full = jnp.repeat(b_s.T, GMLP_HEAD, axis=1)
    return pl.pallas_call(
        functools.partial(_sgu_kernel, n_chunks=tb // CHUNK),
        grid=(t // tb,),
        in_specs=[pl.BlockSpec((tb, dg), lambda i: (i, 0)),
                  pl.BlockSpec((tb, dg), lambda i: (i, 1)),
                  pl.BlockSpec((1, dg), lambda i: (0, 0)),
                  pl.BlockSpec((1, dg), lambda i: (0, 0)),
                  pl.BlockSpec((n_heads, CHUNK, CHUNK), lambda i: (0, 0, 0)),
                  pl.BlockSpec((CHUNK, dg), lambda i: (0, 0))],
        out_specs=pl.BlockSpec((tb, dg), lambda i: (i, 0)),
        out_shape=jax.ShapeDtypeStruct((t, dg), BF16),
        compiler_params=_params("parallel"),
        name="spatial_gating",
    )(uv, uv, ln_g.reshape(1, dg), ln_b.reshape(1, dg), w_s.astype(BF16), bs_full)


def _head_sum(x, ones_bd):
    parts = []
    for g in range(x.shape[1] // SCAN_W):
        xs = x[:, g * SCAN_W:(g + 1) * SCAN_W]
        hi = xs.astype(BF16)
        lo = (xs - hi.astype(F32)).astype(BF16)
        parts.append(jnp.dot(hi, ones_bd, preferred_element_type=F32)
                     + jnp.dot(lo, ones_bd, preferred_element_type=F32))
    return jnp.concatenate(parts, axis=1)


def _prep_kernel(*refs, layer1, tb, segs, dr):
    if layer1:
        (z_ref, zp_ref, zn_ref, conv_ref, w0_ref, w2_ref, a0_ref, a2_ref, g2_ref, kk_ref, ka_ref, rk_ref,
         ones_ref, vf_ref, v0_ref, v2_ref, r_o, v_o, kkn_o, g_o, bonus_o, lw_o, k_o, b_o) = refs
    else:
        (z_ref, zp_ref, zn_ref, conv_ref, w0_ref, w2_ref, a0_ref, a2_ref, g2_ref, kk_ref, ka_ref, rk_ref,
         ones_ref, r_o, v_o, kkn_o, g_o, bonus_o, lw_o, k_o, b_o) = refs
    pos0 = pl.program_id(0) * tb
    first = _is_seq_start(pos0, segs)
    last = _is_seq_start(pos0 + tb, segs)
    row = lax.broadcasted_iota(jnp.int32, (tb, 1), 0)

    def conv(c0, width):
        cols = slice(c0, c0 + width)
        z = z_ref[:, cols].astype(F32)
        prev_row = jnp.where(first, 0.0, zp_ref[:, cols].astype(F32)[BF16_SUBLANES - 1:BF16_SUBLANES])
        next_row = jnp.where(last, 0.0, zn_ref[:, cols].astype(F32)[0:1])
        zprev = jnp.where(row == 0, prev_row, pltpu.roll(z, 1, 0))
        znext = jnp.where(row == tb - 1, next_row, pltpu.roll(z, tb - 1, 0))
        return zprev * conv_ref[0:1, cols] + z * conv_ref[1:2, cols] + znext * conv_ref[2:3, cols]

    lo = conv(3 * dr, LORA_COLS)
    s = LORA_SLOT
    wd = (lo[:, 0:s], lo[:, s:2 * s])
    ad = (lo[:, 2 * s:3 * s], lo[:, 3 * s:4 * s])
    gd = lo[:, 4 * s:6 * s]

    def lora(x, w):
        return jnp.dot(x.astype(BF16), w, preferred_element_type=F32)

    g_o[...] = lora(jax.nn.sigmoid(gd), g2_ref[...]).astype(g_o.dtype)
    v = conv(2 * dr, dr)
    if layer1:
        mv = lo[:, 6 * s:7 * s]
        v = v + (vf_ref[...].astype(F32) - v) * jax.nn.sigmoid(v0_ref[...] + lora(mv, v2_ref[...]))
    v_o[...] = v.astype(v_o.dtype)
    r = conv(0, dr)
    r_o[...] = r.astype(r_o.dtype)
    k = conv(dr, dr)
    ones_bd = ones_ref[...]
    kk = k * kk_ref[...]
    kk = kk * lax.rsqrt(_head_sum(kk * kk, ones_bd) + L2_EPS)
    kkn_o[...] = kk.astype(kkn_o.dtype)
    for d in range(2):
        zw = w0_ref[d:d + 1, :] + lora(jnp.tanh(wd[d]), w2_ref[d])
        w_log = -(jnp.maximum(-zw, 0.0) + jnp.log(1.0 + jnp.exp(-jnp.abs(zw)))) - 0.5
        lw_o[d] = -jnp.exp(w_log)
        a = jax.nn.sigmoid(a0_ref[d:d + 1, :] + lora(ad[d], a2_ref[d]))
        k_d = k * (1.0 + (a - 1.0) * ka_ref[...])
        k_o[d] = k_d.astype(k_o.dtype)
        b_o[d] = (kk * a).astype(b_o.dtype)
        if d == 0:
            bonus_o[...] = (_head_sum(r * k_d * rk_ref[...], ones_bd) * v).astype(bonus_o.dtype)


def _rwkv_prep(zr, conv_w, w0, w2, a0, a2, g2, k_k, k_a, r_k, ones_bd, segs, v_first=None, v0=None, v2=None,
               tb=128):
    t, c = zr.shape
    dr = k_k.shape[-1]
    tb = min(tb, t)
    layer1 = v_first is not None
    hb = tb // BF16_SUBLANES
    n_halo = t // BF16_SUBLANES
    const2 = lambda i: (0, 0)
    const3 = lambda i: (0, 0, 0)
    in_specs = [pl.BlockSpec((tb, c), lambda i: (i, 0)),
                pl.BlockSpec((BF16_SUBLANES, c), lambda i: (jnp.maximum(i * hb - 1, 0), 0)),
                pl.BlockSpec((BF16_SUBLANES, c), lambda i: (jnp.minimum((i + 1) * hb, n_halo - 1), 0)),
                pl.BlockSpec((3, c), const2),
                pl.BlockSpec((2, dr), const2),
                pl.BlockSpec((2, LORA_SLOT, dr), const3),
                pl.BlockSpec((2, dr), const2),
                pl.BlockSpec((2, LORA_SLOT, dr), const3),
                pl.BlockSpec((GATE_LORA, dr), const2),
                pl.BlockSpec((1, dr), const2),
                pl.BlockSpec((1, dr), const2),
                pl.BlockSpec((1, dr), const2),
                pl.BlockSpec((SCAN_W, SCAN_W), const2)]
    args = [zr, zr, zr, conv_w, w0, w2, a0, a2, g2, k_k.reshape(1, dr), k_a.reshape(1, dr), r_k.reshape(1, dr),
            ones_bd]
    if layer1:
        in_specs += [pl.BlockSpec((tb, dr), lambda i: (i, 0)),
                     pl.BlockSpec((1, dr), const2),
                     pl.BlockSpec((LORA_SLOT, dr), const2)]
        args += [v_first, v0.reshape(1, dr), v2]
    tok = pl.BlockSpec((tb, dr), lambda i: (i, 0))
    tok2 = pl.BlockSpec((2, tb, dr), lambda i: (0, i, 0))
    one = jax.ShapeDtypeStruct((t, dr), BF16)
    two = jax.ShapeDtypeStruct((2, t, dr), BF16)
    log_decay = jax.ShapeDtypeStruct((2, t, dr), F32)
    return pl.pallas_call(
        functools.partial(_prep_kernel, layer1=layer1, tb=tb, segs=segs, dr=dr),
        grid=(t // tb,),
        in_specs=in_specs,
        out_specs=[tok, tok, tok, tok, tok, tok2, tok2, tok2],
        out_shape=[one, one, one, one, one, log_decay, two, two],
        compiler_params=_params("parallel"),
        name="rwkv_prep",
    )(*args)


def _scan_consts(reverse):
    L, W, G = SCAN_L, SCAN_W, HEADS_PER_GROUP
    sgn = -1 if reverse else 1
    t_i = lax.broadcasted_iota(jnp.int32, (L, 3 * L), 0)
    s_i = lax.broadcasted_iota(jnp.int32, (L, 3 * L), 1) % L
    tri3 = jnp.where(sgn * (t_i - s_i) >= 0, 1.0, 0.0).astype(BF16)
    tc = lax.broadcasted_iota(jnp.int32, (L, G * L), 0)
    sc = lax.broadcasted_iota(jnp.int32, (L, G * L), 1) % L
    before = sgn * (tc - sc)
    strict = before > 0
    rmask = before > 0 if reverse else before >= 0
    eye = jnp.where(sc == tc, 1.0, 0.0).astype(F32)
    bi = lax.broadcasted_iota(jnp.int32, (W, W), 0) // RWKV_HEAD
    bj = lax.broadcasted_iota(jnp.int32, (W, W), 1) // RWKV_HEAD
    return tri3, strict, rmask, eye, bi == bj


def _bd(x, bdmask):
    return jnp.where(bdmask, jnp.tile(x, (HEADS_PER_GROUP, 1)), jnp.zeros((), x.dtype))


_NT = (((1,), (1,)), ((), ()))
_TN = (((0,), (0,)), ((), ()))


def _chunk_operators(r, v, kk, lw, k, b, consts, reverse):
    L, W, G = SCAN_L, SCAN_W, HEADS_PER_GROUP
    tri3, strict, rmask, eye, bdmask = consts
    bd = functools.partial(_bd, bdmask=bdmask)

    def dot(a_, b_):
        return jnp.dot(a_, b_, preferred_element_type=F32)

    hi = lw.astype(BF16)
    rem = lw - hi.astype(F32)
    mid = rem.astype(BF16)
    low = (rem - mid.astype(F32)).astype(BF16)
    cum = dot(tri3, jnp.concatenate([hi, mid, low], axis=0))
    yield
    tot = jnp.sum(lw, axis=0, keepdims=True)
    e_ex = jnp.exp(cum - lw)
    e_neg = jnp.exp(-cum)
    e_rem = jnp.exp(tot - cum)
    kq = kk * e_ex
    rq = r * (e_ex if reverse else jnp.exp(cum))
    vb = v.astype(BF16)
    q = jnp.concatenate([kq, rq], axis=0).astype(BF16)
    keys = jnp.concatenate([bd((k * e_neg).astype(BF16)), bd((b * e_neg).astype(BF16))], axis=0)
    a = lax.dot_general(q, keys, _NT, preferred_element_type=F32)
    yield
    a_kk = jnp.where(strict, a[:L, :G * L], 0.0)
    a_kb = jnp.where(strict, a[:L, G * L:], 0.0)
    a_rk = jnp.where(rmask, a[L:, :G * L], 0.0)
    a_rb = jnp.where(rmask, a[L:, G * L:], 0.0).astype(BF16)
    p = -a_kb
    tinv = eye + p
    av = dot(jnp.concatenate([a_kk, a_rk], axis=0).astype(BF16), bd(vb))
    p = dot(p.astype(BF16), bd(p.astype(BF16)))
    yield
    levels = L.bit_length() - 2
    for lvl in range(levels):
        wt = bd(p.astype(BF16))
        if lvl < levels - 1:
            tp = dot(jnp.concatenate([tinv, p], axis=0).astype(BF16), wt)
            tinv = tinv + tp[:L]
            p = tp[L:]
        else:
            tinv = tinv + dot(tinv.astype(BF16), wt)
        yield
    tinv = tinv.astype(BF16)
    tk = dot(tinv, jnp.concatenate([bd(kq.astype(BF16)), bd(av[:L].astype(BF16))], axis=1))
    yield
    kq2 = tk[:, :W].astype(BF16)
    cu = tk[:, W:].astype(BF16)
    ar = dot(a_rb, jnp.concatenate([bd(kq2), bd(cu)], axis=1))
    yield
    rq2 = (rq - ar[:, :W]).astype(BF16)
    yc = av[L:] - ar[:, W:]
    keb = (k * e_rem).astype(BF16)
    beb = (b * e_rem).astype(BF16)
    g = jnp.where(bdmask, lax.dot_general(kq2, beb, _TN, preferred_element_type=F32), 0.0).astype(BF16)
    c = jnp.where(bdmask, lax.dot_general(jnp.concatenate([vb, -cu], axis=0),
                                          jnp.concatenate([keb, beb], axis=0), _TN,
                                          preferred_element_type=F32), 0.0)
    return rq2, yc, jnp.exp(tot), g, c


def _run_in_lockstep(gens):
    results = [None] * len(gens)
    active = list(enumerate(gens))
    while active:
        still = []
        for i, gen in active:
            try:
                next(gen)
                still.append((i, gen))
            except StopIteration as done:
                results[i] = done.value
        active = still
    return results


def _scan_kernel(rf_ref, vf_ref, kkf_ref, lwf_ref, kf_ref, bf_ref, rb_ref, vb_ref, kkb_ref, lwb_ref, kb_ref,
                 bb_ref, yf_ref, yb_ref, s_ref, *, tb, segs, nblk, ng):
    L, W = SCAN_L, SCAN_W
    c = pl.program_id(1)
    start_f = _is_seq_start(c * tb, segs)
    start_b = _is_seq_start((nblk - c) * tb, segs)

    @pl.when(start_f)
    def _():
        s_ref[0] = jnp.zeros(s_ref.shape[1:], F32)

    @pl.when(start_b)
    def _():
        s_ref[1] = jnp.zeros(s_ref.shape[1:], F32)

    n_chunks = tb // L
    dirs = ((rf_ref, vf_ref, kkf_ref, lwf_ref, kf_ref, bf_ref, yf_ref),
            (rb_ref, vb_ref, kkb_ref, lwb_ref, kb_ref, bb_ref, yb_ref))
    streams = [(d, g) for d in range(2) for g in range(ng)]
    order = {d: [n_chunks - 1 - j if d == 1 else j for j in range(n_chunks)] for d in range(2)}
    consts = {d: _scan_consts(reverse=d == 1) for d in range(2)}
    gens = []
    for d, g in streams:
        r_ref, v_ref, kk_ref, lw_ref, k_ref, b_ref, _ = dirs[d]
        cols = slice(g * W, (g + 1) * W)
        for ci in order[d]:
            rows = slice(ci * L, (ci + 1) * L)
            gens.append(_chunk_operators(*(ref[rows, cols].astype(F32)
                                           for ref in (r_ref, v_ref, kk_ref, lw_ref, k_ref, b_ref)),
                                         consts[d], reverse=d == 1))
    ops = _run_in_lockstep(gens)
    states = [s_ref[d, g] for d, g in streams]
    for j in range(n_chunks):
        for si, (d, g) in enumerate(streams):
            rq2, yc, decay, gmat, cmat = ops[si * n_chunks + j]
            ci = order[d][j]
            sb = states[si].astype(BF16)
            y_ref = dirs[d][6]
            y_ref[ci * L:(ci + 1) * L, g * W:(g + 1) * W] = (
                lax.dot_general(rq2, sb, _NT, preferred_element_type=F32) + yc).astype(y_ref.dtype)
            states[si] = states[si] * decay - jnp.dot(sb, gmat, preferred_element_type=F32) + cmat
    for si, (d, g) in enumerate(streams):
        s_ref[d, g] = states[si]


def _rwkv_scan(r, v, kk, lw, k, b, segs, tb=256, ng=1):
    t, dr = r.shape
    tb = min(tb, t)
    nblk = t // tb
    w = ng * SCAN_W
    fwd = pl.BlockSpec((tb, w), lambda g, c: (c, g))
    bwd = pl.BlockSpec((tb, w), lambda g, c: (nblk - 1 - c, g))
    fwd2 = pl.BlockSpec((None, tb, w), lambda g, c: (0, c, g))
    bwd2 = pl.BlockSpec((None, tb, w), lambda g, c: (1, nblk - 1 - c, g))
    out = jax.ShapeDtypeStruct((t, dr), BF16)
    return pl.pallas_call(
        functools.partial(_scan_kernel, tb=tb, segs=segs, nblk=nblk, ng=ng),
        grid=(dr // w, nblk),
        in_specs=[fwd, fwd, fwd, fwd2, fwd2, fwd2, bwd, bwd, bwd, bwd2, bwd2, bwd2],
        out_specs=[fwd, bwd],
        out_shape=[out, out],
        scratch_shapes=[pltpu.VMEM((2, ng, SCAN_W, SCAN_W), F32)],
        compiler_params=_params("parallel", "arbitrary"),
        name="rwkv_scan",
    )(r, v, kk, lw, k, b, r, v, kk, lw, k, b)


def _post_kernel(yf_ref, yb_ref, g_ref, bonus_ref, gng_ref, gnb_ref, ones_ref, o_ref):
    y = yf_ref[...].astype(F32) + yb_ref[...].astype(F32)
    ones_bd = ones_ref[...]
    mu = _head_sum(y, ones_bd) * (1.0 / RWKV_HEAD)
    cen = y - mu
    var = _head_sum(cen * cen, ones_bd) * (1.0 / RWKV_HEAD)
    yn = cen * lax.rsqrt(var + GN_EPS) * gng_ref[...] + gnb_ref[...]
    o_ref[...] = ((yn + bonus_ref[...].astype(F32)) * g_ref[...].astype(F32)).astype(o_ref.dtype)


def _rwkv_post(y_f, y_b, g, bonus, gn_g, gn_b, ones_bd, tb=256):
    t, dr = y_f.shape
    tb = min(tb, t)
    tok = pl.BlockSpec((tb, dr), lambda i: (i, 0))
    vec = pl.BlockSpec((1, dr), lambda i: (0, 0))
    return pl.pallas_call(
        _post_kernel,
        grid=(t // tb,),
        in_specs=[tok, tok, tok, tok, vec, vec, pl.BlockSpec((SCAN_W, SCAN_W), lambda i: (0, 0))],
        out_specs=tok,
        out_shape=jax.ShapeDtypeStruct((t, dr), BF16),
        compiler_params=_params("parallel"),
        name="rwkv_post",
    )(y_f, y_b, g, bonus, gn_g.reshape(1, dr), gn_b.reshape(1, dr), ones_bd)


def _ln_kernel(x_ref, m_ref, g_ref, b_ref, o_ref, ob_ref, *, alpha):
    h = alpha * x_ref[...] + m_ref[...].astype(F32)
    mu = jnp.mean(h, axis=-1, keepdims=True)
    cen = h - mu
    var = jnp.mean(cen * cen, axis=-1, keepdims=True)
    y = cen * lax.rsqrt(var + LN_EPS) * g_ref[...] + b_ref[...]
    o_ref[...] = y
    ob_ref[...] = y.astype(BF16)


def _residual_ln(x, mix, g, b, alpha, tb=256):
    t, dm = x.shape
    tb = min(tb, t)
    tok = pl.BlockSpec((tb, dm), lambda i: (i, 0))
    vec = pl.BlockSpec((1, dm), lambda i: (0, 0))
    return pl.pallas_call(
        functools.partial(_ln_kernel, alpha=alpha),
        grid=(t // tb,),
        in_specs=[tok, tok, vec, vec],
        out_specs=[tok, tok],
        out_shape=[jax.ShapeDtypeStruct((t, dm), F32), jax.ShapeDtypeStruct((t, dm), BF16)],
        compiler_params=_params("parallel"),
        name="residual_ln",
    )(x, mix, g.reshape(1, dm), b.reshape(1, dm))


def _ffn_up_kernel(x_ref, wg_ref, wu_ref, h_ref):
    x = x_ref[...]
    a = jnp.dot(x, wg_ref[...], preferred_element_type=F32)
    u = jnp.dot(x, wu_ref[...], preferred_element_type=F32)
    h_ref[...] = (a * jax.nn.sigmoid(a) * u).astype(h_ref.dtype)


def _ffn_up(xb, wg, wu, tm=1024, tn=512):
    t, dm = xb.shape
    f = wg.shape[1]
    tm = min(tm, t)
    wspec = pl.BlockSpec((dm, tn), lambda i, j: (0, j))
    return pl.pallas_call(
        _ffn_up_kernel,
        grid=(t // tm, f // tn),
        in_specs=[pl.BlockSpec((tm, dm), lambda i, j: (i, 0)), wspec, wspec],
        out_specs=pl.BlockSpec((tm, tn), lambda i, j: (i, j)),
        out_shape=jax.ShapeDtypeStruct((t, f), BF16),
        compiler_params=_params("parallel", "parallel"),
        name="ffn_up",
    )(xb, wg, wu)


def _router_kernel(x_ref, w_ref, idx_ref, wt_ref):
    logits = jnp.dot(x_ref[...], w_ref[...], precision=HIGHEST, preferred_element_type=F32)
    lane = lax.broadcasted_iota(jnp.int32, logits.shape, 1)
    neg = jnp.float32(-jnp.inf)
    logits = jnp.where(lane < N_EXPERTS, logits, neg)
    m1 = jnp.max(logits, axis=1, keepdims=True)
    i1 = jnp.min(jnp.where(logits == m1, lane, LANES), axis=1, keepdims=True)
    rest = jnp.where(lane == i1, neg, logits)
    m2 = jnp.max(rest, axis=1, keepdims=True)
    i2 = jnp.min(jnp.where(rest == m2, lane, LANES), axis=1, keepdims=True)
    e2 = jnp.exp(m2 - m1)
    w1 = 1.0 / (1.0 + e2)
    w2 = e2 / (1.0 + e2)
    idx_ref[...] = jnp.where(lane == 0, i1, jnp.where(lane == 1, i2, 0))
    wt_ref[...] = jnp.where(lane == 0, w1, jnp.where(lane == 1, w2, 0.0))


def _router(x, router_w, tb=512):
    t, dm = x.shape
    tb = min(tb, t)
    w = jnp.pad(router_w, ((0, 0), (0, LANES - router_w.shape[1])))
    out = pl.BlockSpec((tb, LANES), lambda i: (i, 0))
    return pl.pallas_call(
        _router_kernel,
        grid=(t // tb,),
        in_specs=[pl.BlockSpec((tb, dm), lambda i: (i, 0)), pl.BlockSpec((dm, LANES), lambda i: (0, 0))],
        out_specs=[out, out],
        out_shape=[jax.ShapeDtypeStruct((t, LANES), jnp.int32), jax.ShapeDtypeStruct((t, LANES), F32)],
        compiler_params=_params("parallel"),
        name="router",
    )(x, w)


MOE_TM = 512


def _moe_dispatch(idx, wts, tm):
    t = idx.shape[0]
    n_rows = 2 * t + N_EXPERTS * tm
    n_tiles = n_rows // tm
    e_ids = idx[:, :2]
    per_tok = (e_ids[:, :, None] == jnp.arange(N_EXPERTS)).astype(jnp.int32).sum(axis=1)
    csum = jnp.cumsum(per_tok, axis=0)
    counts = csum[-1]
    rank = csum - per_tok
    padded = (counts + tm - 1) // tm * tm
    ends = jnp.cumsum(padded)
    offs = ends - padded
    pos = offs[e_ids] + jnp.take_along_axis(rank, e_ids, axis=1)
    code = jnp.full((n_rows,), -1, jnp.int32).at[pos.reshape(-1)].set(jnp.arange(2 * t, dtype=jnp.int32))
    src = jnp.maximum(code, 0) // 2
    row_w = jnp.where(code >= 0, wts[:, :2].reshape(-1)[jnp.maximum(code, 0)], 0.0)
    tile_start = jnp.arange(n_tiles, dtype=jnp.int32) * tm
    owner = jnp.sum(tile_start[:, None] >= ends[None, :], axis=1).astype(jnp.int32)
    tile_valid = (owner < N_EXPERTS).astype(jnp.int32)
    tile_expert = jnp.minimum(owner, N_EXPERTS - 1)
    return src, row_w, pos.astype(jnp.int32), tile_expert, tile_valid


def _gather_rows(idx_ref, n, src_hbm, dst_ref, sem):
    def issue(r, carry):
        pltpu.make_async_copy(src_hbm.at[pl.ds(idx_ref[0, r], 1), :], dst_ref.at[pl.ds(r, 1), :], sem).start()
        return carry

    lax.fori_loop(0, n, issue, 0)
    pltpu.make_async_copy(src_hbm.at[pl.ds(0, n), :], dst_ref.at[pl.ds(0, n), :], sem).wait()


def _moe_up_sparse_kernel(te_ref, tv_ref, src_ref, x_hbm, wg_ref, wu_ref, roww_ref, h_ref, xbuf, xb_buf, sem, *, tm):
    i = pl.program_id(0)
    valid = tv_ref[i] == 1

    @pl.when(jnp.logical_and(valid, pl.program_id(1) == 0))
    def _():
        _gather_rows(src_ref, tm, x_hbm, xbuf, sem)
        xb_buf[...] = xbuf[...].astype(BF16)

    @pl.when(valid)
    def _():
        x = xb_buf[...]
        a = jnp.dot(x, wg_ref[...], preferred_element_type=F32)
        u = jnp.dot(x, wu_ref[...], preferred_element_type=F32)
        h_ref[...] = (a * jax.nn.sigmoid(a) * u * roww_ref[...]).astype(h_ref.dtype)

    @pl.when(jnp.logical_not(valid))
    def _():
        h_ref[...] = jnp.zeros_like(h_ref)


def _moe_up_sparse(x, we_gate, we_up, src, row_w, tile_expert, tile_valid, tm, tn=512):
    t, dm = x.shape
    n_e, _, fe = we_gate.shape
    n_rows = src.shape[0]
    n_tiles = n_rows // tm
    wspec = pl.BlockSpec((None, dm, tn), lambda i, j, te, tv: (te[i], 0, j))
    grid_spec = pltpu.PrefetchScalarGridSpec(
        num_scalar_prefetch=2,
        grid=(n_tiles, fe // tn),
        in_specs=[pl.BlockSpec((None, 1, tm), lambda i, j, te, tv: (i, 0, 0), memory_space=pltpu.SMEM),
                  pl.BlockSpec(memory_space=pl.ANY),
                  wspec, wspec,
                  pl.BlockSpec((tm, 1), lambda i, j, te, tv: (i, 0))],
        out_specs=pl.BlockSpec((tm, tn), lambda i, j, te, tv: (i, j)),
        scratch_shapes=[pltpu.VMEM((tm, dm), F32), pltpu.VMEM((tm, dm), BF16), pltpu.SemaphoreType.DMA(())],
    )
    return pl.pallas_call(
        functools.partial(_moe_up_sparse_kernel, tm=tm),
        grid_spec=grid_spec,
        out_shape=jax.ShapeDtypeStruct((n_rows, fe), BF16),
        compiler_params=_params("arbitrary", "arbitrary"),
        name="moe_up_sparse",
    )(tile_expert, tile_valid, src.reshape(n_tiles, 1, tm), x, we_gate, we_up, row_w.reshape(n_rows, 1))


def _moe_down_kernel(te_ref, tv_ref, h_ref, w_ref, y_ref):
    valid = tv_ref[pl.program_id(0)] == 1

    @pl.when(valid)
    def _():
        y_ref[...] = jnp.dot(h_ref[...], w_ref[...], preferred_element_type=F32)

    @pl.when(jnp.logical_not(valid))
    def _():
        y_ref[...] = jnp.zeros_like(y_ref)


def _moe_down_sparse(h, we_down, tile_expert, tile_valid, tm, tn=1024):
    n_rows, fe = h.shape
    dm = we_down.shape[2]
    grid_spec = pltpu.PrefetchScalarGridSpec(
        num_scalar_prefetch=2,
        grid=(n_rows // tm, dm // tn),
        in_specs=[pl.BlockSpec((tm, fe), lambda i, j, te, tv: (i, 0)),
                  pl.BlockSpec((None, fe, tn), lambda i, j, te, tv: (te[i], 0, j))],
        out_specs=pl.BlockSpec((tm, tn), lambda i, j, te, tv: (i, j)),
    )
    return pl.pallas_call(
        _moe_down_kernel,
        grid_spec=grid_spec,
        out_shape=jax.ShapeDtypeStruct((n_rows, dm), F32),
        compiler_params=_params("parallel", "parallel"),
        name="moe_down_sparse",
    )(tile_expert, tile_valid, h, we_down)


def _moe_combine_ln_kernel(pos_ref, y_hbm, x_ref, g_ref, b_ref, o_ref, ob_ref, ybuf, sem, *, alpha, tb):
    _gather_rows(pos_ref, 2 * tb, y_hbm, ybuf, sem)
    h = alpha * x_ref[...] + (ybuf[0:tb, :] + ybuf[tb:2 * tb, :])
    mu = jnp.mean(h, axis=-1, keepdims=True)
    cen = h - mu
    var = jnp.mean(cen * cen, axis=-1, keepdims=True)
    y = cen * lax.rsqrt(var + LN_EPS) * g_ref[...] + b_ref[...]
    o_ref[...] = y
    ob_ref[...] = y.astype(BF16)


def _moe_combine_ln(x, ys, pos, g, b, alpha, tb=256):
    t, dm = x.shape
    tb = min(tb, t)
    tok = pl.BlockSpec((tb, dm), lambda i: (i, 0))
    vec = pl.BlockSpec((1, dm), lambda i: (0, 0))
    return pl.pallas_call(
        functools.partial(_moe_combine_ln_kernel, alpha=alpha, tb=tb),
        grid=(t // tb,),
        in_specs=[pl.BlockSpec((None, 1, 2 * tb), lambda i: (i, 0, 0), memory_space=pltpu.SMEM),
                  pl.BlockSpec(memory_space=pl.ANY), tok, vec, vec],
        out_specs=[tok, tok],
        out_shape=[jax.ShapeDtypeStruct((t, dm), F32), jax.ShapeDtypeStruct((t, dm), BF16)],
        scratch_shapes=[pltpu.VMEM((2 * tb, dm), F32), pltpu.SemaphoreType.DMA(())],
        compiler_params=_params("arbitrary"),
        name="moe_combine_ln",
    )(jnp.concatenate([pos[:, 0].reshape(t // tb, 1, tb), pos[:, 1].reshape(t // tb, 1, tb)], axis=2),
      ys, x, g.reshape(1, dm), b.reshape(1, dm))


def _ple_kernel(xb_ref, wg_ref, p_ref, wp_ref, x_ref, o_ref, ob_ref):
    gate = jax.nn.sigmoid(jnp.dot(xb_ref[...], wg_ref[...], preferred_element_type=F32))
    emb = jnp.dot(p_ref[...], wp_ref[...], preferred_element_type=F32)
    y = x_ref[...] + gate * emb
    o_ref[...] = y
    ob_ref[...] = y.astype(BF16)


def _ple(xb, x, w_pgate, p, w_pproj, tm=1024, tn=512):
    t, dm = xb.shape
    pd = p.shape[1]
    tm = min(tm, t)
    out = pl.BlockSpec((tm, tn), lambda i, j: (i, j))
    return pl.pallas_call(
        _ple_kernel,
        grid=(t // tm, dm // tn),
        in_specs=[pl.BlockSpec((tm, dm), lambda i, j: (i, 0)),
                  pl.BlockSpec((dm, tn), lambda i, j: (0, j)),
                  pl.BlockSpec((tm, pd), lambda i, j: (i, 0)),
                  pl.BlockSpec((pd, tn), lambda i, j: (0, j)),
                  out],
        out_specs=[out, out],
        out_shape=[jax.ShapeDtypeStruct((t, dm), F32), jax.ShapeDtypeStruct((t, dm), BF16)],
        compiler_params=_params("parallel", "parallel"),
        name="ple_gate",
    )(xb, w_pgate, p, w_pproj, x)


def _pad_cols(w, width):
    return jnp.pad(w, ((0, 0), (0, width - w.shape[1])))


def _pad_rows(w, rows):
    pad = [(0, 0)] * w.ndim
    pad[-2] = (0, rows - w.shape[-2])
    return jnp.pad(w, pad)


def _split_rwkv_cols(w, dg, dr, has_mv):
    base = 2 * dg
    rkv = w[:, base:base + 3 * dr]
    o = base + 3 * dr
    sizes = [DECAY_LORA, DECAY_LORA, AAA_LORA, AAA_LORA, GATE_LORA] + ([MV_LORA] if has_mv else [])
    slots = [LORA_SLOT, LORA_SLOT, LORA_SLOT, LORA_SLOT, GATE_LORA, LORA_SLOT]
    parts = []
    for n, slot in zip(sizes, slots):
        parts.append(_pad_cols(w[:, o:o + n], slot))
        o += n
    lora = _pad_cols(jnp.concatenate(parts, axis=1), LORA_COLS)
    return jnp.concatenate([rkv, lora], axis=1)


def _trunk(x, p, P, segs):
    t, dm = x.shape
    depth = P['w_out'].shape[0]
    alpha = (2 * depth) ** 0.25
    dg = P['sgu_ln_g'].shape[1]
    dr = P['k_k'].shape[1]
    hb = lax.broadcasted_iota(jnp.int32, (SCAN_W, SCAN_W), 0) // RWKV_HEAD
    hc = lax.broadcasted_iota(jnp.int32, (SCAN_W, SCAN_W), 1) // RWKV_HEAD
    ones_bd = (hb == hc).astype(BF16)
    xb = x.astype(BF16)
    w_out_b = _to_bf16(P['w_out'])
    w_pgate_b = _to_bf16(P['w_pgate'])
    v_first = None
    for l in range(depth):
        if l == 0:
            w_in_l, conv_l, v0_l, v2_l = P['w_in0'], P['conv0'], None, None
        else:
            w_in_l, conv_l, v0_l, v2_l = P['w_in'][l - 1], P['conv'][l - 1], P['v0'][l - 1], P['v2'][l - 1]
        w_in_b = _to_bf16(w_in_l)
        uv = _matmul(xb, w_in_b[:, :2 * dg], BF16)
        zr = _matmul(xb, _split_rwkv_cols(w_in_b, dg, dr, l > 0), BF16)
        y_g = _spatial_gating(uv, P['sgu_ln_g'][l], P['sgu_ln_b'][l], P['w_s'][l], P['b_s'][l])
        conv_r = _split_rwkv_cols(jnp.pad(conv_l, ((0, 0), (2 * dg, 0))), dg, dr, l > 0)
        prep_args = (zr, conv_r, P['w0'][l], _pad_rows(P['w2'][l], LORA_SLOT).astype(BF16), P['a0'][l],
                     _pad_rows(P['a2'][l], LORA_SLOT).astype(BF16), P['g2'][l].astype(BF16),
                     P['k_k'][l], P['k_a'][l], P['r_k'][l].reshape(-1), ones_bd, segs)
        if l == 0:
            r, v, kk, g, bonus, lw, k, b = _rwkv_prep(*prep_args)
            v_first = v
        else:
            r, v, kk, g, bonus, lw, k, b = _rwkv_prep(*prep_args, v_first=v_first, v0=v0_l,
                                                      v2=_pad_rows(v2_l, LORA_SLOT).astype(BF16))
        y_f, y_b = _rwkv_scan(r, v, kk, lw, k, b, segs)
        y_r = _rwkv_post(y_f, y_b, g, bonus, P['gn_g'][l], P['gn_b'][l], ones_bd)
        w_out = w_out_b[l]
        mix = _matmul_pair(y_g, w_out[:dg], y_r, w_out[dg:], BF16)
        x, xb = _residual_ln(x, mix, P['ln1_g'][l], P['ln1_b'][l], alpha)
        j = l // 2
        if l % 2 == 0:
            h = _ffn_up(xb, _to_bf16(P['w_ff_gate'][j]), _to_bf16(P['w_ff_up'][j]))
            ff = _matmul(h, _to_bf16(P['w_ff_down'][j]), BF16, tk=2048)
        else:
            idx, wts = _router(x, P['router'][j])
            tm = min(MOE_TM, t)
            src, row_w, pos, tile_expert, tile_valid = _moe_dispatch(idx, wts, tm)
            h = _moe_up_sparse(x, _to_bf16(P['we_gate'][j]), _to_bf16(P['we_up'][j]), src, row_w,
                               tile_expert, tile_valid, tm)
            ys = _moe_down_sparse(h, _to_bf16(P['we_down'][j]), tile_expert, tile_valid, tm)
            x, xb = _moe_combine_ln(x, ys, pos, P['ln2_g'][l], P['ln2_b'][l], alpha)
        if l % 2 == 0:
            x, xb = _residual_ln(x, ff, P['ln2_g'][l], P['ln2_b'][l], alpha)
        x, xb = _ple(xb, x, w_pgate_b[l], p[l].astype(BF16), P['w_pproj'][l].astype(BF16))
    return x


def kernel(x_prompt, x_sample, p_prompt, p_sample, w_in0, conv0, w_in, conv, sgu_ln_g, sgu_ln_b, w_s, b_s, w0, w2, a0, a2, g2, k_k, k_a, r_k, gn_g, gn_b, v0, v2, w_out, ln1_g, ln1_b, ln2_g, ln2_b, w_ff_gate, w_ff_up, w_ff_down, router, we_gate, we_up, we_down, w_pproj, w_pgate):
    P = dict(w_in0=w_in0, conv0=conv0, w_in=w_in, conv=conv, sgu_ln_g=sgu_ln_g, sgu_ln_b=sgu_ln_b,
             w_s=w_s, b_s=b_s, w0=w0, w2=w2, a0=a0, a2=a2, g2=g2, k_k=k_k, k_a=k_a, r_k=r_k,
             gn_g=gn_g, gn_b=gn_b, v0=v0, v2=v2, w_out=w_out, ln1_g=ln1_g, ln1_b=ln1_b,
             ln2_g=ln2_g, ln2_b=ln2_b, w_ff_gate=w_ff_gate, w_ff_up=w_ff_up, w_ff_down=w_ff_down,
             router=router, we_gate=we_gate, we_up=we_up, we_down=we_down,
             w_pproj=w_pproj, w_pgate=w_pgate)
    b1, s1, dm = x_prompt.shape
    b2, s2, _ = x_sample.shape
    depth, pd = p_prompt.shape[0], p_prompt.shape[-1]
    t1, t2 = b1 * s1, b2 * s2
    segs = ((t1, s1), (t2, s2))
    x = jnp.concatenate([x_prompt.reshape(t1, dm), x_sample.reshape(t2, dm)], axis=0)
    p = jnp.concatenate([p_prompt.reshape(depth, t1, pd), p_sample.reshape(depth, t2, pd)], axis=1)
    y = _trunk(x, p, P, segs)
    return y[:t1].reshape(b1, s1, dm), y[t1:].reshape(b2, s2, dm)
```

```python
import functools

import jax
import jax.numpy as jnp
from jax import lax
from jax.experimental import pallas as pl
from jax.experimental.pallas import tpu as pltpu

F32 = jnp.float32
BF16 = jnp.bfloat16
HIGHEST = lax.Precision.HIGHEST

LANES = 128
BF16_SUBLANES = 16
VMEM_LIMIT_BYTES = 56 * 1024 * 1024

CHUNK = 128
GMLP_HEAD = 128
RWKV_HEAD = 64
DECAY_LORA = 96
AAA_LORA = 96
MV_LORA = 64
GATE_LORA = 256
N_EXPERTS = 8
LN_EPS = 1e-5
GN_EPS = 64e-5
L2_EPS = 1e-12

SCAN_L = 64
SCAN_W = 256
HEADS_PER_GROUP = SCAN_W // RWKV_HEAD
LORA_SLOT = 128
LORA_COLS = 1024


def _params(*sem):
    return pltpu.CompilerParams(dimension_semantics=sem, vmem_limit_bytes=VMEM_LIMIT_BYTES)


def _is_seq_start(pos, segs):
    (t1, s1), (_, s2) = segs
    return jnp.where(pos < t1, pos % s1 == 0, (pos - t1) % s2 == 0)


CAST_BLOCK_BYTES = 8 * 1024 * 1024


def _cast_kernel(x_ref, o_ref):
    o_ref[...] = x_ref[...].astype(o_ref.dtype)


def _to_bf16(w):
    w2 = w.reshape(-1, w.shape[-1])
    rows, n = w2.shape
    tb = rows
    while tb * n * w2.dtype.itemsize > CAST_BLOCK_BYTES and tb % 2 == 0 and (tb // 2) % BF16_SUBLANES == 0:
        tb //= 2
    out = pl.pallas_call(
        _cast_kernel,
        grid=(rows // tb,),
        in_specs=[pl.BlockSpec((tb, n), lambda i: (i, 0))],
        out_specs=pl.BlockSpec((tb, n), lambda i: (i, 0)),
        out_shape=jax.ShapeDtypeStruct((rows, n), BF16),
        compiler_params=_params("parallel"),
        name="cast_bf16",
    )(w2)
    return out.reshape(w.shape)


def _mm_kernel(x_ref, w_ref, o_ref, *scratch, nk):
    if nk == 1:
        o_ref[...] = jnp.dot(x_ref[...], w_ref[...], preferred_element_type=F32).astype(o_ref.dtype)
        return
    acc_ref, = scratch
    k = pl.program_id(2)

    @pl.when(k == 0)
    def _():
        acc_ref[...] = jnp.zeros_like(acc_ref)

    acc_ref[...] += jnp.dot(x_ref[...], w_ref[...], preferred_element_type=F32)

    @pl.when(k == nk - 1)
    def _():
        o_ref[...] = acc_ref[...].astype(o_ref.dtype)


def _matmul(x, w, out_dtype, tm=1024, tn=1024, tk=None):
    m, kdim = x.shape
    n = w.shape[1]
    tm, tn = min(tm, m), min(tn, n)
    tk = kdim if tk is None else min(tk, kdim)
    nk = kdim // tk
    assert m % tm == 0 and n % tn == 0 and kdim % tk == 0
    scratch = [] if nk == 1 else [pltpu.VMEM((tm, tn), F32)]
    return pl.pallas_call(
        functools.partial(_mm_kernel, nk=nk),
        grid=(m // tm, n // tn, nk),
        in_specs=[pl.BlockSpec((tm, tk), lambda i, j, k: (i, k)),
                  pl.BlockSpec((tk, tn), lambda i, j, k: (k, j))],
        out_specs=pl.BlockSpec((tm, tn), lambda i, j, k: (i, j)),
        out_shape=jax.ShapeDtypeStruct((m, n), out_dtype),
        scratch_shapes=scratch,
        compiler_params=_params("parallel", "parallel", "arbitrary"),
        name="matmul",
    )(x, w)


def _mm_pair_kernel(xa_ref, wa_ref, xb_ref, wb_ref, o_ref):
    acc = jnp.dot(xa_ref[...], wa_ref[...], preferred_element_type=F32)
    acc += jnp.dot(xb_ref[...], wb_ref[...], preferred_element_type=F32)
    o_ref[...] = acc.astype(o_ref.dtype)


def _matmul_pair(xa, wa, xb, wb, out_dtype, tm=1024, tn=1024):
    m, ka = xa.shape
    kb = xb.shape[1]
    n = wa.shape[1]
    tm, tn = min(tm, m), min(tn, n)
    return pl.pallas_call(
        _mm_pair_kernel,
        grid=(m // tm, n // tn),
        in_specs=[pl.BlockSpec((tm, ka), lambda i, j: (i, 0)),
                  pl.BlockSpec((ka, tn), lambda i, j: (0, j)),
                  pl.BlockSpec((tm, kb), lambda i, j: (i, 0)),
                  pl.BlockSpec((kb, tn), lambda i, j: (0, j))],
        out_specs=pl.BlockSpec((tm, tn), lambda i, j: (i, j)),
        out_shape=jax.ShapeDtypeStruct((m, n), out_dtype),
        compiler_params=_params("parallel", "parallel"),
        name="matmul_pair",
    )(xa, wa, xb, wb)


def _sgu_kernel(u_ref, v_ref, g_ref, b_ref, ws_ref, bs_ref, o_ref, *, n_chunks):
    gv = jax.nn.gelu(v_ref[...].astype(F32))
    mu = jnp.mean(gv, axis=-1, keepdims=True)
    cen = gv - mu
    var = jnp.mean(cen * cen, axis=-1, keepdims=True)
    vn = (cen * lax.rsqrt(var + LN_EPS) * g_ref[...] + b_ref[...]).astype(BF16)
    n_heads = ws_ref.shape[0]
    for c in range(n_chunks):
        rows = slice(c * CHUNK, (c + 1) * CHUNK)
        for h in range(n_heads):
            cols = slice(h * GMLP_HEAD, (h + 1) * GMLP_HEAD)
            sv = jnp.dot(ws_ref[h], vn[rows, cols], preferred_element_type=F32) + bs_ref[:, cols]
            gu = jax.nn.gelu(u_ref[rows, cols].astype(F32))
            o_ref[rows, cols] = (gu * sv).astype(o_ref.dtype)


def _spatial_gating(uv, ln_g, ln_b, w_s, b_s, tb=512):
    t = uv.shape[0]
    dg = uv.shape[1] // 2
    tb = min(tb, t)
    n_heads = w_s.shape[0]
    bs_full = jnp.repeat(b_s.T, GMLP_HEAD, axis=1)
    return pl.pallas_call(
        functools.partial(_sgu_kernel, n_chunks=tb // CHUNK),
        grid=(t // tb,),
        in_specs=[pl.BlockSpec((tb, dg), lambda i: (i, 0)),
                  pl.BlockSpec((tb, dg), lambda i: (i, 1)),
                  pl.BlockSpec((1, dg), lambda i: (0, 0)),
                  pl.BlockSpec((1, dg), lambda i: (0, 0)),
                  pl.BlockSpec((n_heads, CHUNK, CHUNK), lambda i: (0, 0, 0)),
                  pl.BlockSpec((CHUNK, dg), lambda i: (0, 0))],
        out_specs=pl.BlockSpec((tb, dg), lambda i: (i, 0)),
        out_shape=jax.ShapeDtypeStruct((t, dg), BF16),
        compiler_params=_params("parallel"),
        name="spatial_gating",
    )(uv, uv, ln_g.reshape(1, dg), ln_b.reshape(1, dg), w_s.astype(BF16), bs_full)


def _head_sum(x, ones_bd):
    parts = []
    for g in range(x.shape[1] // SCAN_W):
        xs = x[:, g * SCAN_W:(g + 1) * SCAN_W]
        hi = xs.astype(BF16)
        lo = (xs - hi.astype(F32)).astype(BF16)
        parts.append(jnp.dot(hi, ones_bd, preferred_element_type=F32)
                     + jnp.dot(lo, ones_bd, preferred_element_type=F32))
    return jnp.concatenate(parts, axis=1)


def _prep_kernel(*refs, layer1, tb, segs, dr):
    if layer1:
        (z_ref, zp_ref, zn_ref, conv_ref, w0_ref, w2_ref, a0_ref, a2_ref, g2_ref, kk_ref, ka_ref, rk_ref,
         ones_ref, vf_ref, v0_ref, v2_ref, r_o, v_o, kkn_o, g_o, bonus_o, lw_o, k_o, b_o) = refs
    else:
        (z_ref, zp_ref, zn_ref, conv_ref, w0_ref, w2_ref, a0_ref, a2_ref, g2_ref, kk_ref, ka_ref, rk_ref,
         ones_ref, r_o, v_o, kkn_o, g_o, bonus_o, lw_o, k_o, b_o) = refs
    pos0 = pl.program_id(0) * tb
    first = _is_seq_start(pos0, segs)
    last = _is_seq_start(pos0 + tb, segs)
    row = lax.broadcasted_iota(jnp.int32, (tb, 1), 0)

    def conv(c0, width):
        cols = slice(c0, c0 + width)
        z = z_ref[:, cols].astype(F32)
        prev_row = jnp.where(first, 0.0, zp_ref[:, cols].astype(F32)[BF16_SUBLANES - 1:BF16_SUBLANES])
        next_row = jnp.where(last, 0.0, zn_ref[:, cols].astype(F32)[0:1])
        zprev = jnp.where(row == 0, prev_row, pltpu.roll(z, 1, 0))
        znext = jnp.where(row == tb - 1, next_row, pltpu.roll(z, tb - 1, 0))
        return zprev * conv_ref[0:1, cols] + z * conv_ref[1:2, cols] + znext * conv_ref[2:3, cols]

    lo = conv(3 * dr, LORA_COLS)
    s = LORA_SLOT
    wd = (lo[:, 0:s], lo[:, s:2 * s])
    ad = (lo[:, 2 * s:3 * s], lo[:, 3 * s:4 * s])
    gd = lo[:, 4 * s:6 * s]

    def lora(x, w):
        return jnp.dot(x.astype(BF16), w, preferred_element_type=F32)

    g_o[...] = lora(jax.nn.sigmoid(gd), g2_ref[...]).astype(g_o.dtype)
    v = conv(2 * dr, dr)
    if layer1:
        mv = lo[:, 6 * s:7 * s]
        v = v + (vf_ref[...].astype(F32) - v) * jax.nn.sigmoid(v0_ref[...] + lora(mv, v2_ref[...]))
    v_o[...] = v.astype(v_o.dtype)
    r = conv(0, dr)
    r_o[...] = r.astype(r_o.dtype)
    k = conv(dr, dr)
    ones_bd = ones_ref[...]
    kk = k * kk_ref[...]
    kk = kk * lax.rsqrt(_head_sum(kk * kk, ones_bd) + L2_EPS)
    kkn_o[...] = kk.astype(kkn_o.dtype)
    for d in range(2):
        zw = w0_ref[d:d + 1, :] + lora(jnp.tanh(wd[d]), w2_ref[d])
        w_log = -(jnp.maximum(-zw, 0.0) + jnp.log(1.0 + jnp.exp(-jnp.abs(zw)))) - 0.5
        lw_o[d] = -jnp.exp(w_log)
        a = jax.nn.sigmoid(a0_ref[d:d + 1, :] + lora(ad[d], a2_ref[d]))
        k_d = k * (1.0 + (a - 1.0) * ka_ref[...])
        k_o[d] = k_d.astype(k_o.dtype)
        b_o[d] = (kk * a).astype(b_o.dtype)
        if d == 0:
            bonus_o[...] = (_head_sum(r * k_d * rk_ref[...], ones_bd) * v).astype(bonus_o.dtype)


def _rwkv_prep(zr, conv_w, w0, w2, a0, a2, g2, k_k, k_a, r_k, ones_bd, segs, v_first=None, v0=None, v2=None,
               tb=128):
    t, c = zr.shape
    dr = k_k.shape[-1]
    tb = min(tb, t)
    layer1 = v_first is not None
    hb = tb // BF16_SUBLANES
    n_halo = t // BF16_SUBLANES
    const2 = lambda i: (0, 0)
    const3 = lambda i: (0, 0, 0)
    in_specs = [pl.BlockSpec((tb, c), lambda i: (i, 0)),
                pl.BlockSpec((BF16_SUBLANES, c), lambda i: (jnp.maximum(i * hb - 1, 0), 0)),
                pl.BlockSpec((BF16_SUBLANES, c), lambda i: (jnp.minimum((i + 1) * hb, n_halo - 1), 0)),
                pl.BlockSpec((3, c), const2),
                pl.BlockSpec((2, dr), const2),
                pl.BlockSpec((2, LORA_SLOT, dr), const3),
                pl.BlockSpec((2, dr), const2),
                pl.BlockSpec((2, LORA_SLOT, dr), const3),
                pl.BlockSpec((GATE_LORA, dr), const2),
                pl.BlockSpec((1, dr), const2),
                pl.BlockSpec((1, dr), const2),
                pl.BlockSpec((1, dr), const2),
                pl.BlockSpec((SCAN_W, SCAN_W), const2)]
    args = [zr, zr, zr, conv_w, w0, w2, a0, a2, g2, k_k.reshape(1, dr), k_a.reshape(1, dr), r_k.reshape(1, dr),
            ones_bd]
    if layer1:
        in_specs += [pl.BlockSpec((tb, dr), lambda i: (i, 0)),
                     pl.BlockSpec((1, dr), const2),
                     pl.BlockSpec((LORA_SLOT, dr), const2)]
        args += [v_first, v0.reshape(1, dr), v2]
    tok = pl.BlockSpec((tb, dr), lambda i: (i, 0))
    tok2 = pl.BlockSpec((2, tb, dr), lambda i: (0, i, 0))
    one = jax.ShapeDtypeStruct((t, dr), BF16)
    two = jax.ShapeDtypeStruct((2, t, dr), BF16)
    log_decay = jax.ShapeDtypeStruct((2, t, dr), F32)
    return pl.pallas_call(
        functools.partial(_prep_kernel, layer1=layer1, tb=tb, segs=segs, dr=dr),
        grid=(t // tb,),
        in_specs=in_specs,
        out_specs=[tok, tok, tok, tok, tok, tok2, tok2, tok2],
        out_shape=[one, one, one, one, one, log_decay, two, two],
        compiler_params=_params("parallel"),
        name="rwkv_prep",
    )(*args)


def _scan_consts(reverse):
    L, W, G = SCAN_L, SCAN_W, HEADS_PER_GROUP
    sgn = -1 if reverse else 1
    t_i = lax.broadcasted_iota(jnp.int32, (L, 3 * L), 0)
    s_i = lax.broadcasted_iota(jnp.int32, (L, 3 * L), 1) % L
    tri3 = jnp.where(sgn * (t_i - s_i) >= 0, 1.0, 0.0).astype(BF16)
    tc = lax.broadcasted_iota(jnp.int32, (L, G * L), 0)
    sc = lax.broadcasted_iota(jnp.int32, (L, G * L), 1) % L
    before = sgn * (tc - sc)
    strict = before > 0
    rmask = before > 0 if reverse else before >= 0
    eye = jnp.where(sc == tc, 1.0, 0.0).astype(F32)
    bi = lax.broadcasted_iota(jnp.int32, (W, W), 0) // RWKV_HEAD
    bj = lax.broadcasted_iota(jnp.int32, (W, W), 1) // RWKV_HEAD
    return tri3, strict, rmask, eye, bi == bj


def _bd(x, bdmask):
    return jnp.where(bdmask, jnp.tile(x, (HEADS_PER_GROUP, 1)), jnp.zeros((), x.dtype))


_NT = (((1,), (1,)), ((), ()))
_TN = (((0,), (0,)), ((), ()))


def _chunk_operators(r, v, kk, lw, k, b, consts, reverse):
    L, W, G = SCAN_L, SCAN_W, HEADS_PER_GROUP
    tri3, strict, rmask, eye, bdmask = consts
    bd = functools.partial(_bd, bdmask=bdmask)

    def dot(a_, b_):
        return jnp.dot(a_, b_, preferred_element_type=F32)

    hi = lw.astype(BF16)
    rem = lw - hi.astype(F32)
    mid = rem.astype(BF16)
    low = (rem - mid.astype(F32)).astype(BF16)
    cum = dot(tri3, jnp.concatenate([hi, mid, low], axis=0))
    yield
    tot = jnp.sum(lw, axis=0, keepdims=True)
    e_ex = jnp.exp(cum - lw)
    e_neg = jnp.exp(-cum)
    e_rem = jnp.exp(tot - cum)
    kq = kk * e_ex
    rq = r * (e_ex if reverse else jnp.exp(cum))
    vb = v.astype(BF16)
    q = jnp.concatenate([kq, rq], axis=0).astype(BF16)
    keys = jnp.concatenate([bd((k * e_neg).astype(BF16)), bd((b * e_neg).astype(BF16))], axis=0)
    a = lax.dot_general(q, keys, _NT, preferred_element_type=F32)
    yield
    a_kk = jnp.where(strict, a[:L, :G * L], 0.0)
    a_kb = jnp.where(strict, a[:L, G * L:], 0.0)
    a_rk = jnp.where(rmask, a[L:, :G * L], 0.0)
    a_rb = jnp.where(rmask, a[L:, G * L:], 0.0).astype(BF16)
    p = -a_kb
    tinv = eye + p
    av = dot(jnp.concatenate([a_kk, a_rk], axis=0).astype(BF16), bd(vb))
    p = dot(p.astype(BF16), bd(p.astype(BF16)))
    yield
    levels = L.bit_length() - 2
    for lvl in range(levels):
        wt = bd(p.astype(BF16))
        if lvl < levels - 1:
            tp = dot(jnp.concatenate([tinv, p], axis=0).astype(BF16), wt)
            tinv = tinv + tp[:L]
            p = tp[L:]
        else:
            tinv = tinv + dot(tinv.astype(BF16), wt)
        yield
    tinv = tinv.astype(BF16)
    tk = dot(tinv, jnp.concatenate([bd(kq.astype(BF16)), bd(av[:L].astype(BF16))], axis=1))
    yield
    kq2 = tk[:, :W]
    cu = tk[:, W:]
    ar = dot(a_rb, jnp.concatenate([bd(kq2.astype(BF16)), bd(cu.astype(BF16))], axis=1))
    yield
    rq2 = rq - ar[:, :W]
    yc = av[L:] - ar[:, W:]
    qs = jnp.concatenate([rq2, kq2], axis=0).astype(BF16)
    kb = jnp.concatenate([k * e_rem, b * e_rem], axis=0).astype(BF16)
    return qs, yc, cu, jnp.exp(tot), kb, vb


def _run_in_lockstep(gens):
    results = [None] * len(gens)
    active = list(enumerate(gens))
    while active:
        still = []
        for i, gen in active:
            try:
                next(gen)
                still.append((i, gen))
            except StopIteration as done:
                results[i] = done.value
        active = still
    return results


def _scan_kernel(rf_ref, vf_ref, kkf_ref, lwf_ref, kf_ref, bf_ref, rb_ref, vb_ref, kkb_ref, lwb_ref, kb_ref,
                 bb_ref, yf_ref, yb_ref, s_ref, *, tb, segs, nblk, ng):
    L, W = SCAN_L, SCAN_W
    c = pl.program_id(1)
    start_f = _is_seq_start(c * tb, segs)
    start_b = _is_seq_start((nblk - c) * tb, segs)

    @pl.when(start_f)
    def _():
        s_ref[0] = jnp.zeros(s_ref.shape[1:], F32)

    @pl.when(start_b)
    def _():
        s_ref[1] = jnp.zeros(s_ref.shape[1:], F32)

    n_chunks = tb // L
    dirs = ((rf_ref, vf_ref, kkf_ref, lwf_ref, kf_ref, bf_ref, yf_ref),
            (rb_ref, vb_ref, kkb_ref, lwb_ref, kb_ref, bb_ref, yb_ref))
    streams = [(d, g) for d in range(2) for g in range(ng)]
    order = {d: [n_chunks - 1 - j if d == 1 else j for j in range(n_chunks)] for d in range(2)}
    consts = {d: _scan_consts(reverse=d == 1) for d in range(2)}
    gens = []
    for d, g in streams:
        r_ref, v_ref, kk_ref, lw_ref, k_ref, b_ref, _ = dirs[d]
        cols = slice(g * W, (g + 1) * W)
        for ci in order[d]:
            rows = slice(ci * L, (ci + 1) * L)
            gens.append(_chunk_operators(*(ref[rows, cols].astype(F32)
                                           for ref in (r_ref, v_ref, kk_ref, lw_ref, k_ref, b_ref)),
                                         consts[d], reverse=d == 1))
    ops = _run_in_lockstep(gens)
    states = [s_ref[d, g] for d, g in streams]
    for j in range(n_chunks):
        for si, (d, g) in enumerate(streams):
            qs, yc, cu, decay, kb, vb = ops[si * n_chunks + j]
            ci = order[d][j]
            yu = lax.dot_general(qs, states[si].astype(BF16), _NT, preferred_element_type=F32)
            y_ref = dirs[d][6]
            y_ref[ci * L:(ci + 1) * L, g * W:(g + 1) * W] = (yu[:L] + yc).astype(y_ref.dtype)
            vu = jnp.concatenate([vb, (-(yu[L:] + cu)).astype(BF16)], axis=0)
            upd = lax.dot_general(vu, kb, _TN, preferred_element_type=F32)
            states[si] = states[si] * decay + jnp.where(consts[d][4], upd, 0.0)
    for si, (d, g) in enumerate(streams):
        s_ref[d, g] = states[si]


def _rwkv_scan(r, v, kk, lw, k, b, segs, tb=256, ng=2):
    t, dr = r.shape
    tb = min(tb, t)
    nblk = t // tb
    w = ng * SCAN_W
    fwd = pl.BlockSpec((tb, w), lambda g, c: (c, g))
    bwd = pl.BlockSpec((tb, w), lambda g, c: (nblk - 1 - c, g))
    fwd2 = pl.BlockSpec((None, tb, w), lambda g, c: (0, c, g))
    bwd2 = pl.BlockSpec((None, tb, w), lambda g, c: (1, nblk - 1 - c, g))
    out = jax.ShapeDtypeStruct((t, dr), BF16)
    return pl.pallas_call(
        functools.partial(_scan_kernel, tb=tb, segs=segs, nblk=nblk, ng=ng),
        grid=(dr // w, nblk),
        in_specs=[fwd, fwd, fwd, fwd2, fwd2, fwd2, bwd, bwd, bwd, bwd2, bwd2, bwd2],
        out_specs=[fwd, bwd],
        out_shape=[out, out],
        scratch_shapes=[pltpu.VMEM((2, ng, SCAN_W, SCAN_W), F32)],
        compiler_params=_params("parallel", "arbitrary"),
        name="rwkv_scan",
    )(r, v, kk, lw, k, b, r, v, kk, lw, k, b)


def _post_kernel(yf_ref, yb_ref, g_ref, bonus_ref, gng_ref, gnb_ref, ones_ref, o_ref):
    y = yf_ref[...].astype(F32) + yb_ref[...].astype(F32)
    ones_bd = ones_ref[...]
    mu = _head_sum(y, ones_bd) * (1.0 / RWKV_HEAD)
    cen = y - mu
    var = _head_sum(cen * cen, ones_bd) * (1.0 / RWKV_HEAD)
    yn = cen * lax.rsqrt(var + GN_EPS) * gng_ref[...] + gnb_ref[...]
    o_ref[...] = ((yn + bonus_ref[...].astype(F32)) * g_ref[...].astype(F32)).astype(o_ref.dtype)


def _rwkv_post(y_f, y_b, g, bonus, gn_g, gn_b, ones_bd, tb=256):
    t, dr = y_f.shape
    tb = min(tb, t)
    tok = pl.BlockSpec((tb, dr), lambda i: (i, 0))
    vec = pl.BlockSpec((1, dr), lambda i: (0, 0))
    return pl.pallas_call(
        _post_kernel,
        grid=(t // tb,),
        in_specs=[tok, tok, tok, tok, vec, vec, pl.BlockSpec((SCAN_W, SCAN_W), lambda i: (0, 0))],
        out_specs=tok,
        out_shape=jax.ShapeDtypeStruct((t, dr), BF16),
        compiler_params=_params("parallel"),
        name="rwkv_post",
    )(y_f, y_b, g, bonus, gn_g.reshape(1, dr), gn_b.reshape(1, dr), ones_bd)


def _residual_layernorm(x, m, g, b, alpha):
    h = alpha * x + m.astype(F32)
    mu = jnp.mean(h, axis=-1, keepdims=True)
    cen = h - mu
    var = jnp.mean(cen * cen, axis=-1, keepdims=True)
    return cen * lax.rsqrt(var + LN_EPS) * g + b


def _ln_kernel(x_ref, m_ref, g_ref, b_ref, o_ref, ob_ref, *, alpha):
    y = _residual_layernorm(x_ref[...], m_ref[...], g_ref[...], b_ref[...], alpha)
    o_ref[...] = y
    ob_ref[...] = y.astype(BF16)


def _residual_ln(x, mix, g, b, alpha, tb=256):
    t, dm = x.shape
    tb = min(tb, t)
    tok = pl.BlockSpec((tb, dm), lambda i: (i, 0))
    vec = pl.BlockSpec((1, dm), lambda i: (0, 0))
    return pl.pallas_call(
        functools.partial(_ln_kernel, alpha=alpha),
        grid=(t // tb,),
        in_specs=[tok, tok, vec, vec],
        out_specs=[tok, tok],
        out_shape=[jax.ShapeDtypeStruct((t, dm), F32), jax.ShapeDtypeStruct((t, dm), BF16)],
        compiler_params=_params("parallel"),
        name="residual_ln",
    )(x, mix, g.reshape(1, dm), b.reshape(1, dm))


def _ffn_up_kernel(x_ref, wg_ref, wu_ref, h_ref):
    x = x_ref[...]
    a = jnp.dot(x, wg_ref[...], preferred_element_type=F32)
    u = jnp.dot(x, wu_ref[...], preferred_element_type=F32)
    h_ref[...] = (a * jax.nn.sigmoid(a) * u).astype(h_ref.dtype)


def _ffn_up(xb, wg, wu, tm=1024, tn=512):
    t, dm = xb.shape
    f = wg.shape[1]
    tm = min(tm, t)
    wspec = pl.BlockSpec((dm, tn), lambda i, j: (0, j))
    return pl.pallas_call(
        _ffn_up_kernel,
        grid=(t // tm, f // tn),
        in_specs=[pl.BlockSpec((tm, dm), lambda i, j: (i, 0)), wspec, wspec],
        out_specs=pl.BlockSpec((tm, tn), lambda i, j: (i, j)),
        out_shape=jax.ShapeDtypeStruct((t, f), BF16),
        compiler_params=_params("parallel", "parallel"),
        name="ffn_up",
    )(xb, wg, wu)


def _top2(x, w):
    logits = jnp.dot(x, w, precision=HIGHEST, preferred_element_type=F32)
    lane = lax.broadcasted_iota(jnp.int32, logits.shape, 1)
    neg = jnp.float32(-jnp.inf)
    logits = jnp.where(lane < N_EXPERTS, logits, neg)
    m1 = jnp.max(logits, axis=1, keepdims=True)
    i1 = jnp.min(jnp.where(logits == m1, lane, LANES), axis=1, keepdims=True)
    rest = jnp.where(lane == i1, neg, logits)
    m2 = jnp.max(rest, axis=1, keepdims=True)
    i2 = jnp.min(jnp.where(rest == m2, lane, LANES), axis=1, keepdims=True)
    e2 = jnp.exp(m2 - m1)
    w1 = 1.0 / (1.0 + e2)
    w2 = e2 / (1.0 + e2)
    idx = jnp.where(lane == 0, i1, jnp.where(lane == 1, i2, 0))
    wts = jnp.where(lane == 0, w1, jnp.where(lane == 1, w2, 0.0))
    return idx, wts


def _ln_router_kernel(x_ref, m_ref, g_ref, b_ref, rw_ref, o_ref, ob_ref, idx_ref, wt_ref, *, alpha):
    y = _residual_layernorm(x_ref[...], m_ref[...], g_ref[...], b_ref[...], alpha)
    o_ref[...] = y
    ob_ref[...] = y.astype(BF16)
    idx_ref[...], wt_ref[...] = _top2(y, rw_ref[...])


def _residual_ln_router(x, mix, g, b, alpha, router_w, tb=256):
    t, dm = x.shape
    tb = min(tb, t)
    w = jnp.pad(router_w, ((0, 0), (0, LANES - router_w.shape[1])))
    tok = pl.BlockSpec((tb, dm), lambda i: (i, 0))
    vec = pl.BlockSpec((1, dm), lambda i: (0, 0))
    route = pl.BlockSpec((tb, LANES), lambda i: (i, 0))
    return pl.pallas_call(
        functools.partial(_ln_router_kernel, alpha=alpha),
        grid=(t // tb,),
        in_specs=[tok, tok, vec, vec, pl.BlockSpec((dm, LANES), lambda i: (0, 0))],
        out_specs=[tok, tok, route, route],
        out_shape=[jax.ShapeDtypeStruct((t, dm), F32), jax.ShapeDtypeStruct((t, dm), BF16),
                   jax.ShapeDtypeStruct((t, LANES), jnp.int32), jax.ShapeDtypeStruct((t, LANES), F32)],
        compiler_params=_params("parallel"),
        name="residual_ln_router",
    )(x, mix, g.reshape(1, dm), b.reshape(1, dm), w)


MOE_TM = 512


def _moe_dispatch(idx, wts, tm):
    t = idx.shape[0]
    n_rows = 2 * t + N_EXPERTS * tm
    n_tiles = n_rows // tm
    e_ids = idx[:, :2]
    per_tok = (e_ids[:, :, None] == jnp.arange(N_EXPERTS)).astype(jnp.int32).sum(axis=1)
    csum = jnp.cumsum(per_tok, axis=0)
    counts = csum[-1]
    rank = csum - per_tok
    padded = (counts + tm - 1) // tm * tm
    ends = jnp.cumsum(padded)
    offs = ends - padded
    pos = offs[e_ids] + jnp.take_along_axis(rank, e_ids, axis=1)
    code = jnp.full((n_rows,), -1, jnp.int32).at[pos.reshape(-1)].set(jnp.arange(2 * t, dtype=jnp.int32))
    src = jnp.maximum(code, 0) // 2
    row_w = jnp.where(code >= 0, wts[:, :2].reshape(-1)[jnp.maximum(code, 0)], 0.0)
    tile_start = jnp.arange(n_tiles, dtype=jnp.int32) * tm
    owner = jnp.sum(tile_start[:, None] >= ends[None, :], axis=1).astype(jnp.int32)
    tile_valid = (owner < N_EXPERTS).astype(jnp.int32)
    tile_expert = jnp.minimum(owner, N_EXPERTS - 1)
    return src, row_w, pos.astype(jnp.int32), tile_expert, tile_valid


def _gather_rows(idx_ref, n, src_hbm, dst_ref, sem):
    def issue(r, carry):
        pltpu.make_async_copy(src_hbm.at[pl.ds(idx_ref[0, r], 1), :], dst_ref.at[pl.ds(r, 1), :], sem).start()
        return carry

    lax.fori_loop(0, n, issue, 0)
    pltpu.make_async_copy(src_hbm.at[pl.ds(0, n), :], dst_ref.at[pl.ds(0, n), :], sem).wait()


def _moe_up_sparse_kernel(te_ref, tv_ref, src_ref, x_hbm, wg_ref, wu_ref, roww_ref, h_ref, xbuf, xb_buf, sem, *, tm):
    i = pl.program_id(0)
    valid = tv_ref[i] == 1

    @pl.when(jnp.logical_and(valid, pl.program_id(1) == 0))
    def _():
        _gather_rows(src_ref, tm, x_hbm, xbuf, sem)
        xb_buf[...] = xbuf[...].astype(BF16)

    @pl.when(valid)
    def _():
        x = xb_buf[...]
        a = jnp.dot(x, wg_ref[...], preferred_element_type=F32)
        u = jnp.dot(x, wu_ref[...], preferred_element_type=F32)
        h_ref[...] = (a * jax.nn.sigmoid(a) * u * roww_ref[...]).astype(h_ref.dtype)

    @pl.when(jnp.logical_not(valid))
    def _():
        h_ref[...] = jnp.zeros_like(h_ref)


def _moe_up_sparse(x, we_gate, we_up, src, row_w, tile_expert, tile_valid, tm, tn=512):
    t, dm = x.shape
    n_e, _, fe = we_gate.shape
    n_rows = src.shape[0]
    n_tiles = n_rows // tm
    wspec = pl.BlockSpec((None, dm, tn), lambda i, j, te, tv: (te[i], 0, j))
    grid_spec = pltpu.PrefetchScalarGridSpec(
        num_scalar_prefetch=2,
        grid=(n_tiles, fe // tn),
        in_specs=[pl.BlockSpec((None, 1, tm), lambda i, j, te, tv: (i, 0, 0), memory_space=pltpu.SMEM),
                  pl.BlockSpec(memory_space=pl.ANY),
                  wspec, wspec,
                  pl.BlockSpec((tm, 1), lambda i, j, te, tv: (i, 0))],
        out_specs=pl.BlockSpec((tm, tn), lambda i, j, te, tv: (i, j)),
        scratch_shapes=[pltpu.VMEM((tm, dm), F32), pltpu.VMEM((tm, dm), BF16), pltpu.SemaphoreType.DMA(())],
    )
    return pl.pallas_call(
        functools.partial(_moe_up_sparse_kernel, tm=tm),
        grid_spec=grid_spec,
        out_shape=jax.ShapeDtypeStruct((n_rows, fe), BF16),
        compiler_params=_params("arbitrary", "arbitrary"),
        name="moe_up_sparse",
    )(tile_expert, tile_valid, src.reshape(n_tiles, 1, tm), x, we_gate, we_up, row_w.reshape(n_rows, 1))


def _moe_down_kernel(te_ref, tv_ref, h_ref, w_ref, y_ref):
    valid = tv_ref[pl.program_id(0)] == 1

    @pl.when(valid)
    def _():
        y_ref[...] = jnp.dot(h_ref[...], w_ref[...], preferred_element_type=F32)

    @pl.when(jnp.logical_not(valid))
    def _():
        y_ref[...] = jnp.zeros_like(y_ref)


def _moe_down_sparse(h, we_down, tile_expert, tile_valid, tm, tn=1024):
    n_rows, fe = h.shape
    dm = we_down.shape[2]
    grid_spec = pltpu.PrefetchScalarGridSpec(
        num_scalar_prefetch=2,
        grid=(n_rows // tm, dm // tn),
        in_specs=[pl.BlockSpec((tm, fe), lambda i, j, te, tv: (i, 0)),
                  pl.BlockSpec((None, fe, tn), lambda i, j, te, tv: (te[i], 0, j))],
        out_specs=pl.BlockSpec((tm, tn), lambda i, j, te, tv: (i, j)),
    )
    return pl.pallas_call(
        _moe_down_kernel,
        grid_spec=grid_spec,
        out_shape=jax.ShapeDtypeStruct((n_rows, dm), F32),
        compiler_params=_params("parallel", "parallel"),
        name="moe_down_sparse",
    )(tile_expert, tile_valid, h, we_down)


def _moe_combine_ln_kernel(pos_ref, y_hbm, x_ref, g_ref, b_ref, o_ref, ob_ref, ybuf, sem, *, alpha, tb):
    _gather_rows(pos_ref, 2 * tb, y_hbm, ybuf, sem)
    h = alpha * x_ref[...] + (ybuf[0:tb, :] + ybuf[tb:2 * tb, :])
    mu = jnp.mean(h, axis=-1, keepdims=True)
    cen = h - mu
    var = jnp.mean(cen * cen, axis=-1, keepdims=True)
    y = cen * lax.rsqrt(var + LN_EPS) * g_ref[...] + b_ref[...]
    o_ref[...] = y
    ob_ref[...] = y.astype(BF16)


def _moe_combine_ln(x, ys, pos, g, b, alpha, tb=256):
    t, dm = x.shape
    tb = min(tb, t)
    tok = pl.BlockSpec((tb, dm), lambda i: (i, 0))
    vec = pl.BlockSpec((1, dm), lambda i: (0, 0))
    return pl.pallas_call(
        functools.partial(_moe_combine_ln_kernel, alpha=alpha, tb=tb),
        grid=(t // tb,),
        in_specs=[pl.BlockSpec((None, 1, 2 * tb), lambda i: (i, 0, 0), memory_space=pltpu.SMEM),
                  pl.BlockSpec(memory_space=pl.ANY), tok, vec, vec],
        out_specs=[tok, tok],
        out_shape=[jax.ShapeDtypeStruct((t, dm), F32), jax.ShapeDtypeStruct((t, dm), BF16)],
        scratch_shapes=[pltpu.VMEM((2 * tb, dm), F32), pltpu.SemaphoreType.DMA(())],
        compiler_params=_params("arbitrary"),
        name="moe_combine_ln",
    )(jnp.concatenate([pos[:, 0].reshape(t // tb, 1, tb), pos[:, 1].reshape(t // tb, 1, tb)], axis=2),
      ys, x, g.reshape(1, dm), b.reshape(1, dm))


def _ple_kernel(xb_ref, wg_ref, p_ref, wp_ref, x_ref, o_ref, ob_ref):
    gate = jax.nn.sigmoid(jnp.dot(xb_ref[...], wg_ref[...], preferred_element_type=F32))
    emb = jnp.dot(p_ref[...], wp_ref[...], preferred_element_type=F32)
    y = x_ref[...] + gate * emb
    o_ref[...] = y
    ob_ref[...] = y.astype(BF16)


def _ple(xb, x, w_pgate, p, w_pproj, tm=1024, tn=512):
    t, dm = xb.shape
    pd = p.shape[1]
    tm = min(tm, t)
    out = pl.BlockSpec((tm, tn), lambda i, j: (i, j))
    return pl.pallas_call(
        _ple_kernel,
        grid=(t // tm, dm // tn),
        in_specs=[pl.BlockSpec((tm, dm), lambda i, j: (i, 0)),
                  pl.BlockSpec((dm, tn), lambda i, j: (0, j)),
                  pl.BlockSpec((tm, pd), lambda i, j: (i, 0)),
                  pl.BlockSpec((pd, tn), lambda i, j: (0, j)),
                  out],
        out_specs=[out, out],
        out_shape=[jax.ShapeDtypeStruct((t, dm), F32), jax.ShapeDtypeStruct((t, dm), BF16)],
        compiler_params=_params("parallel", "parallel"),
        name="ple_gate",
    )(xb, w_pgate, p, w_pproj, x)


def _pad_cols(w, width):
    return jnp.pad(w, ((0, 0), (0, width - w.shape[1])))


def _pad_rows(w, rows):
    pad = [(0, 0)] * w.ndim
    pad[-2] = (0, rows - w.shape[-2])
    return jnp.pad(w, pad)


def _split_rwkv_cols(w, dg, dr, has_mv):
    base = 2 * dg
    rkv = w[:, base:base + 3 * dr]
    o = base + 3 * dr
    sizes = [DECAY_LORA, DECAY_LORA, AAA_LORA, AAA_LORA, GATE_LORA] + ([MV_LORA] if has_mv else [])
    slots = [LORA_SLOT, LORA_SLOT, LORA_SLOT, LORA_SLOT, GATE_LORA, LORA_SLOT]
    parts = []
    for n, slot in zip(sizes, slots):
        parts.append(_pad_cols(w[:, o:o + n], slot))
        o += n
    lora = _pad_cols(jnp.concatenate(parts, axis=1), LORA_COLS)
    return jnp.concatenate([rkv, lora], axis=1)


def _trunk(x, p, P, segs):
    t, dm = x.shape
    depth = P['w_out'].shape[0]
    alpha = (2 * depth) ** 0.25
    dg = P['sgu_ln_g'].shape[1]
    dr = P['k_k'].shape[1]
    hb = lax.broadcasted_iota(jnp.int32, (SCAN_W, SCAN_W), 0) // RWKV_HEAD
    hc = lax.broadcasted_iota(jnp.int32, (SCAN_W, SCAN_W), 1) // RWKV_HEAD
    ones_bd = (hb == hc).astype(BF16)
    xb = x.astype(BF16)
    w_out_b = _to_bf16(P['w_out'])
    w_pgate_b = _to_bf16(P['w_pgate'])
    v_first = None
    for l in range(depth):
        if l == 0:
            w_in_l, conv_l, v0_l, v2_l = P['w_in0'], P['conv0'], None, None
        else:
            w_in_l, conv_l, v0_l, v2_l = P['w_in'][l - 1], P['conv'][l - 1], P['v0'][l - 1], P['v2'][l - 1]
        w_in_b = _to_bf16(w_in_l)
        uv = _matmul(xb, w_in_b[:, :2 * dg], BF16)
        zr = _matmul(xb, _split_rwkv_cols(w_in_b, dg, dr, l > 0), BF16)
        y_g = _spatial_gating(uv, P['sgu_ln_g'][l], P['sgu_ln_b'][l], P['w_s'][l], P['b_s'][l])
        conv_r = _split_rwkv_cols(jnp.pad(conv_l, ((0, 0), (2 * dg, 0))), dg, dr, l > 0)
        prep_args = (zr, conv_r, P['w0'][l], _pad_rows(P['w2'][l], LORA_SLOT).astype(BF16), P['a0'][l],
                     _pad_rows(P['a2'][l], LORA_SLOT).astype(BF16), P['g2'][l].astype(BF16),
                     P['k_k'][l], P['k_a'][l], P['r_k'][l].reshape(-1), ones_bd, segs)
        if l == 0:
            r, v, kk, g, bonus, lw, k, b = _rwkv_prep(*prep_args)
            v_first = v
        else:
            r, v, kk, g, bonus, lw, k, b = _rwkv_prep(*prep_args, v_first=v_first, v0=v0_l,
                                                      v2=_pad_rows(v2_l, LORA_SLOT).astype(BF16))
        y_f, y_b = _rwkv_scan(r, v, kk, lw, k, b, segs)
        y_r = _rwkv_post(y_f, y_b, g, bonus, P['gn_g'][l], P['gn_b'][l], ones_bd)
        w_out = w_out_b[l]
        mix = _matmul_pair(y_g, w_out[:dg], y_r, w_out[dg:], BF16)
        j = l // 2
        if l % 2 == 0:
            x, xb = _residual_ln(x, mix, P['ln1_g'][l], P['ln1_b'][l], alpha)
            h = _ffn_up(xb, _to_bf16(P['w_ff_gate'][j]), _to_bf16(P['w_ff_up'][j]))
            ff = _matmul(h, _to_bf16(P['w_ff_down'][j]), BF16, tk=2048)
        else:
            x, xb, idx, wts = _residual_ln_router(x, mix, P['ln1_g'][l], P['ln1_b'][l], alpha, P['router'][j])
            tm = min(MOE_TM, t)
            src, row_w, pos, tile_expert, tile_valid = _moe_dispatch(idx, wts, tm)
            h = _moe_up_sparse(x, _to_bf16(P['we_gate'][j]), _to_bf16(P['we_up'][j]), src, row_w,
                               tile_expert, tile_valid, tm)
            ys = _moe_down_sparse(h, _to_bf16(P['we_down'][j]), tile_expert, tile_valid, tm)
            x, xb = _moe_combine_ln(x, ys, pos, P['ln2_g'][l], P['ln2_b'][l], alpha)
        if l % 2 == 0:
            x, xb = _residual_ln(x, ff, P['ln2_g'][l], P['ln2_b'][l], alpha)
        x, xb = _ple(xb, x, w_pgate_b[l], p[l].astype(BF16), P['w_pproj'][l].astype(BF16))
    return x


def kernel(x_prompt, x_sample, p_prompt, p_sample, w_in0, conv0, w_in, conv, sgu_ln_g, sgu_ln_b, w_s, b_s, w0, w2, a0, a2, g2, k_k, k_a, r_k, gn_g, gn_b, v0, v2, w_out, ln1_g, ln1_b, ln2_g, ln2_b, w_ff_gate, w_ff_up, w_ff_down, router, we_gate, we_up, we_down, w_pproj, w_pgate):
    P = dict(w_in0=w_in0, conv0=conv0, w_in=w_in, conv=conv, sgu_ln_g=sgu_ln_g, sgu_ln_b=sgu_ln_b,
             w_s=w_s, b_s=b_s, w0=w0, w2=w2, a0=a0, a2=a2, g2=g2, k_k=k_k, k_a=k_a, r_k=r_k,
             gn_g=gn_g, gn_b=gn_b, v0=v0, v2=v2, w_out=w_out, ln1_g=ln1_g, ln1_b=ln1_b,
             ln2_g=ln2_g, ln2_b=ln2_b, w_ff_gate=w_ff_gate, w_ff_up=w_ff_up, w_ff_down=w_ff_down,
             router=router, we_gate=we_gate, we_up=we_up, we_down=we_down,
             w_pproj=w_pproj, w_pgate=w_pgate)
    b1, s1, dm = x_prompt.shape
    b2, s2, _ = x_sample.shape
    depth, pd = p_prompt.shape[0], p_prompt.shape[-1]
    t1, t2 = b1 * s1, b2 * s2
    segs = ((t1, s1), (t2, s2))
    x = jnp.concatenate([x_prompt.reshape(t1, dm), x_sample.reshape(t2, dm)], axis=0)
    p = jnp.concatenate([p_prompt.reshape(depth, t1, pd), p_sample.reshape(depth, t2, pd)], axis=1)
    y = _trunk(x, p, P, segs)
    return y[:t1].reshape(b1, s1, dm), y[t1:].reshape(b2, s2, dm)
```

```python
import functools

import jax
import jax.numpy as jnp
from jax import lax
from jax.experimental import pallas as pl
from jax.experimental.pallas import tpu as pltpu

F32 = jnp.float32
BF16 = jnp.bfloat16
HIGHEST = lax.Precision.HIGHEST

LANES = 128
BF16_SUBLANES = 16
VMEM_LIMIT_BYTES = 56 * 1024 * 1024

CHUNK = 128
GMLP_HEAD = 128
RWKV_HEAD = 64
DECAY_LORA = 96
AAA_LORA = 96
MV_LORA = 64
GATE_LORA = 256
N_EXPERTS = 8
LN_EPS = 1e-5
GN_EPS = 64e-5
L2_EPS = 1e-12
DECAY_SCALE = 0.6065306597126334

SCAN_L = 64
SCAN_W = 256
HEADS_PER_GROUP = SCAN_W // RWKV_HEAD
LORA_SLOT = 128
LORA_COLS = 1024


def _params(*sem):
    return pltpu.CompilerParams(dimension_semantics=sem, vmem_limit_bytes=VMEM_LIMIT_BYTES)


def _is_seq_start(pos, segs):
    (t1, s1), (_, s2) = segs
    return jnp.where(pos < t1, pos % s1 == 0, (pos - t1) % s2 == 0)


CAST_BLOCK_BYTES = 8 * 1024 * 1024


def _cast_kernel(x_ref, o_ref):
    o_ref[...] = x_ref[...].astype(o_ref.dtype)


def _to_bf16(w):
    w2 = w.reshape(-1, w.shape[-1])
    rows, n = w2.shape
    tb = rows
    while tb * n * w2.dtype.itemsize > CAST_BLOCK_BYTES and tb % 2 == 0 and (tb // 2) % BF16_SUBLANES == 0:
        tb //= 2
    out = pl.pallas_call(
        _cast_kernel,
        grid=(rows // tb,),
        in_specs=[pl.BlockSpec((tb, n), lambda i: (i, 0))],
        out_specs=pl.BlockSpec((tb, n), lambda i: (i, 0)),
        out_shape=jax.ShapeDtypeStruct((rows, n), BF16),
        compiler_params=_params("parallel"),
        name="cast_bf16",
    )(w2)
    return out.reshape(w.shape)


def _mm_kernel(x_ref, w_ref, o_ref, *scratch, nk):
    if nk == 1:
        o_ref[...] = jnp.dot(x_ref[...], w_ref[...], preferred_element_type=F32).astype(o_ref.dtype)
        return
    acc_ref, = scratch
    k = pl.program_id(2)

    @pl.when(k == 0)
    def _():
        acc_ref[...] = jnp.zeros_like(acc_ref)

    acc_ref[...] += jnp.dot(x_ref[...], w_ref[...], preferred_element_type=F32)

    @pl.when(k == nk - 1)
    def _():
        o_ref[...] = acc_ref[...].astype(o_ref.dtype)


def _matmul(x, w, out_dtype, tm=1024, tn=1024, tk=None):
    m, kdim = x.shape
    n = w.shape[1]
    tm, tn = min(tm, m), min(tn, n)
    tk = kdim if tk is None else min(tk, kdim)
    nk = kdim // tk
    assert m % tm == 0 and n % tn == 0 and kdim % tk == 0
    scratch = [] if nk == 1 else [pltpu.VMEM((tm, tn), F32)]
    return pl.pallas_call(
        functools.partial(_mm_kernel, nk=nk),
        grid=(m // tm, n // tn, nk),
        in_specs=[pl.BlockSpec((tm, tk), lambda i, j, k: (i, k)),
                  pl.BlockSpec((tk, tn), lambda i, j, k: (k, j))],
        out_specs=pl.BlockSpec((tm, tn), lambda i, j, k: (i, j)),
        out_shape=jax.ShapeDtypeStruct((m, n), out_dtype),
        scratch_shapes=scratch,
        compiler_params=_params("parallel", "parallel", "arbitrary"),
        name="matmul",
    )(x, w)


def _mm_pair_kernel(xa_ref, wa_ref, xb_ref, wb_ref, o_ref):
    acc = jnp.dot(xa_ref[...], wa_ref[...], preferred_element_type=F32)
    acc += jnp.dot(xb_ref[...], wb_ref[...], preferred_element_type=F32)
    o_ref[...] = acc.astype(o_ref.dtype)


def _matmul_pair(xa, wa, xb, wb, out_dtype, tm=1024, tn=1024):
    m, ka = xa.shape
    kb = xb.shape[1]
    n = wa.shape[1]
    tm, tn = min(tm, m), min(tn, n)
    return pl.pallas_call(
        _mm_pair_kernel,
        grid=(m // tm, n // tn),
        in_specs=[pl.BlockSpec((tm, ka), lambda i, j: (i, 0)),
                  pl.BlockSpec((ka, tn), lambda i, j: (0, j)),
                  pl.BlockSpec((tm, kb), lambda i, j: (i, 0)),
                  pl.BlockSpec((kb, tn), lambda i, j: (0, j))],
        out_specs=pl.BlockSpec((tm, tn), lambda i, j: (i, j)),
        out_shape=jax.ShapeDtypeStruct((m, n), out_dtype),
        compiler_params=_params("parallel", "parallel"),
        name="matmul_pair",
    )(xa, wa, xb, wb)


def _sgu_kernel(u_ref, v_ref, g_ref, b_ref, ws_ref, bs_ref, o_ref, *, n_chunks):
    gv = jax.nn.gelu(v_ref[...].astype(F32))
    mu = jnp.mean(gv, axis=-1, keepdims=True)
    cen = gv - mu
    var = jnp.mean(cen * cen, axis=-1, keepdims=True)
    vn = (cen * lax.rsqrt(var + LN_EPS) * g_ref[...] + b_ref[...]).astype(BF16)
    n_heads = ws_ref.shape[0]
    for c in range(n_chunks):
        rows = slice(c * CHUNK, (c + 1) * CHUNK)
        for h in range(n_heads):
            cols = slice(h * GMLP_HEAD, (h + 1) * GMLP_HEAD)
            sv = jnp.dot(ws_ref[h], vn[rows, cols], preferred_element_type=F32) + bs_ref[:, cols]
            gu = jax.nn.gelu(u_ref[rows, cols].astype(F32))
            o_ref[rows, cols] = (gu * sv).astype(o_ref.dtype)


def _spatial_gating(uv, ln_g, ln_b, w_s, b_s, tb=512):
    t = uv.shape[0]
    dg = uv.shape[1] // 2
    tb = min(tb, t)
    n_heads = w_s.shape[0]
    bs_full = jnp.repeat(b_s.T, GMLP_HEAD, axis=1)
    return pl.pallas_call(
        functools.partial(_sgu_kernel, n_chunks=tb // CHUNK),
        grid=(t // tb,),
        in_specs=[pl.BlockSpec((tb, dg), lambda i: (i, 0)),
                  pl.BlockSpec((tb, dg), lambda i: (i, 1)),
                  pl.BlockSpec((1, dg), lambda i: (0, 0)),
                  pl.BlockSpec((1, dg), lambda i: (0, 0)),
                  pl.BlockSpec((n_heads, CHUNK, CHUNK), lambda i: (0, 0, 0)),
                  pl.BlockSpec((CHUNK, dg), lambda i: (0, 0))],
        out_specs=pl.BlockSpec((tb, dg), lambda i: (i, 0)),
        out_shape=jax.ShapeDtypeStruct((t, dg), BF16),
        compiler_params=_params("parallel"),
        name="spatial_gating",
    )(uv, uv, ln_g.reshape(1, dg), ln_b.reshape(1, dg), w_s.astype(BF16), bs_full)


def _head_sum(x, ones_bd):
    parts = []
    for g in range(x.shape[1] // SCAN_W):
        xs = x[:, g * SCAN_W:(g + 1) * SCAN_W]
        hi = xs.astype(BF16)
        lo = (xs - hi.astype(F32)).astype(BF16)
        parts.append(jnp.dot(hi, ones_bd, preferred_element_type=F32)
                     + jnp.dot(lo, ones_bd, preferred_element_type=F32))
    return jnp.concatenate(parts, axis=1)


def _prep_kernel(*refs, layer1, tb, segs, dr):
    if layer1:
        (z_ref, zp_ref, zn_ref, conv_ref, w0_ref, w2_ref, a0_ref, a2_ref, g2_ref, kk_ref, ka_ref, rk_ref,
         ones_ref, vf_ref, v0_ref, v2_ref, r_o, v_o, kkn_o, g_o, bonus_o, lw_o, k_o, b_o) = refs
    else:
        (z_ref, zp_ref, zn_ref, conv_ref, w0_ref, w2_ref, a0_ref, a2_ref, g2_ref, kk_ref, ka_ref, rk_ref,
         ones_ref, r_o, v_o, kkn_o, g_o, bonus_o, lw_o, k_o, b_o) = refs
    pos0 = pl.program_id(0) * tb
    first = _is_seq_start(pos0, segs)
    last = _is_seq_start(pos0 + tb, segs)
    row = lax.broadcasted_iota(jnp.int32, (tb, 1), 0)

    def conv(c0, width):
        cols = slice(c0, c0 + width)
        z = z_ref[:, cols].astype(F32)
        prev_row = jnp.where(first, 0.0, zp_ref[:, cols].astype(F32)[BF16_SUBLANES - 1:BF16_SUBLANES])
        next_row = jnp.where(last, 0.0, zn_ref[:, cols].astype(F32)[0:1])
        zprev = jnp.where(row == 0, prev_row, pltpu.roll(z, 1, 0))
        znext = jnp.where(row == tb - 1, next_row, pltpu.roll(z, tb - 1, 0))
        return zprev * conv_ref[0:1, cols] + z * conv_ref[1:2, cols] + znext * conv_ref[2:3, cols]

    lo = conv(3 * dr, LORA_COLS)
    s = LORA_SLOT
    wd = (lo[:, 0:s], lo[:, s:2 * s])
    ad = (lo[:, 2 * s:3 * s], lo[:, 3 * s:4 * s])
    gd = lo[:, 4 * s:6 * s]

    def lora(x, w):
        return jnp.dot(x.astype(BF16), w, preferred_element_type=F32)

    g_o[...] = lora(jax.nn.sigmoid(gd), g2_ref[...]).astype(g_o.dtype)
    v = conv(2 * dr, dr)
    if layer1:
        mv = lo[:, 6 * s:7 * s]
        v = v + (vf_ref[...].astype(F32) - v) * jax.nn.sigmoid(v0_ref[...] + lora(mv, v2_ref[...]))
    v_o[...] = v.astype(v_o.dtype)
    r = conv(0, dr)
    r_o[...] = r.astype(r_o.dtype)
    k = conv(dr, dr)
    ones_bd = ones_ref[...]
    kk = k * kk_ref[...]
    kk = kk * lax.rsqrt(_head_sum(kk * kk, ones_bd) + L2_EPS)
    kkn_o[...] = kk.astype(kkn_o.dtype)
    for d in range(2):
        zw = w0_ref[d:d + 1, :] + lora(jnp.tanh(wd[d]), w2_ref[d])
        lw_o[d] = -DECAY_SCALE / (1.0 + jnp.exp(-zw))
        a = jax.nn.sigmoid(a0_ref[d:d + 1, :] + lora(ad[d], a2_ref[d]))
        k_d = k * (1.0 + (a - 1.0) * ka_ref[...])
        k_o[d] = k_d.astype(k_o.dtype)
        b_o[d] = (kk * a).astype(b_o.dtype)
        if d == 0:
            bonus_o[...] = (_head_sum(r * k_d * rk_ref[...], ones_bd) * v).astype(bonus_o.dtype)


def _rwkv_prep(zr, conv_w, w0, w2, a0, a2, g2, k_k, k_a, r_k, ones_bd, segs, v_first=None, v0=None, v2=None,
               tb=128):
    t, c = zr.shape
    dr = k_k.shape[-1]
    tb = min(tb, t)
    layer1 = v_first is not None
    hb = tb // BF16_SUBLANES
    n_halo = t // BF16_SUBLANES
    const2 = lambda i: (0, 0)
    const3 = lambda i: (0, 0, 0)
    in_specs = [pl.BlockSpec((tb, c), lambda i: (i, 0)),
                pl.BlockSpec((BF16_SUBLANES, c), lambda i: (jnp.maximum(i * hb - 1, 0), 0)),
                pl.BlockSpec((BF16_SUBLANES, c), lambda i: (jnp.minimum((i + 1) * hb, n_halo - 1), 0)),
                pl.BlockSpec((3, c), const2),
                pl.BlockSpec((2, dr), const2),
                pl.BlockSpec((2, LORA_SLOT, dr), const3),
                pl.BlockSpec((2, dr), const2),
                pl.BlockSpec((2, LORA_SLOT, dr), const3),
                pl.BlockSpec((GATE_LORA, dr), const2),
                pl.BlockSpec((1, dr), const2),
                pl.BlockSpec((1, dr), const2),
                pl.BlockSpec((1, dr), const2),
                pl.BlockSpec((SCAN_W, SCAN_W), const2)]
    args = [zr, zr, zr, conv_w, w0, w2, a0, a2, g2, k_k.reshape(1, dr), k_a.reshape(1, dr), r_k.reshape(1, dr),
            ones_bd]
    if layer1:
        in_specs += [pl.BlockSpec((tb, dr), lambda i: (i, 0)),
                     pl.BlockSpec((1, dr), const2),
                     pl.BlockSpec((LORA_SLOT, dr), const2)]
        args += [v_first, v0.reshape(1, dr), v2]
    tok = pl.BlockSpec((tb, dr), lambda i: (i, 0))
    tok2 = pl.BlockSpec((2, tb, dr), lambda i: (0, i, 0))
    one = jax.ShapeDtypeStruct((t, dr), BF16)
    two = jax.ShapeDtypeStruct((2, t, dr), BF16)
    log_decay = jax.ShapeDtypeStruct((2, t, dr), F32)
    return pl.pallas_call(
        functools.partial(_prep_kernel, layer1=layer1, tb=tb, segs=segs, dr=dr),
        grid=(t // tb,),
        in_specs=in_specs,
        out_specs=[tok, tok, tok, tok, tok, tok2, tok2, tok2],
        out_shape=[one, one, one, one, one, log_decay, two, two],
        compiler_params=_params("parallel"),
        name="rwkv_prep",
    )(*args)


def _scan_consts(reverse):
    L, W, G = SCAN_L, SCAN_W, HEADS_PER_GROUP
    sgn = -1 if reverse else 1
    t_i = lax.broadcasted_iota(jnp.int32, (L, 3 * L), 0)
    s_i = lax.broadcasted_iota(jnp.int32, (L, 3 * L), 1) % L
    tri3 = jnp.where(sgn * (t_i - s_i) >= 0, 1.0, 0.0).astype(BF16)
    tc = lax.broadcasted_iota(jnp.int32, (L, G * L), 0)
    sc = lax.broadcasted_iota(jnp.int32, (L, G * L), 1) % L
    before = sgn * (tc - sc)
    strict = before > 0
    rmask = before > 0 if reverse else before >= 0
    eye = jnp.where(sc == tc, 1.0, 0.0).astype(F32)
    bi = lax.broadcasted_iota(jnp.int32, (W, W), 0) // RWKV_HEAD
    bj = lax.broadcasted_iota(jnp.int32, (W, W), 1) // RWKV_HEAD
    return tri3, strict, rmask, eye, bi == bj


def _bd(x, bdmask):
    return jnp.where(bdmask, jnp.tile(x, (HEADS_PER_GROUP, 1)), jnp.zeros((), x.dtype))


_NT = (((1,), (1,)), ((), ()))
_TN = (((0,), (0,)), ((), ()))


def _chunk_operators(r, v, kk, lw, k, b, consts, reverse):
    L, W, G = SCAN_L, SCAN_W, HEADS_PER_GROUP
    tri3, strict, rmask, eye, bdmask = consts
    bd = functools.partial(_bd, bdmask=bdmask)

    def dot(a_, b_):
        return jnp.dot(a_, b_, preferred_element_type=F32)

    hi = lw.astype(BF16)
    rem = lw - hi.astype(F32)
    mid = rem.astype(BF16)
    low = (rem - mid.astype(F32)).astype(BF16)
    cum = dot(tri3, jnp.concatenate([hi, mid, low], axis=0))
    yield
    tot = jnp.sum(lw, axis=0, keepdims=True)
    e_ex = jnp.exp(cum - lw)
    e_neg = jnp.exp(-cum)
    e_rem = jnp.exp(tot - cum)
    kq = kk * e_ex
    rq = r * (e_ex if reverse else jnp.exp(cum))
    vb = v.astype(BF16)
    q = jnp.concatenate([kq, rq], axis=0).astype(BF16)
    keys = jnp.concatenate([bd((k * e_neg).astype(BF16)), bd((b * e_neg).astype(BF16))], axis=0)
    a = lax.dot_general(q, keys, _NT, preferred_element_type=F32)
    yield
    a_kk = jnp.where(strict, a[:L, :G * L], 0.0)
    a_kb = jnp.where(strict, a[:L, G * L:], 0.0)
    a_rk = jnp.where(rmask, a[L:, :G * L], 0.0)
    a_rb = jnp.where(rmask, a[L:, G * L:], 0.0).astype(BF16)
    p = -a_kb
    tinv = eye + p
    av = dot(jnp.concatenate([a_kk, a_rk], axis=0).astype(BF16), bd(vb))
    p = dot(p.astype(BF16), bd(p.astype(BF16)))
    yield
    levels = L.bit_length() - 2
    for lvl in range(levels):
        wt = bd(p.astype(BF16))
        if lvl < levels - 1:
            tp = dot(jnp.concatenate([tinv, p], axis=0).astype(BF16), wt)
            tinv = tinv + tp[:L]
            p = tp[L:]
        else:
            tinv = tinv + dot(tinv.astype(BF16), wt)
        yield
    tinv = tinv.astype(BF16)
    tk = dot(tinv, jnp.concatenate([bd(kq.astype(BF16)), bd(av[:L].astype(BF16))], axis=1))
    yield
    kq2 = tk[:, :W]
    cu = tk[:, W:]
    ar = dot(a_rb, jnp.concatenate([bd(kq2.astype(BF16)), bd(cu.astype(BF16))], axis=1))
    yield
    rq2 = rq - ar[:, :W]
    yc = av[L:] - ar[:, W:]
    qs = jnp.concatenate([rq2, kq2], axis=0).astype(BF16)
    kb = jnp.concatenate([k * e_rem, b * e_rem], axis=0).astype(BF16)
    return qs, yc, cu, jnp.exp(tot), kb, vb


def _run_in_lockstep(gens):
    results = [None] * len(gens)
    active = list(enumerate(gens))
    while active:
        still = []
        for i, gen in active:
            try:
                next(gen)
                still.append((i, gen))
            except StopIteration as done:
                results[i] = done.value
        active = still
    return results


def _scan_kernel(rf_ref, vf_ref, kkf_ref, lwf_ref, kf_ref, bf_ref, rb_ref, vb_ref, kkb_ref, lwb_ref, kb_ref,
                 bb_ref, yf_ref, yb_ref, s_ref, *, tb, segs, nblk, ng):
    L, W = SCAN_L, SCAN_W
    c = pl.program_id(1)
    start_f = _is_seq_start(c * tb, segs)
    start_b = _is_seq_start((nblk - c) * tb, segs)

    @pl.when(start_f)
    def _():
        s_ref[0] = jnp.zeros(s_ref.shape[1:], F32)

    @pl.when(start_b)
    def _():
        s_ref[1] = jnp.zeros(s_ref.shape[1:], F32)

    n_chunks = tb // L
    dirs = ((rf_ref, vf_ref, kkf_ref, lwf_ref, kf_ref, bf_ref, yf_ref),
            (rb_ref, vb_ref, kkb_ref, lwb_ref, kb_ref, bb_ref, yb_ref))
    streams = [(d, g) for d in range(2) for g in range(ng)]
    order = {d: [n_chunks - 1 - j if d == 1 else j for j in range(n_chunks)] for d in range(2)}
    consts = {d: _scan_consts(reverse=d == 1) for d in range(2)}
    gens = []
    for d, g in streams:
        r_ref, v_ref, kk_ref, lw_ref, k_ref, b_ref, _ = dirs[d]
        cols = slice(g * W, (g + 1) * W)
        for ci in order[d]:
            rows = slice(ci * L, (ci + 1) * L)
            gens.append(_chunk_operators(*(ref[rows, cols].astype(F32)
                                           for ref in (r_ref, v_ref, kk_ref, lw_ref, k_ref, b_ref)),
                                         consts[d], reverse=d == 1))
    ops = _run_in_lockstep(gens)
    states = [s_ref[d, g] for d, g in streams]
    for j in range(n_chunks):
        for si, (d, g) in enumerate(streams):
            qs, yc, cu, decay, kb, vb = ops[si * n_chunks + j]
            ci = order[d][j]
            yu = lax.dot_general(qs, states[si].astype(BF16), _NT, preferred_element_type=F32)
            y_ref = dirs[d][6]
            y_ref[ci * L:(ci + 1) * L, g * W:(g + 1) * W] = (yu[:L] + yc).astype(y_ref.dtype)
            vu = jnp.concatenate([vb, (-(yu[L:] + cu)).astype(BF16)], axis=0)
            upd = lax.dot_general(vu, kb, _TN, preferred_element_type=F32)
            states[si] = states[si] * decay + jnp.where(consts[d][4], upd, 0.0)
    for si, (d, g) in enumerate(streams):
        s_ref[d, g] = states[si]


def _rwkv_scan(r, v, kk, lw, k, b, segs, tb=256, ng=2):
    t, dr = r.shape
    tb = min(tb, t)
    nblk = t // tb
    w = ng * SCAN_W
    fwd = pl.BlockSpec((tb, w), lambda g, c: (c, g))
    bwd = pl.BlockSpec((tb, w), lambda g, c: (nblk - 1 - c, g))
    fwd2 = pl.BlockSpec((None, tb, w), lambda g, c: (0, c, g))
    bwd2 = pl.BlockSpec((None, tb, w), lambda g, c: (1, nblk - 1 - c, g))
    out = jax.ShapeDtypeStruct((t, dr), BF16)
    return pl.pallas_call(
        functools.partial(_scan_kernel, tb=tb, segs=segs, nblk=nblk, ng=ng),
        grid=(dr // w, nblk),
        in_specs=[fwd, fwd, fwd, fwd2, fwd2, fwd2, bwd, bwd, bwd, bwd2, bwd2, bwd2],
        out_specs=[fwd, bwd],
        out_shape=[out, out],
        scratch_shapes=[pltpu.VMEM((2, ng, SCAN_W, SCAN_W), F32)],
        compiler_params=_params("parallel", "arbitrary"),
        name="rwkv_scan",
    )(r, v, kk, lw, k, b, r, v, kk, lw, k, b)


def _post_kernel(yf_ref, yb_ref, g_ref, bonus_ref, gng_ref, gnb_ref, ones_ref, o_ref):
    y = yf_ref[...].astype(F32) + yb_ref[...].astype(F32)
    ones_bd = ones_ref[...]
    mu = _head_sum(y, ones_bd) * (1.0 / RWKV_HEAD)
    cen = y - mu
    var = _head_sum(cen * cen, ones_bd) * (1.0 / RWKV_HEAD)
    yn = cen * lax.rsqrt(var + GN_EPS) * gng_ref[...] + gnb_ref[...]
    o_ref[...] = ((yn + bonus_ref[...].astype(F32)) * g_ref[...].astype(F32)).astype(o_ref.dtype)


def _rwkv_post(y_f, y_b, g, bonus, gn_g, gn_b, ones_bd, tb=256):
    t, dr = y_f.shape
    tb = min(tb, t)
    tok = pl.BlockSpec((tb, dr), lambda i: (i, 0))
    vec = pl.BlockSpec((1, dr), lambda i: (0, 0))
    return pl.pallas_call(
        _post_kernel,
        grid=(t // tb,),
        in_specs=[tok, tok, tok, tok, vec, vec, pl.BlockSpec((SCAN_W, SCAN_W), lambda i: (0, 0))],
        out_specs=tok,
        out_shape=jax.ShapeDtypeStruct((t, dr), BF16),
        compiler_params=_params("parallel"),
        name="rwkv_post",
    )(y_f, y_b, g, bonus, gn_g.reshape(1, dr), gn_b.reshape(1, dr), ones_bd)


def _residual_layernorm(x, m, g, b, alpha):
    h = alpha * x + m.astype(F32)
    mu = jnp.mean(h, axis=-1, keepdims=True)
    cen = h - mu
    var = jnp.mean(cen * cen, axis=-1, keepdims=True)
    return cen * lax.rsqrt(var + LN_EPS) * g + b


def _ln_kernel(*refs, alpha, n_first):
    *x_refs, m_ref, g_ref, b_ref, o_ref, ob_ref = refs
    if n_first is None:
        x = x_refs[0][...]
    else:
        x = jnp.where(pl.program_id(0) < n_first, x_refs[0][...], x_refs[1][...])
    y = _residual_layernorm(x, m_ref[...], g_ref[...], b_ref[...], alpha)
    o_ref[...] = y
    ob_ref[...] = y.astype(BF16)


def _residual_ln(x, mix, g, b, alpha, tb=256):
    t, dm = mix.shape
    tb = min(tb, t)
    tok = pl.BlockSpec((tb, dm), lambda i: (i, 0))
    vec = pl.BlockSpec((1, dm), lambda i: (0, 0))
    if isinstance(x, tuple):
        xa, xb = x
        n_first = xa.shape[0] // tb
        assert xa.shape[0] % tb == 0 and xb.shape[0] % tb == 0
        n_last = xb.shape[0] // tb - 1
        x_args = [xa, xb]
        x_specs = [pl.BlockSpec((tb, dm), lambda i: (jnp.minimum(i, n_first - 1), 0)),
                   pl.BlockSpec((tb, dm), lambda i: (jnp.clip(i - n_first, 0, n_last), 0))]
    else:
        n_first, x_args, x_specs = None, [x], [tok]
    return pl.pallas_call(
        functools.partial(_ln_kernel, alpha=alpha, n_first=n_first),
        grid=(t // tb,),
        in_specs=x_specs + [tok, vec, vec],
        out_specs=[tok, tok],
        out_shape=[jax.ShapeDtypeStruct((t, dm), F32), jax.ShapeDtypeStruct((t, dm), BF16)],
        compiler_params=_params("parallel"),
        name="residual_ln",
    )(*x_args, mix, g.reshape(1, dm), b.reshape(1, dm))


def _ffn_up_kernel(x_ref, wg_ref, wu_ref, h_ref):
    x = x_ref[...]
    a = jnp.dot(x, wg_ref[...], preferred_element_type=F32)
    u = jnp.dot(x, wu_ref[...], preferred_element_type=F32)
    h_ref[...] = (a * jax.nn.sigmoid(a) * u).astype(h_ref.dtype)


def _ffn_up(xb, wg, wu, tm=1024, tn=512):
    t, dm = xb.shape
    f = wg.shape[1]
    tm = min(tm, t)
    wspec = pl.BlockSpec((dm, tn), lambda i, j: (0, j))
    return pl.pallas_call(
        _ffn_up_kernel,
        grid=(t // tm, f // tn),
        in_specs=[pl.BlockSpec((tm, dm), lambda i, j: (i, 0)), wspec, wspec],
        out_specs=pl.BlockSpec((tm, tn), lambda i, j: (i, j)),
        out_shape=jax.ShapeDtypeStruct((t, f), BF16),
        compiler_params=_params("parallel", "parallel"),
        name="ffn_up",
    )(xb, wg, wu)


def _top2(x, w):
    logits = jnp.dot(x, w, precision=HIGHEST, preferred_element_type=F32)
    lane = lax.broadcasted_iota(jnp.int32, logits.shape, 1)
    neg = jnp.float32(-jnp.inf)
    logits = jnp.where(lane < N_EXPERTS, logits, neg)
    m1 = jnp.max(logits, axis=1, keepdims=True)
    i1 = jnp.min(jnp.where(logits == m1, lane, LANES), axis=1, keepdims=True)
    rest = jnp.where(lane == i1, neg, logits)
    m2 = jnp.max(rest, axis=1, keepdims=True)
    i2 = jnp.min(jnp.where(rest == m2, lane, LANES), axis=1, keepdims=True)
    e2 = jnp.exp(m2 - m1)
    w1 = 1.0 / (1.0 + e2)
    w2 = e2 / (1.0 + e2)
    idx = jnp.where(lane == 0, i1, jnp.where(lane == 1, i2, 0))
    wts = jnp.where(lane == 0, w1, jnp.where(lane == 1, w2, 0.0))
    return idx, wts


def _ln_router_kernel(x_ref, m_ref, g_ref, b_ref, rw_ref, o_ref, ob_ref, idx_ref, wt_ref, *, alpha):
    y = _residual_layernorm(x_ref[...], m_ref[...], g_ref[...], b_ref[...], alpha)
    o_ref[...] = y
    ob_ref[...] = y.astype(BF16)
    idx_ref[...], wt_ref[...] = _top2(y, rw_ref[...])


def _residual_ln_router(x, mix, g, b, alpha, router_w, tb=256):
    t, dm = x.shape
    tb = min(tb, t)
    w = jnp.pad(router_w, ((0, 0), (0, LANES - router_w.shape[1])))
    tok = pl.BlockSpec((tb, dm), lambda i: (i, 0))
    vec = pl.BlockSpec((1, dm), lambda i: (0, 0))
    route = pl.BlockSpec((tb, LANES), lambda i: (i, 0))
    return pl.pallas_call(
        functools.partial(_ln_router_kernel, alpha=alpha),
        grid=(t // tb,),
        in_specs=[tok, tok, vec, vec, pl.BlockSpec((dm, LANES), lambda i: (0, 0))],
        out_specs=[tok, tok, route, route],
        out_shape=[jax.ShapeDtypeStruct((t, dm), F32), jax.ShapeDtypeStruct((t, dm), BF16),
                   jax.ShapeDtypeStruct((t, LANES), jnp.int32), jax.ShapeDtypeStruct((t, LANES), F32)],
        compiler_params=_params("parallel"),
        name="residual_ln_router",
    )(x, mix, g.reshape(1, dm), b.reshape(1, dm), w)


MOE_TM = 512


def _moe_dispatch(idx, wts, tm):
    t = idx.shape[0]
    n_rows = 2 * t + N_EXPERTS * tm
    n_tiles = n_rows // tm
    e_ids = idx[:, :2]
    per_tok = (e_ids[:, :, None] == jnp.arange(N_EXPERTS)).astype(jnp.int32).sum(axis=1)
    csum = jnp.cumsum(per_tok, axis=0)
    counts = csum[-1]
    rank = csum - per_tok
    padded = (counts + tm - 1) // tm * tm
    ends = jnp.cumsum(padded)
    offs = ends - padded
    pos = offs[e_ids] + jnp.take_along_axis(rank, e_ids, axis=1)
    code = jnp.full((n_rows,), -1, jnp.int32).at[pos.reshape(-1)].set(jnp.arange(2 * t, dtype=jnp.int32))
    src = jnp.maximum(code, 0) // 2
    row_w = jnp.where(code >= 0, wts[:, :2].reshape(-1)[jnp.maximum(code, 0)], 0.0)
    tile_start = jnp.arange(n_tiles, dtype=jnp.int32) * tm
    owner = jnp.sum(tile_start[:, None] >= ends[None, :], axis=1).astype(jnp.int32)
    tile_valid = (owner < N_EXPERTS).astype(jnp.int32)
    tile_expert = jnp.minimum(owner, N_EXPERTS - 1)
    return src, row_w, pos.astype(jnp.int32), tile_expert, tile_valid


def _gather_rows(idx_ref, n, src_hbm, dst_ref, sem):
    def issue(r, carry):
        pltpu.make_async_copy(src_hbm.at[pl.ds(idx_ref[0, r], 1), :], dst_ref.at[pl.ds(r, 1), :], sem).start()
        return carry

    lax.fori_loop(0, n, issue, 0)
    pltpu.make_async_copy(src_hbm.at[pl.ds(0, n), :], dst_ref.at[pl.ds(0, n), :], sem).wait()


def _moe_up_sparse_kernel(te_ref, tv_ref, src_ref, x_hbm, wg_ref, wu_ref, roww_ref, h_ref, xbuf, xb_buf, sem, *, tm):
    i = pl.program_id(0)
    valid = tv_ref[i] == 1

    @pl.when(jnp.logical_and(valid, pl.program_id(1) == 0))
    def _():
        _gather_rows(src_ref, tm, x_hbm, xbuf, sem)
        xb_buf[...] = xbuf[...].astype(BF16)

    @pl.when(valid)
    def _():
        x = xb_buf[...]
        a = jnp.dot(x, wg_ref[...], preferred_element_type=F32)
        u = jnp.dot(x, wu_ref[...], preferred_element_type=F32)
        h_ref[...] = (a * jax.nn.sigmoid(a) * u * roww_ref[...]).astype(h_ref.dtype)

    @pl.when(jnp.logical_not(valid))
    def _():
        h_ref[...] = jnp.zeros_like(h_ref)


def _moe_up_sparse(x, we_gate, we_up, src, row_w, tile_expert, tile_valid, tm, tn=512):
    t, dm = x.shape
    n_e, _, fe = we_gate.shape
    n_rows = src.shape[0]
    n_tiles = n_rows // tm
    wspec = pl.BlockSpec((None, dm, tn), lambda i, j, te, tv: (te[i], 0, j))
    grid_spec = pltpu.PrefetchScalarGridSpec(
        num_scalar_prefetch=2,
        grid=(n_tiles, fe // tn),
        in_specs=[pl.BlockSpec((None, 1, tm), lambda i, j, te, tv: (i, 0, 0), memory_space=pltpu.SMEM),
                  pl.BlockSpec(memory_space=pl.ANY),
                  wspec, wspec,
                  pl.BlockSpec((tm, 1), lambda i, j, te, tv: (i, 0))],
        out_specs=pl.BlockSpec((tm, tn), lambda i, j, te, tv: (i, j)),
        scratch_shapes=[pltpu.VMEM((tm, dm), F32), pltpu.VMEM((tm, dm), BF16), pltpu.SemaphoreType.DMA(())],
    )
    return pl.pallas_call(
        functools.partial(_moe_up_sparse_kernel, tm=tm),
        grid_spec=grid_spec,
        out_shape=jax.ShapeDtypeStruct((n_rows, fe), BF16),
        compiler_params=_params("arbitrary", "arbitrary"),
        name="moe_up_sparse",
    )(tile_expert, tile_valid, src.reshape(n_tiles, 1, tm), x, we_gate, we_up, row_w.reshape(n_rows, 1))


def _moe_down_kernel(te_ref, tv_ref, h_ref, w_ref, y_ref):
    valid = tv_ref[pl.program_id(1)] == 1

    @pl.when(valid)
    def _():
        y_ref[...] = jnp.dot(h_ref[...], w_ref[...], preferred_element_type=F32)

    @pl.when(jnp.logical_not(valid))
    def _():
        y_ref[...] = jnp.zeros_like(y_ref)


def _moe_down_sparse(h, we_down, tile_expert, tile_valid, tm, tn=1024):
    n_rows, fe = h.shape
    dm = we_down.shape[2]
    grid_spec = pltpu.PrefetchScalarGridSpec(
        num_scalar_prefetch=2,
        grid=(dm // tn, n_rows // tm),
        in_specs=[pl.BlockSpec((tm, fe), lambda j, i, te, tv: (i, 0)),
                  pl.BlockSpec((None, fe, tn), lambda j, i, te, tv: (te[i], 0, j))],
        out_specs=pl.BlockSpec((tm, tn), lambda j, i, te, tv: (i, j)),
    )
    return pl.pallas_call(
        _moe_down_kernel,
        grid_spec=grid_spec,
        out_shape=jax.ShapeDtypeStruct((n_rows, dm), F32),
        compiler_params=_params("parallel", "parallel"),
        name="moe_down_sparse",
    )(tile_expert, tile_valid, h, we_down)


def _moe_combine_ln_kernel(pos_ref, y_hbm, x_ref, g_ref, b_ref, o_ref, ob_ref, ybuf, sem, *, alpha, tb):
    _gather_rows(pos_ref, 2 * tb, y_hbm, ybuf, sem)
    h = alpha * x_ref[...] + (ybuf[0:tb, :] + ybuf[tb:2 * tb, :])
    mu = jnp.mean(h, axis=-1, keepdims=True)
    cen = h - mu
    var = jnp.mean(cen * cen, axis=-1, keepdims=True)
    y = cen * lax.rsqrt(var + LN_EPS) * g_ref[...] + b_ref[...]
    o_ref[...] = y
    ob_ref[...] = y.astype(BF16)


def _moe_combine_ln(x, ys, pos, g, b, alpha, tb=256):
    t, dm = x.shape
    tb = min(tb, t)
    tok = pl.BlockSpec((tb, dm), lambda i: (i, 0))
    vec = pl.BlockSpec((1, dm), lambda i: (0, 0))
    return pl.pallas_call(
        functools.partial(_moe_combine_ln_kernel, alpha=alpha, tb=tb),
        grid=(t // tb,),
        in_specs=[pl.BlockSpec((None, 1, 2 * tb), lambda i: (i, 0, 0), memory_space=pltpu.SMEM),
                  pl.BlockSpec(memory_space=pl.ANY), tok, vec, vec],
        out_specs=[tok, tok],
        out_shape=[jax.ShapeDtypeStruct((t, dm), F32), jax.ShapeDtypeStruct((t, dm), BF16)],
        scratch_shapes=[pltpu.VMEM((2 * tb, dm), F32), pltpu.SemaphoreType.DMA(())],
        compiler_params=_params("arbitrary"),
        name="moe_combine_ln",
    )(jnp.concatenate([pos[:, 0].reshape(t // tb, 1, tb), pos[:, 1].reshape(t // tb, 1, tb)], axis=2),
      ys, x, g.reshape(1, dm), b.reshape(1, dm))


def _ple_kernel(xb_ref, wg_ref, p_ref, wp_ref, x_ref, o_ref, *maybe_ob_ref):
    gate = jax.nn.sigmoid(jnp.dot(xb_ref[...], wg_ref[...], preferred_element_type=F32))
    emb = jnp.dot(p_ref[...], wp_ref[...], preferred_element_type=F32)
    y = x_ref[...] + gate * emb
    o_ref[...] = y
    for ob_ref in maybe_ob_ref:
        ob_ref[...] = y.astype(BF16)


def _ple(xb, x, w_pgate, p, w_pproj, rows=None, tm=1024, tn=512):
    t, dm = xb.shape
    pd = p.shape[1]
    start, count = (0, t) if rows is None else rows
    tm = min(tm, count)
    assert start % tm == 0 and count % tm == 0
    off = start // tm
    out = pl.BlockSpec((tm, tn), lambda i, j: (i, j))
    outs = [jax.ShapeDtypeStruct((count, dm), F32)] + ([jax.ShapeDtypeStruct((count, dm), BF16)] if rows is None else [])
    return pl.pallas_call(
        _ple_kernel,
        grid=(count // tm, dm // tn),
        in_specs=[pl.BlockSpec((tm, dm), lambda i, j: (i + off, 0)),
                  pl.BlockSpec((dm, tn), lambda i, j: (0, j)),
                  pl.BlockSpec((tm, pd), lambda i, j: (i + off, 0)),
                  pl.BlockSpec((pd, tn), lambda i, j: (0, j)),
                  pl.BlockSpec((tm, tn), lambda i, j: (i + off, j))],
        out_specs=[out] * len(outs),
        out_shape=outs,
        compiler_params=_params("parallel", "parallel"),
        name="ple_gate",
    )(xb, w_pgate, p, w_pproj, x)


def _pad_cols(w, width):
    return jnp.pad(w, ((0, 0), (0, width - w.shape[1])))


def _pad_rows(w, rows):
    pad = [(0, 0)] * w.ndim
    pad[-2] = (0, rows - w.shape[-2])
    return jnp.pad(w, pad)


def _split_rwkv_cols(w, dg, dr, has_mv):
    base = 2 * dg
    rkv = w[:, base:base + 3 * dr]
    o = base + 3 * dr
    sizes = [DECAY_LORA, DECAY_LORA, AAA_LORA, AAA_LORA, GATE_LORA] + ([MV_LORA] if has_mv else [])
    slots = [LORA_SLOT, LORA_SLOT, LORA_SLOT, LORA_SLOT, GATE_LORA, LORA_SLOT]
    parts = []
    for n, slot in zip(sizes, slots):
        parts.append(_pad_cols(w[:, o:o + n], slot))
        o += n
    lora = _pad_cols(jnp.concatenate(parts, axis=1), LORA_COLS)
    return jnp.concatenate([rkv, lora], axis=1)


def _trunk(x, p, P, segs):
    (t1, _), (t2, _) = segs
    t = t1 + t2
    depth = P['w_out'].shape[0]
    alpha = (2 * depth) ** 0.25
    dg = P['sgu_ln_g'].shape[1]
    dr = P['k_k'].shape[1]
    hb = lax.broadcasted_iota(jnp.int32, (SCAN_W, SCAN_W), 0) // RWKV_HEAD
    hc = lax.broadcasted_iota(jnp.int32, (SCAN_W, SCAN_W), 1) // RWKV_HEAD
    ones_bd = (hb == hc).astype(BF16)
    xb = jnp.concatenate([part.astype(BF16) for part in x], axis=0)
    w_out_b = _to_bf16(P['w_out'])
    w_pgate_b = _to_bf16(P['w_pgate'])
    v_first = None
    for l in range(depth):
        if l == 0:
            w_in_l, conv_l, v0_l, v2_l = P['w_in0'], P['conv0'], None, None
        else:
            w_in_l, conv_l, v0_l, v2_l = P['w_in'][l - 1], P['conv'][l - 1], P['v0'][l - 1], P['v2'][l - 1]
        w_in_b = _to_bf16(w_in_l)
        uv = _matmul(xb, w_in_b[:, :2 * dg], BF16)
        zr = _matmul(xb, _split_rwkv_cols(w_in_b, dg, dr, l > 0), BF16)
        y_g = _spatial_gating(uv, P['sgu_ln_g'][l], P['sgu_ln_b'][l], P['w_s'][l], P['b_s'][l])
        conv_r = _split_rwkv_cols(jnp.pad(conv_l, ((0, 0), (2 * dg, 0))), dg, dr, l > 0)
        prep_args = (zr, conv_r, P['w0'][l], _pad_rows(P['w2'][l], LORA_SLOT).astype(BF16), P['a0'][l],
                     _pad_rows(P['a2'][l], LORA_SLOT).astype(BF16), P['g2'][l].astype(BF16),
                     P['k_k'][l], P['k_a'][l], P['r_k'][l].reshape(-1), ones_bd, segs)
        if l == 0:
            r, v, kk, g, bonus, lw, k, b = _rwkv_prep(*prep_args)
            v_first = v
        else:
            r, v, kk, g, bonus, lw, k, b = _rwkv_prep(*prep_args, v_first=v_first, v0=v0_l,
                                                      v2=_pad_rows(v2_l, LORA_SLOT).astype(BF16))
        y_f, y_b = _rwkv_scan(r, v, kk, lw, k, b, segs)
        y_r = _rwkv_post(y_f, y_b, g, bonus, P['gn_g'][l], P['gn_b'][l], ones_bd)
        w_out = w_out_b[l]
        mix = _matmul_pair(y_g, w_out[:dg], y_r, w_out[dg:], BF16)
        j = l // 2
        if l % 2 == 0:
            x, xb = _residual_ln(x, mix, P['ln1_g'][l], P['ln1_b'][l], alpha)
            h = _ffn_up(xb, _to_bf16(P['w_ff_gate'][j]), _to_bf16(P['w_ff_up'][j]))
            ff = _matmul(h, _to_bf16(P['w_ff_down'][j]), BF16, tk=2048)
        else:
            if isinstance(x, tuple):
                x = jnp.concatenate(x, axis=0)
            x, xb, idx, wts = _residual_ln_router(x, mix, P['ln1_g'][l], P['ln1_b'][l], alpha, P['router'][j])
            tm = min(MOE_TM, t)
            src, row_w, pos, tile_expert, tile_valid = _moe_dispatch(idx, wts, tm)
            h = _moe_up_sparse(x, _to_bf16(P['we_gate'][j]), _to_bf16(P['we_up'][j]), src, row_w,
                               tile_expert, tile_valid, tm)
            ys = _moe_down_sparse(h, _to_bf16(P['we_down'][j]), tile_expert, tile_valid, tm)
            x, xb = _moe_combine_ln(x, ys, pos, P['ln2_g'][l], P['ln2_b'][l], alpha)
        if l % 2 == 0:
            x, xb = _residual_ln(x, ff, P['ln2_g'][l], P['ln2_b'][l], alpha)
        ple_args = (xb, x, w_pgate_b[l], p[l].astype(BF16), P['w_pproj'][l].astype(BF16))
        if l < depth - 1:
            x, xb = _ple(*ple_args)
    (y1,), (y2,) = _ple(*ple_args, rows=(0, t1)), _ple(*ple_args, rows=(t1, t2))
    return y1, y2


def kernel(x_prompt, x_sample, p_prompt, p_sample, w_in0, conv0, w_in, conv, sgu_ln_g, sgu_ln_b, w_s, b_s, w0, w2, a0, a2, g2, k_k, k_a, r_k, gn_g, gn_b, v0, v2, w_out, ln1_g, ln1_b, ln2_g, ln2_b, w_ff_gate, w_ff_up, w_ff_down, router, we_gate, we_up, we_down, w_pproj, w_pgate):
    P = dict(w_in0=w_in0, conv0=conv0, w_in=w_in, conv=conv, sgu_ln_g=sgu_ln_g, sgu_ln_b=sgu_ln_b,
             w_s=w_s, b_s=b_s, w0=w0, w2=w2, a0=a0, a2=a2, g2=g2, k_k=k_k, k_a=k_a, r_k=r_k,
             gn_g=gn_g, gn_b=gn_b, v0=v0, v2=v2, w_out=w_out, ln1_g=ln1_g, ln1_b=ln1_b,
             ln2_g=ln2_g, ln2_b=ln2_b, w_ff_gate=w_ff_gate, w_ff_up=w_ff_up, w_ff_down=w_ff_down,
             router=router, we_gate=we_gate, we_up=we_up, we_down=we_down,
             w_pproj=w_pproj, w_pgate=w_pgate)
    b1, s1, dm = x_prompt.shape
    b2, s2, _ = x_sample.shape
    depth, pd = p_prompt.shape[0], p_prompt.shape[-1]
    t1, t2 = b1 * s1, b2 * s2
    segs = ((t1, s1), (t2, s2))
    p = jnp.concatenate([p_prompt.reshape(depth, t1, pd), p_sample.reshape(depth, t2, pd)], axis=1)
    y1, y2 = _trunk((x_prompt.reshape(t1, dm), x_sample.reshape(t2, dm)), p, P, segs)
    return y1.reshape(b1, s1, dm), y2.reshape(b2, s2, dm)
```

```python
import functools

import jax
import jax.numpy as jnp
from jax import lax
from jax.experimental import pallas as pl
from jax.experimental.pallas import tpu as pltpu

F32 = jnp.float32
BF16 = jnp.bfloat16
HIGHEST = lax.Precision.HIGHEST

LANES = 128
BF16_SUBLANES = 16
VMEM_LIMIT_BYTES = 56 * 1024 * 1024

CHUNK = 128
GMLP_HEAD = 128
RWKV_HEAD = 64
DECAY_LORA = 96
AAA_LORA = 96
MV_LORA = 64
GATE_LORA = 256
N_EXPERTS = 8
LN_EPS = 1e-5
GN_EPS = 64e-5
L2_EPS = 1e-12
DECAY_SCALE = 0.6065306597126334

SCAN_L = 64
SCAN_W = 256
HEADS_PER_GROUP = SCAN_W // RWKV_HEAD
LORA_SLOT = 128
LORA_COLS = 1024


def _params(*sem):
    return pltpu.CompilerParams(dimension_semantics=sem, vmem_limit_bytes=VMEM_LIMIT_BYTES)


def _is_seq_start(pos, segs):
    (t1, s1), (_, s2) = segs
    return jnp.where(pos < t1, pos % s1 == 0, (pos - t1) % s2 == 0)


CAST_BLOCK_BYTES = 8 * 1024 * 1024


def _cast_kernel(x_ref, o_ref):
    o_ref[...] = x_ref[...].astype(o_ref.dtype)


def _to_bf16(w):
    w2 = w.reshape(-1, w.shape[-1])
    rows, n = w2.shape
    tb = rows
    while tb * n * w2.dtype.itemsize > CAST_BLOCK_BYTES and tb % 2 == 0 and (tb // 2) % BF16_SUBLANES == 0:
        tb //= 2
    out = pl.pallas_call(
        _cast_kernel,
        grid=(rows // tb,),
        in_specs=[pl.BlockSpec((tb, n), lambda i: (i, 0))],
        out_specs=pl.BlockSpec((tb, n), lambda i: (i, 0)),
        out_shape=jax.ShapeDtypeStruct((rows, n), BF16),
        compiler_params=_params("parallel"),
        name="cast_bf16",
    )(w2)
    return out.reshape(w.shape)


def _mm_kernel(x_ref, w_ref, o_ref, *scratch, nk):
    if nk == 1:
        o_ref[...] = jnp.dot(x_ref[...], w_ref[...], preferred_element_type=F32).astype(o_ref.dtype)
        return
    acc_ref, = scratch
    k = pl.program_id(2)

    @pl.when(k == 0)
    def _():
        acc_ref[...] = jnp.zeros_like(acc_ref)

    acc_ref[...] += jnp.dot(x_ref[...], w_ref[...], preferred_element_type=F32)

    @pl.when(k == nk - 1)
    def _():
        o_ref[...] = acc_ref[...].astype(o_ref.dtype)


def _matmul(x, w, out_dtype, tm=1024, tn=1024, tk=None):
    m, kdim = x.shape
    n = w.shape[1]
    tm, tn = min(tm, m), min(tn, n)
    tk = kdim if tk is None else min(tk, kdim)
    nk = kdim // tk
    assert m % tm == 0 and n % tn == 0 and kdim % tk == 0
    scratch = [] if nk == 1 else [pltpu.VMEM((tm, tn), F32)]
    return pl.pallas_call(
        functools.partial(_mm_kernel, nk=nk),
        grid=(m // tm, n // tn, nk),
        in_specs=[pl.BlockSpec((tm, tk), lambda i, j, k: (i, k)),
                  pl.BlockSpec((tk, tn), lambda i, j, k: (k, j))],
        out_specs=pl.BlockSpec((tm, tn), lambda i, j, k: (i, j)),
        out_shape=jax.ShapeDtypeStruct((m, n), out_dtype),
        scratch_shapes=scratch,
        compiler_params=_params("parallel", "parallel", "arbitrary"),
        name="matmul",
    )(x, w)


def _mm_pair_kernel(xa_ref, wa_ref, xb_ref, wb_ref, o_ref):
    acc = jnp.dot(xa_ref[...], wa_ref[...], preferred_element_type=F32)
    acc += jnp.dot(xb_ref[...], wb_ref[...], preferred_element_type=F32)
    o_ref[...] = acc.astype(o_ref.dtype)


def _matmul_pair(xa, wa, xb, wb, out_dtype, tm=1024, tn=1024):
    m, ka = xa.shape
    kb = xb.shape[1]
    n = wa.shape[1]
    tm, tn = min(tm, m), min(tn, n)
    return pl.pallas_call(
        _mm_pair_kernel,
        grid=(m // tm, n // tn),
        in_specs=[pl.BlockSpec((tm, ka), lambda i, j: (i, 0)),
                  pl.BlockSpec((ka, tn), lambda i, j: (0, j)),
                  pl.BlockSpec((tm, kb), lambda i, j: (i, 0)),
                  pl.BlockSpec((kb, tn), lambda i, j: (0, j))],
        out_specs=pl.BlockSpec((tm, tn), lambda i, j: (i, j)),
        out_shape=jax.ShapeDtypeStruct((m, n), out_dtype),
        compiler_params=_params("parallel", "parallel"),
        name="matmul_pair",
    )(xa, wa, xb, wb)


def _sgu_kernel(u_ref, v_ref, g_ref, b_ref, ws_ref, bs_ref, o_ref, *, n_chunks):
    gv = jax.nn.gelu(v_ref[...].astype(F32))
    mu = jnp.mean(gv, axis=-1, keepdims=True)
    cen = gv - mu
    var = jnp.mean(cen * cen, axis=-1, keepdims=True)
    vn = (cen * lax.rsqrt(var + LN_EPS) * g_ref[...] + b_ref[...]).astype(BF16)
    n_heads = ws_ref.shape[0]
    for c in range(n_chunks):
        rows = slice(c * CHUNK, (c + 1) * CHUNK)
        for h in range(n_heads):
            cols = slice(h * GMLP_HEAD, (h + 1) * GMLP_HEAD)
            sv = jnp.dot(ws_ref[h], vn[rows, cols], preferred_element_type=F32) + bs_ref[:, cols]
            gu = jax.nn.gelu(u_ref[rows, cols].astype(F32))
            o_ref[rows, cols] = (gu * sv).astype(o_ref.dtype)


def _spatial_gating(uv, ln_g, ln_b, w_s, b_s, tb=512):
    t = uv.shape[0]
    dg = uv.shape[1] // 2
    tb = min(tb, t)
    n_heads = w_s.shape[0]
    bs_full = jnp.repeat(b_s.T, GMLP_HEAD, axis=1)
    return pl.pallas_call(
        functools.partial(_sgu_kernel, n_chunks=tb // CHUNK),
        grid=(t // tb,),
        in_specs=[pl.BlockSpec((tb, dg), lambda i: (i, 0)),
                  pl.BlockSpec((tb, dg), lambda i: (i, 1)),
                  pl.BlockSpec((1, dg), lambda i: (0, 0)),
                  pl.BlockSpec((1, dg), lambda i: (0, 0)),
                  pl.BlockSpec((n_heads, CHUNK, CHUNK), lambda i: (0, 0, 0)),
                  pl.BlockSpec((CHUNK, dg), lambda i: (0, 0))],
        out_specs=pl.BlockSpec((tb, dg), lambda i: (i, 0)),
        out_shape=jax.ShapeDtypeStruct((t, dg), BF16),
        compiler_params=_params("parallel"),
        name="spatial_gating",
    )(uv, uv, ln_g.reshape(1, dg), ln_b.reshape(1, dg), w_s.astype(BF16), bs_full)


def _head_sum(x, ones_bd):
    parts = []
    for g in range(x.shape[1] // SCAN_W):
        xs = x[:, g * SCAN_W:(g + 1) * SCAN_W]
        hi = xs.astype(BF16)
        lo = (xs - hi.astype(F32)).astype(BF16)
        parts.append(jnp.dot(hi, ones_bd, preferred_element_type=F32)
                     + jnp.dot(lo, ones_bd, preferred_element_type=F32))
    return jnp.concatenate(parts, axis=1)


def _prep_kernel(*refs, layer1, tb, segs, dr):
    if layer1:
        (z_ref, zp_ref, zn_ref, conv_ref, w0_ref, w2_ref, a0_ref, a2_ref, g2_ref, kk_ref, ka_ref, rk_ref,
         ones_ref, vf_ref, v0_ref, v2_ref, r_o, v_o, kkn_o, g_o, bonus_o, lw_o, k_o, b_o) = refs
    else:
        (z_ref, zp_ref, zn_ref, conv_ref, w0_ref, w2_ref, a0_ref, a2_ref, g2_ref, kk_ref, ka_ref, rk_ref,
         ones_ref, r_o, v_o, kkn_o, g_o, bonus_o, lw_o, k_o, b_o) = refs
    pos0 = pl.program_id(0) * tb
    first = _is_seq_start(pos0, segs)
    last = _is_seq_start(pos0 + tb, segs)
    row = lax.broadcasted_iota(jnp.int32, (tb, 1), 0)

    def conv(c0, width):
        cols = slice(c0, c0 + width)
        z = z_ref[:, cols].astype(F32)
        prev_row = jnp.where(first, 0.0, zp_ref[:, cols].astype(F32)[BF16_SUBLANES - 1:BF16_SUBLANES])
        next_row = jnp.where(last, 0.0, zn_ref[:, cols].astype(F32)[0:1])
        zprev = jnp.where(row == 0, prev_row, pltpu.roll(z, 1, 0))
        znext = jnp.where(row == tb - 1, next_row, pltpu.roll(z, tb - 1, 0))
        return zprev * conv_ref[0:1, cols] + z * conv_ref[1:2, cols] + znext * conv_ref[2:3, cols]

    lo = conv(3 * dr, LORA_COLS)
    s = LORA_SLOT
    wd = (lo[:, 0:s], lo[:, s:2 * s])
    ad = (lo[:, 2 * s:3 * s], lo[:, 3 * s:4 * s])
    gd = lo[:, 4 * s:6 * s]

    def lora(x, w):
        return jnp.dot(x.astype(BF16), w, preferred_element_type=F32)

    g_o[...] = lora(jax.nn.sigmoid(gd), g2_ref[...]).astype(g_o.dtype)
    v = conv(2 * dr, dr)
    if layer1:
        mv = lo[:, 6 * s:7 * s]
        v = v + (vf_ref[...].astype(F32) - v) * jax.nn.sigmoid(v0_ref[...] + lora(mv, v2_ref[...]))
    v_o[...] = v.astype(v_o.dtype)
    r = conv(0, dr)
    r_o[...] = r.astype(r_o.dtype)
    k = conv(dr, dr)
    ones_bd = ones_ref[...]
    kk = k * kk_ref[...]
    kk = kk * lax.rsqrt(_head_sum(kk * kk, ones_bd) + L2_EPS)
    kkn_o[...] = kk.astype(kkn_o.dtype)
    for d in range(2):
        zw = w0_ref[d:d + 1, :] + lora(jnp.tanh(wd[d]), w2_ref[d])
        lw_o[d] = -DECAY_SCALE / (1.0 + jnp.exp(-zw))
        a = jax.nn.sigmoid(a0_ref[d:d + 1, :] + lora(ad[d], a2_ref[d]))
        k_d = k * (1.0 + (a - 1.0) * ka_ref[...])
        k_o[d] = k_d.astype(k_o.dtype)
        b_o[d] = (kk * a).astype(b_o.dtype)
        if d == 0:
            bonus_o[...] = (_head_sum(r * k_d * rk_ref[...], ones_bd) * v).astype(bonus_o.dtype)


def _rwkv_prep(zr, conv_w, w0, w2, a0, a2, g2, k_k, k_a, r_k, ones_bd, segs, v_first=None, v0=None, v2=None,
               tb=128):
    t, c = zr.shape
    dr = k_k.shape[-1]
    tb = min(tb, t)
    layer1 = v_first is not None
    hb = tb // BF16_SUBLANES
    n_halo = t // BF16_SUBLANES
    const2 = lambda i: (0, 0)
    const3 = lambda i: (0, 0, 0)
    in_specs = [pl.BlockSpec((tb, c), lambda i: (i, 0)),
                pl.BlockSpec((BF16_SUBLANES, c), lambda i: (jnp.maximum(i * hb - 1, 0), 0)),
                pl.BlockSpec((BF16_SUBLANES, c), lambda i: (jnp.minimum((i + 1) * hb, n_halo - 1), 0)),
                pl.BlockSpec((3, c), const2),
                pl.BlockSpec((2, dr), const2),
                pl.BlockSpec((2, LORA_SLOT, dr), const3),
                pl.BlockSpec((2, dr), const2),
                pl.BlockSpec((2, LORA_SLOT, dr), const3),
                pl.BlockSpec((GATE_LORA, dr), const2),
                pl.BlockSpec((1, dr), const2),
                pl.BlockSpec((1, dr), const2),
                pl.BlockSpec((1, dr), const2),
                pl.BlockSpec((SCAN_W, SCAN_W), const2)]
    args = [zr, zr, zr, conv_w, w0, w2, a0, a2, g2, k_k.reshape(1, dr), k_a.reshape(1, dr), r_k.reshape(1, dr),
            ones_bd]
    if layer1:
        in_specs += [pl.BlockSpec((tb, dr), lambda i: (i, 0)),
                     pl.BlockSpec((1, dr), const2),
                     pl.BlockSpec((LORA_SLOT, dr), const2)]
        args += [v_first, v0.reshape(1, dr), v2]
    tok = pl.BlockSpec((tb, dr), lambda i: (i, 0))
    tok2 = pl.BlockSpec((2, tb, dr), lambda i: (0, i, 0))
    one = jax.ShapeDtypeStruct((t, dr), BF16)
    two = jax.ShapeDtypeStruct((2, t, dr), BF16)
    log_decay = jax.ShapeDtypeStruct((2, t, dr), F32)
    return pl.pallas_call(
        functools.partial(_prep_kernel, layer1=layer1, tb=tb, segs=segs, dr=dr),
        grid=(t // tb,),
        in_specs=in_specs,
        out_specs=[tok, tok, tok, tok, tok, tok2, tok2, tok2],
        out_shape=[one, one, one, one, one, log_decay, two, two],
        compiler_params=_params("parallel"),
        name="rwkv_prep",
    )(*args)


def _scan_consts(reverse):
    L, W, G = SCAN_L, SCAN_W, HEADS_PER_GROUP
    sgn = -1 if reverse else 1
    t_i = lax.broadcasted_iota(jnp.int32, (L, 3 * L), 0)
    s_i = lax.broadcasted_iota(jnp.int32, (L, 3 * L), 1) % L
    tri3 = jnp.where(sgn * (t_i - s_i) >= 0, 1.0, 0.0).astype(BF16)
    tc = lax.broadcasted_iota(jnp.int32, (L, G * L), 0)
    sc = lax.broadcasted_iota(jnp.int32, (L, G * L), 1) % L
    before = sgn * (tc - sc)
    strict = before > 0
    rmask = before > 0 if reverse else before >= 0
    eye = jnp.where(sc == tc, 1.0, 0.0).astype(F32)
    bi = lax.broadcasted_iota(jnp.int32, (W, W), 0) // RWKV_HEAD
    bj = lax.broadcasted_iota(jnp.int32, (W, W), 1) // RWKV_HEAD
    return tri3, strict, rmask, eye, bi == bj


def _bd(x):
    per_tile = LANES // RWKV_HEAD
    n_tiles = x.shape[1] // LANES
    lane_head = lax.broadcasted_iota(jnp.int32, (x.shape[0], LANES), 1) // RWKV_HEAD
    zero = jnp.zeros((x.shape[0], LANES), x.dtype)
    blocks = []
    for h in range(HEADS_PER_GROUP):
        own = h // per_tile
        kept = jnp.where(lane_head == h % per_tile, x[:, own * LANES:(own + 1) * LANES], zero)
        blocks.append(jnp.concatenate([kept if tile == own else zero for tile in range(n_tiles)], axis=1))
    return jnp.concatenate(blocks, axis=0)


_NT = (((1,), (1,)), ((), ()))
_TN = (((0,), (0,)), ((), ()))


def _chunk_operators(r, v, kk, lw, k, b, consts, reverse):
    L, W, G = SCAN_L, SCAN_W, HEADS_PER_GROUP
    tri3, strict, rmask, eye, _ = consts
    bd = _bd

    def dot(a_, b_):
        return jnp.dot(a_, b_, preferred_element_type=F32)

    hi = lw.astype(BF16)
    rem = lw - hi.astype(F32)
    mid = rem.astype(BF16)
    low = (rem - mid.astype(F32)).astype(BF16)
    cum = dot(tri3, jnp.concatenate([hi, mid, low], axis=0))
    yield
    tot = jnp.sum(lw, axis=0, keepdims=True)
    e_ex = jnp.exp(cum - lw)
    e_neg = jnp.exp(-cum)
    e_rem = jnp.exp(tot - cum)
    kq = kk * e_ex
    rq = r * (e_ex if reverse else jnp.exp(cum))
    vb = v.astype(BF16)
    q = jnp.concatenate([kq, rq], axis=0).astype(BF16)
    keys = jnp.concatenate([bd((k * e_neg).astype(BF16)), bd((b * e_neg).astype(BF16))], axis=0)
    a = lax.dot_general(q, keys, _NT, preferred_element_type=F32)
    yield
    a_kk = jnp.where(strict, a[:L, :G * L], 0.0)
    a_kb = jnp.where(strict, a[:L, G * L:], 0.0)
    a_rk = jnp.where(rmask, a[L:, :G * L], 0.0)
    a_rb = jnp.where(rmask, a[L:, G * L:], 0.0).astype(BF16)
    p = -a_kb
    tinv = eye + p
    av = dot(jnp.concatenate([a_kk, a_rk], axis=0).astype(BF16), bd(vb))
    p = dot(p.astype(BF16), bd(p.astype(BF16)))
    yield
    levels = L.bit_length() - 2
    for lvl in range(levels):
        wt = bd(p.astype(BF16))
        if lvl < levels - 1:
            tp = dot(jnp.concatenate([tinv, p], axis=0).astype(BF16), wt)
            tinv = tinv + tp[:L]
            p = tp[L:]
        else:
            tinv = tinv + dot(tinv.astype(BF16), wt)
        yield
    tinv = tinv.astype(BF16)
    tk = dot(tinv, jnp.concatenate([bd(kq.astype(BF16)), bd(av[:L].astype(BF16))], axis=1))
    yield
    kq2 = tk[:, :W]
    cu = tk[:, W:]
    ar = dot(a_rb, jnp.concatenate([bd(kq2.astype(BF16)), bd(cu.astype(BF16))], axis=1))
    yield
    rq2 = rq - ar[:, :W]
    yc = av[L:] - ar[:, W:]
    qs = jnp.concatenate([rq2, kq2], axis=0).astype(BF16)
    kb = jnp.concatenate([k * e_rem, b * e_rem], axis=0).astype(BF16)
    return qs, yc, cu, jnp.exp(tot), kb, vb


def _run_in_lockstep(gens):
    results = [None] * len(gens)
    active = list(enumerate(gens))
    while active:
        still = []
        for i, gen in active:
            try:
                next(gen)
                still.append((i, gen))
            except StopIteration as done:
                results[i] = done.value
        active = still
    return results


def _scan_kernel(rf_ref, vf_ref, kkf_ref, lwf_ref, kf_ref, bf_ref, rb_ref, vb_ref, kkb_ref, lwb_ref, kb_ref,
                 bb_ref, yf_ref, yb_ref, s_ref, *, tb, segs, nblk, ng):
    L, W = SCAN_L, SCAN_W
    c = pl.program_id(1)
    start_f = _is_seq_start(c * tb, segs)
    start_b = _is_seq_start((nblk - c) * tb, segs)

    @pl.when(start_f)
    def _():
        s_ref[0] = jnp.zeros(s_ref.shape[1:], F32)

    @pl.when(start_b)
    def _():
        s_ref[1] = jnp.zeros(s_ref.shape[1:], F32)

    n_chunks = tb // L
    dirs = ((rf_ref, vf_ref, kkf_ref, lwf_ref, kf_ref, bf_ref, yf_ref),
            (rb_ref, vb_ref, kkb_ref, lwb_ref, kb_ref, bb_ref, yb_ref))
    streams = [(d, g) for d in range(2) for g in range(ng)]
    order = {d: [n_chunks - 1 - j if d == 1 else j for j in range(n_chunks)] for d in range(2)}
    consts = {d: _scan_consts(reverse=d == 1) for d in range(2)}
    gens = []
    for d, g in streams:
        r_ref, v_ref, kk_ref, lw_ref, k_ref, b_ref, _ = dirs[d]
        cols = slice(g * W, (g + 1) * W)
        for ci in order[d]:
            rows = slice(ci * L, (ci + 1) * L)
            gens.append(_chunk_operators(*(ref[rows, cols].astype(F32)
                                           for ref in (r_ref, v_ref, kk_ref, lw_ref, k_ref, b_ref)),
                                         consts[d], reverse=d == 1))
    ops = _run_in_lockstep(gens)
    states = [s_ref[d, g] for d, g in streams]
    for j in range(n_chunks):
        for si, (d, g) in enumerate(streams):
            qs, yc, cu, decay, kb, vb = ops[si * n_chunks + j]
            ci = order[d][j]
            yu = lax.dot_general(qs, states[si].astype(BF16), _NT, preferred_element_type=F32)
            y_ref = dirs[d][6]
            y_ref[ci * L:(ci + 1) * L, g * W:(g + 1) * W] = (yu[:L] + yc).astype(y_ref.dtype)
            vu = jnp.concatenate([vb, (-(yu[L:] + cu)).astype(BF16)], axis=0)
            upd = lax.dot_general(vu, kb, _TN, preferred_element_type=F32)
            states[si] = states[si] * decay + jnp.where(consts[d][4], upd, 0.0)
    for si, (d, g) in enumerate(streams):
        s_ref[d, g] = states[si]


def _rwkv_scan(r, v, kk, lw, k, b, segs, tb=256, ng=2):
    t, dr = r.shape
    tb = min(tb, t)
    nblk = t // tb
    w = ng * SCAN_W
    fwd = pl.BlockSpec((tb, w), lambda g, c: (c, g))
    bwd = pl.BlockSpec((tb, w), lambda g, c: (nblk - 1 - c, g))
    fwd2 = pl.BlockSpec((None, tb, w), lambda g, c: (0, c, g))
    bwd2 = pl.BlockSpec((None, tb, w), lambda g, c: (1, nblk - 1 - c, g))
    out = jax.ShapeDtypeStruct((t, dr), BF16)
    return pl.pallas_call(
        functools.partial(_scan_kernel, tb=tb, segs=segs, nblk=nblk, ng=ng),
        grid=(dr // w, nblk),
        in_specs=[fwd, fwd, fwd, fwd2, fwd2, fwd2, bwd, bwd, bwd, bwd2, bwd2, bwd2],
        out_specs=[fwd, bwd],
        out_shape=[out, out],
        scratch_shapes=[pltpu.VMEM((2, ng, SCAN_W, SCAN_W), F32)],
        compiler_params=_params("parallel", "arbitrary"),
        name="rwkv_scan",
    )(r, v, kk, lw, k, b, r, v, kk, lw, k, b)


def _post_kernel(yf_ref, yb_ref, g_ref, bonus_ref, gng_ref, gnb_ref, ones_ref, o_ref):
    y = yf_ref[...].astype(F32) + yb_ref[...].astype(F32)
    ones_bd = ones_ref[...]
    mu = _head_sum(y, ones_bd) * (1.0 / RWKV_HEAD)
    cen = y - mu
    var = _head_sum(cen * cen, ones_bd) * (1.0 / RWKV_HEAD)
    yn = cen * lax.rsqrt(var + GN_EPS) * gng_ref[...] + gnb_ref[...]
    o_ref[...] = ((yn + bonus_ref[...].astype(F32)) * g_ref[...].astype(F32)).astype(o_ref.dtype)


def _rwkv_post(y_f, y_b, g, bonus, gn_g, gn_b, ones_bd, tb=256):
    t, dr = y_f.shape
    tb = min(tb, t)
    tok = pl.BlockSpec((tb, dr), lambda i: (i, 0))
    vec = pl.BlockSpec((1, dr), lambda i: (0, 0))
    return pl.pallas_call(
        _post_kernel,
        grid=(t // tb,),
        in_specs=[tok, tok, tok, tok, vec, vec, pl.BlockSpec((SCAN_W, SCAN_W), lambda i: (0, 0))],
        out_specs=tok,
        out_shape=jax.ShapeDtypeStruct((t, dr), BF16),
        compiler_params=_params("parallel"),
        name="rwkv_post",
    )(y_f, y_b, g, bonus, gn_g.reshape(1, dr), gn_b.reshape(1, dr), ones_bd)


def _residual_layernorm(x, m, g, b, alpha):
    h = alpha * x + m.astype(F32)
    mu = jnp.mean(h, axis=-1, keepdims=True)
    cen = h - mu
    var = jnp.mean(cen * cen, axis=-1, keepdims=True)
    return cen * lax.rsqrt(var + LN_EPS) * g + b


def _ln_kernel(*refs, alpha, n_first):
    *x_refs, m_ref, g_ref, b_ref, o_ref, ob_ref = refs
    if n_first is None:
        x = x_refs[0][...]
    else:
        x = jnp.where(pl.program_id(0) < n_first, x_refs[0][...], x_refs[1][...])
    y = _residual_layernorm(x, m_ref[...], g_ref[...], b_ref[...], alpha)
    o_ref[...] = y
    ob_ref[...] = y.astype(BF16)


def _residual_ln(x, mix, g, b, alpha, tb=256):
    t, dm = mix.shape
    tb = min(tb, t)
    tok = pl.BlockSpec((tb, dm), lambda i: (i, 0))
    vec = pl.BlockSpec((1, dm), lambda i: (0, 0))
    if isinstance(x, tuple):
        xa, xb = x
        n_first = xa.shape[0] // tb
        assert xa.shape[0] % tb == 0 and xb.shape[0] % tb == 0
        n_last = xb.shape[0] // tb - 1
        x_args = [xa, xb]
        x_specs = [pl.BlockSpec((tb, dm), lambda i: (jnp.minimum(i, n_first - 1), 0)),
                   pl.BlockSpec((tb, dm), lambda i: (jnp.clip(i - n_first, 0, n_last), 0))]
    else:
        n_first, x_args, x_specs = None, [x], [tok]
    return pl.pallas_call(
        functools.partial(_ln_kernel, alpha=alpha, n_first=n_first),
        grid=(t // tb,),
        in_specs=x_specs + [tok, vec, vec],
        out_specs=[tok, tok],
        out_shape=[jax.ShapeDtypeStruct((t, dm), F32), jax.ShapeDtypeStruct((t, dm), BF16)],
        compiler_params=_params("parallel"),
        name="residual_ln",
    )(*x_args, mix, g.reshape(1, dm), b.reshape(1, dm))


def _ffn_up_kernel(x_ref, wg_ref, wu_ref, h_ref):
    x = x_ref[...]
    a = jnp.dot(x, wg_ref[...], preferred_element_type=F32)
    u = jnp.dot(x, wu_ref[...], preferred_element_type=F32)
    h_ref[...] = (a * jax.nn.sigmoid(a) * u).astype(h_ref.dtype)


def _ffn_up(xb, wg, wu, tm=1024, tn=512):
    t, dm = xb.shape
    f = wg.shape[1]
    tm = min(tm, t)
    wspec = pl.BlockSpec((dm, tn), lambda i, j: (0, j))
    return pl.pallas_call(
        _ffn_up_kernel,
        grid=(t // tm, f // tn),
        in_specs=[pl.BlockSpec((tm, dm), lambda i, j: (i, 0)), wspec, wspec],
        out_specs=pl.BlockSpec((tm, tn), lambda i, j: (i, j)),
        out_shape=jax.ShapeDtypeStruct((t, f), BF16),
        compiler_params=_params("parallel", "parallel"),
        name="ffn_up",
    )(xb, wg, wu)


def _top2(x, w):
    logits = jnp.dot(x, w, precision=HIGHEST, preferred_element_type=F32)
    lane = lax.broadcasted_iota(jnp.int32, logits.shape, 1)
    neg = jnp.float32(-jnp.inf)
    logits = jnp.where(lane < N_EXPERTS, logits, neg)
    m1 = jnp.max(logits, axis=1, keepdims=True)
    i1 = jnp.min(jnp.where(logits == m1, lane, LANES), axis=1, keepdims=True)
    rest = jnp.where(lane == i1, neg, logits)
    m2 = jnp.max(rest, axis=1, keepdims=True)
    i2 = jnp.min(jnp.where(rest == m2, lane, LANES), axis=1, keepdims=True)
    e2 = jnp.exp(m2 - m1)
    w1 = 1.0 / (1.0 + e2)
    w2 = e2 / (1.0 + e2)
    idx = jnp.where(lane == 0, i1, jnp.where(lane == 1, i2, 0))
    wts = jnp.where(lane == 0, w1, jnp.where(lane == 1, w2, 0.0))
    return idx, wts


def _ln_router_kernel(x_ref, m_ref, g_ref, b_ref, rw_ref, o_ref, ob_ref, idx_ref, wt_ref, *, alpha):
    y = _residual_layernorm(x_ref[...], m_ref[...], g_ref[...], b_ref[...], alpha)
    o_ref[...] = y
    ob_ref[...] = y.astype(BF16)
    idx_ref[...], wt_ref[...] = _top2(y, rw_ref[...])


def _residual_ln_router(x, mix, g, b, alpha, router_w, tb=256):
    t, dm = x.shape
    tb = min(tb, t)
    w = jnp.pad(router_w, ((0, 0), (0, LANES - router_w.shape[1])))
    tok = pl.BlockSpec((tb, dm), lambda i: (i, 0))
    vec = pl.BlockSpec((1, dm), lambda i: (0, 0))
    route = pl.BlockSpec((tb, LANES), lambda i: (i, 0))
    return pl.pallas_call(
        functools.partial(_ln_router_kernel, alpha=alpha),
        grid=(t // tb,),
        in_specs=[tok, tok, vec, vec, pl.BlockSpec((dm, LANES), lambda i: (0, 0))],
        out_specs=[tok, tok, route, route],
        out_shape=[jax.ShapeDtypeStruct((t, dm), F32), jax.ShapeDtypeStruct((t, dm), BF16),
                   jax.ShapeDtypeStruct((t, LANES), jnp.int32), jax.ShapeDtypeStruct((t, LANES), F32)],
        compiler_params=_params("parallel"),
        name="residual_ln_router",
    )(x, mix, g.reshape(1, dm), b.reshape(1, dm), w)


MOE_TM = 512


def _moe_dispatch(idx, wts, tm):
    t = idx.shape[0]
    n_rows = 2 * t + N_EXPERTS * tm
    n_tiles = n_rows // tm
    e_ids = idx[:, :2]
    per_tok = (e_ids[:, :, None] == jnp.arange(N_EXPERTS)).astype(jnp.int32).sum(axis=1)
    csum = jnp.cumsum(per_tok, axis=0)
    counts = csum[-1]
    rank = csum - per_tok
    padded = (counts + tm - 1) // tm * tm
    ends = jnp.cumsum(padded)
    offs = ends - padded
    pos = offs[e_ids] + jnp.take_along_axis(rank, e_ids, axis=1)
    code = jnp.full((n_rows,), -1, jnp.int32).at[pos.reshape(-1)].set(jnp.arange(2 * t, dtype=jnp.int32))
    src = jnp.maximum(code, 0) // 2
    row_w = jnp.where(code >= 0, wts[:, :2].reshape(-1)[jnp.maximum(code, 0)], 0.0)
    tile_start = jnp.arange(n_tiles, dtype=jnp.int32) * tm
    owner = jnp.sum(tile_start[:, None] >= ends[None, :], axis=1).astype(jnp.int32)
    tile_valid = (owner < N_EXPERTS).astype(jnp.int32)
    tile_expert = jnp.minimum(owner, N_EXPERTS - 1)
    return src, row_w, pos.astype(jnp.int32), tile_expert, tile_valid


def _start_row_gather(idx_ref, n, src_hbm, dst_ref, sem):
    def issue(r, carry):
        pltpu.make_async_copy(src_hbm.at[pl.ds(idx_ref[0, r], 1), :], dst_ref.at[pl.ds(r, 1), :], sem).start()
        return carry

    lax.fori_loop(0, n, issue, 0)


def _wait_row_gather(n, src_hbm, dst_ref, sem):
    pltpu.make_async_copy(src_hbm.at[pl.ds(0, n), :], dst_ref.at[pl.ds(0, n), :], sem).wait()


def _gather_rows(idx_ref, n, src_hbm, dst_ref, sem):
    _start_row_gather(idx_ref, n, src_hbm, dst_ref, sem)
    _wait_row_gather(n, src_hbm, dst_ref, sem)


def _moe_up_sparse_kernel(te_ref, tv_ref, src_ref, nxt_ref, x_hbm, wg_ref, wu_ref, roww_ref, h_ref, xbuf, xb_buf,
                          sem, *, tm, n_tiles):
    i = pl.program_id(0)
    valid = tv_ref[i] == 1
    first_col = pl.program_id(1) == 0

    @pl.when(jnp.logical_and(first_col, jnp.logical_and(valid, i == 0)))
    def _():
        _start_row_gather(src_ref, tm, x_hbm, xbuf, sem)

    @pl.when(jnp.logical_and(first_col, valid))
    def _():
        _wait_row_gather(tm, x_hbm, xbuf, sem)
        xb_buf[...] = xbuf[...].astype(BF16)

    nxt = jnp.minimum(i + 1, n_tiles - 1)

    @pl.when(jnp.logical_and(first_col, jnp.logical_and(i + 1 < n_tiles, tv_ref[nxt] == 1)))
    def _():
        _start_row_gather(nxt_ref, tm, x_hbm, xbuf, sem)

    @pl.when(valid)
    def _():
        x = xb_buf[...]
        a = jnp.dot(x, wg_ref[...], preferred_element_type=F32)
        u = jnp.dot(x, wu_ref[...], preferred_element_type=F32)
        h_ref[...] = (a * jax.nn.sigmoid(a) * u * roww_ref[...]).astype(h_ref.dtype)

    @pl.when(jnp.logical_not(valid))
    def _():
        h_ref[...] = jnp.zeros_like(h_ref)


def _moe_up_sparse(x, we_gate, we_up, src, row_w, tile_expert, tile_valid, tm, tn=512):
    t, dm = x.shape
    n_e, _, fe = we_gate.shape
    n_rows = src.shape[0]
    n_tiles = n_rows // tm
    wspec = pl.BlockSpec((None, dm, tn), lambda i, j, te, tv: (te[i], 0, j))
    grid_spec = pltpu.PrefetchScalarGridSpec(
        num_scalar_prefetch=2,
        grid=(n_tiles, fe // tn),
        in_specs=[pl.BlockSpec((None, 1, tm), lambda i, j, te, tv: (i, 0, 0), memory_space=pltpu.SMEM),
                  pl.BlockSpec((None, 1, tm), lambda i, j, te, tv: (jnp.minimum(i + 1, n_tiles - 1), 0, 0),
                               memory_space=pltpu.SMEM),
                  pl.BlockSpec(memory_space=pl.ANY),
                  wspec, wspec,
                  pl.BlockSpec((tm, 1), lambda i, j, te, tv: (i, 0))],
        out_specs=pl.BlockSpec((tm, tn), lambda i, j, te, tv: (i, j)),
        scratch_shapes=[pltpu.VMEM((tm, dm), F32), pltpu.VMEM((tm, dm), BF16), pltpu.SemaphoreType.DMA(())],
    )
    return pl.pallas_call(
        functools.partial(_moe_up_sparse_kernel, tm=tm, n_tiles=n_tiles),
        grid_spec=grid_spec,
        out_shape=jax.ShapeDtypeStruct((n_rows, fe), BF16),
        compiler_params=_params("arbitrary", "arbitrary"),
        name="moe_up_sparse",
    )(tile_expert, tile_valid, src.reshape(n_tiles, 1, tm), src.reshape(n_tiles, 1, tm), x, we_gate, we_up,
      row_w.reshape(n_rows, 1))


def _moe_down_kernel(te_ref, tv_ref, h_ref, w_ref, y_ref):
    valid = tv_ref[pl.program_id(1)] == 1

    @pl.when(valid)
    def _():
        y_ref[...] = jnp.dot(h_ref[...], w_ref[...], preferred_element_type=F32)

    @pl.when(jnp.logical_not(valid))
    def _():
        y_ref[...] = jnp.zeros_like(y_ref)


def _moe_down_sparse(h, we_down, tile_expert, tile_valid, tm, tn=1024):
    n_rows, fe = h.shape
    dm = we_down.shape[2]
    grid_spec = pltpu.PrefetchScalarGridSpec(
        num_scalar_prefetch=2,
        grid=(dm // tn, n_rows // tm),
        in_specs=[pl.BlockSpec((tm, fe), lambda j, i, te, tv: (i, 0)),
                  pl.BlockSpec((None, fe, tn), lambda j, i, te, tv: (te[i], 0, j))],
        out_specs=pl.BlockSpec((tm, tn), lambda j, i, te, tv: (i, j)),
    )
    return pl.pallas_call(
        _moe_down_kernel,
        grid_spec=grid_spec,
        out_shape=jax.ShapeDtypeStruct((n_rows, dm), F32),
        compiler_params=_params("parallel", "parallel"),
        name="moe_down_sparse",
    )(tile_expert, tile_valid, h, we_down)


def _moe_combine_ln_kernel(pos_ref, nxt_ref, y_hbm, x_ref, g_ref, b_ref, o_ref, ob_ref, ybuf, sem, *, alpha, tb,
                           n_steps):
    i = pl.program_id(0)
    slot = i % 2

    @pl.when(i == 0)
    def _():
        _start_row_gather(pos_ref, 2 * tb, y_hbm, ybuf.at[0], sem.at[0])

    @pl.when(i + 1 < n_steps)
    def _():
        _start_row_gather(nxt_ref, 2 * tb, y_hbm, ybuf.at[1 - slot], sem.at[1 - slot])

    _wait_row_gather(2 * tb, y_hbm, ybuf.at[slot], sem.at[slot])
    rows = ybuf.at[slot]
    y = _residual_layernorm(x_ref[...], rows[0:tb, :] + rows[tb:2 * tb, :], g_ref[...], b_ref[...], alpha)
    o_ref[...] = y
    ob_ref[...] = y.astype(BF16)


def _moe_combine_ln(x, ys, pos, g, b, alpha, tb=256):
    t, dm = x.shape
    tb = min(tb, t)
    tok = pl.BlockSpec((tb, dm), lambda i: (i, 0))
    vec = pl.BlockSpec((1, dm), lambda i: (0, 0))
    n_steps = t // tb
    rows = jnp.concatenate([pos[:, 0].reshape(n_steps, 1, tb), pos[:, 1].reshape(n_steps, 1, tb)], axis=2)
    return pl.pallas_call(
        functools.partial(_moe_combine_ln_kernel, alpha=alpha, tb=tb, n_steps=n_steps),
        grid=(n_steps,),
        in_specs=[pl.BlockSpec((None, 1, 2 * tb), lambda i: (i, 0, 0), memory_space=pltpu.SMEM),
                  pl.BlockSpec((None, 1, 2 * tb), lambda i: (jnp.minimum(i + 1, n_steps - 1), 0, 0),
                               memory_space=pltpu.SMEM),
                  pl.BlockSpec(memory_space=pl.ANY), tok, vec, vec],
        out_specs=[tok, tok],
        out_shape=[jax.ShapeDtypeStruct((t, dm), F32), jax.ShapeDtypeStruct((t, dm), BF16)],
        scratch_shapes=[pltpu.VMEM((2, 2 * tb, dm), F32), pltpu.SemaphoreType.DMA((2,))],
        compiler_params=_params("arbitrary"),
        name="moe_combine_ln",
    )(rows, rows, ys, x, g.reshape(1, dm), b.reshape(1, dm))


def _ple_kernel(xb_ref, wg_ref, p_ref, wp_ref, x_ref, o_ref, *maybe_ob_ref):
    gate = jax.nn.sigmoid(jnp.dot(xb_ref[...], wg_ref[...], preferred_element_type=F32))
    emb = jnp.dot(p_ref[...], wp_ref[...], preferred_element_type=F32)
    y = x_ref[...] + gate * emb
    o_ref[...] = y
    for ob_ref in maybe_ob_ref:
        ob_ref[...] = y.astype(BF16)


def _ple(xb, x, w_pgate, p, w_pproj, rows=None, tm=1024, tn=512):
    t, dm = xb.shape
    pd = p.shape[1]
    start, count = (0, t) if rows is None else rows
    tm = min(tm, count)
    assert start % tm == 0 and count % tm == 0
    off = start // tm
    out = pl.BlockSpec((tm, tn), lambda i, j: (i, j))
    outs = [jax.ShapeDtypeStruct((count, dm), F32)] + ([jax.ShapeDtypeStruct((count, dm), BF16)] if rows is None else [])
    return pl.pallas_call(
        _ple_kernel,
        grid=(count // tm, dm // tn),
        in_specs=[pl.BlockSpec((tm, dm), lambda i, j: (i + off, 0)),
                  pl.BlockSpec((dm, tn), lambda i, j: (0, j)),
                  pl.BlockSpec((tm, pd), lambda i, j: (i + off, 0)),
                  pl.BlockSpec((pd, tn), lambda i, j: (0, j)),
                  pl.BlockSpec((tm, tn), lambda i, j: (i + off, j))],
        out_specs=[out] * len(outs),
        out_shape=outs,
        compiler_params=_params("parallel", "parallel"),
        name="ple_gate",
    )(xb, w_pgate, p, w_pproj, x)


def _pad_cols(w, width):
    return jnp.pad(w, ((0, 0), (0, width - w.shape[1])))


def _pad_rows(w, rows):
    pad = [(0, 0)] * w.ndim
    pad[-2] = (0, rows - w.shape[-2])
    return jnp.pad(w, pad)


def _split_rwkv_cols(w, dg, dr, has_mv):
    base = 2 * dg
    rkv = w[:, base:base + 3 * dr]
    o = base + 3 * dr
    sizes = [DECAY_LORA, DECAY_LORA, AAA_LORA, AAA_LORA, GATE_LORA] + ([MV_LORA] if has_mv else [])
    slots = [LORA_SLOT, LORA_SLOT, LORA_SLOT, LORA_SLOT, GATE_LORA, LORA_SLOT]
    parts = []
    for n, slot in zip(sizes, slots):
        parts.append(_pad_cols(w[:, o:o + n], slot))
        o += n
    lora = _pad_cols(jnp.concatenate(parts, axis=1), LORA_COLS)
    return jnp.concatenate([rkv, lora], axis=1)


def _trunk(x, p, P, segs):
    (t1, _), (t2, _) = segs
    t = t1 + t2
    depth = P['w_out'].shape[0]
    alpha = (2 * depth) ** 0.25
    dg = P['sgu_ln_g'].shape[1]
    dr = P['k_k'].shape[1]
    hb = lax.broadcasted_iota(jnp.int32, (SCAN_W, SCAN_W), 0) // RWKV_HEAD
    hc = lax.broadcasted_iota(jnp.int32, (SCAN_W, SCAN_W), 1) // RWKV_HEAD
    ones_bd = (hb == hc).astype(BF16)
    xb = jnp.concatenate([part.astype(BF16) for part in x], axis=0)
    w_out_b = _to_bf16(P['w_out'])
    w_pgate_b = _to_bf16(P['w_pgate'])
    v_first = None
    for l in range(depth):
        if l == 0:
            w_in_l, conv_l, v0_l, v2_l = P['w_in0'], P['conv0'], None, None
        else:
            w_in_l, conv_l, v0_l, v2_l = P['w_in'][l - 1], P['conv'][l - 1], P['v0'][l - 1], P['v2'][l - 1]
        w_in_b = _to_bf16(w_in_l)
        uv = _matmul(xb, w_in_b[:, :2 * dg], BF16)
        zr = _matmul(xb, _split_rwkv_cols(w_in_b, dg, dr, l > 0), BF16)
        y_g = _spatial_gating(uv, P['sgu_ln_g'][l], P['sgu_ln_b'][l], P['w_s'][l], P['b_s'][l])
        conv_r = _split_rwkv_cols(jnp.pad(conv_l, ((0, 0), (2 * dg, 0))), dg, dr, l > 0)
        prep_args = (zr, conv_r, P['w0'][l], _pad_rows(P['w2'][l], LORA_SLOT).astype(BF16), P['a0'][l],
                     _pad_rows(P['a2'][l], LORA_SLOT).astype(BF16), P['g2'][l].astype(BF16),
                     P['k_k'][l], P['k_a'][l], P['r_k'][l].reshape(-1), ones_bd, segs)
        if l == 0:
            r, v, kk, g, bonus, lw, k, b = _rwkv_prep(*prep_args)
            v_first = v
        else:
            r, v, kk, g, bonus, lw, k, b = _rwkv_prep(*prep_args, v_first=v_first, v0=v0_l,
                                                      v2=_pad_rows(v2_l, LORA_SLOT).astype(BF16))
        y_f, y_b = _rwkv_scan(r, v, kk, lw, k, b, segs)
        y_r = _rwkv_post(y_f, y_b, g, bonus, P['gn_g'][l], P['gn_b'][l], ones_bd)
        w_out = w_out_b[l]
        mix = _matmul_pair(y_g, w_out[:dg], y_r, w_out[dg:], BF16)
        j = l // 2
        if l % 2 == 0:
            x, xb = _residual_ln(x, mix, P['ln1_g'][l], P['ln1_b'][l], alpha)
            h = _ffn_up(xb, _to_bf16(P['w_ff_gate'][j]), _to_bf16(P['w_ff_up'][j]))
            ff = _matmul(h, _to_bf16(P['w_ff_down'][j]), BF16, tk=2048)
        else:
            if isinstance(x, tuple):
                x = jnp.concatenate(x, axis=0)
            x, xb, idx, wts = _residual_ln_router(x, mix, P['ln1_g'][l], P['ln1_b'][l], alpha, P['router'][j])
            tm = min(MOE_TM, t)
            src, row_w, pos, tile_expert, tile_valid = _moe_dispatch(idx, wts, tm)
            h = _moe_up_sparse(x, _to_bf16(P['we_gate'][j]), _to_bf16(P['we_up'][j]), src, row_w,
                               tile_expert, tile_valid, tm)
            ys = _moe_down_sparse(h, _to_bf16(P['we_down'][j]), tile_expert, tile_valid, tm)
            x, xb = _moe_combine_ln(x, ys, pos, P['ln2_g'][l], P['ln2_b'][l], alpha)
        if l % 2 == 0:
            x, xb = _residual_ln(x, ff, P['ln2_g'][l], P['ln2_b'][l], alpha)
        ple_args = (xb, x, w_pgate_b[l], p[l].astype(BF16), P['w_pproj'][l].astype(BF16))
        if l < depth - 1:
            x, xb = _ple(*ple_args)
    (y1,), (y2,) = _ple(*ple_args, rows=(0, t1)), _ple(*ple_args, rows=(t1, t2))
    return y1, y2


def kernel(x_prompt, x_sample, p_prompt, p_sample, w_in0, conv0, w_in, conv, sgu_ln_g, sgu_ln_b, w_s, b_s, w0, w2, a0, a2, g2, k_k, k_a, r_k, gn_g, gn_b, v0, v2, w_out, ln1_g, ln1_b, ln2_g, ln2_b, w_ff_gate, w_ff_up, w_ff_down, router, we_gate, we_up, we_down, w_pproj, w_pgate):
    P = dict(w_in0=w_in0, conv0=conv0, w_in=w_in, conv=conv, sgu_ln_g=sgu_ln_g, sgu_ln_b=sgu_ln_b,
             w_s=w_s, b_s=b_s, w0=w0, w2=w2, a0=a0, a2=a2, g2=g2, k_k=k_k, k_a=k_a, r_k=r_k,
             gn_g=gn_g, gn_b=gn_b, v0=v0, v2=v2, w_out=w_out, ln1_g=ln1_g, ln1_b=ln1_b,
             ln2_g=ln2_g, ln2_b=ln2_b, w_ff_gate=w_ff_gate, w_ff_up=w_ff_up, w_ff_down=w_ff_down,
             router=router, we_gate=we_gate, we_up=we_up, we_down=we_down,
             w_pproj=w_pproj, w_pgate=w_pgate)
    b1, s1, dm = x_prompt.shape
    b2, s2, _ = x_sample.shape
    depth, pd = p_prompt.shape[0], p_prompt.shape[-1]
    t1, t2 = b1 * s1, b2 * s2
    segs = ((t1, s1), (t2, s2))
    p = jnp.concatenate([p_prompt.reshape(depth, t1, pd), p_sample.reshape(depth, t2, pd)], axis=1)
    y1, y2 = _trunk((x_prompt.reshape(t1, dm), x_sample.reshape(t2, dm)), p, P, segs)
    return y1.reshape(b1, s1, dm), y2.reshape(b2, s2, dm)
```

```python
import functools

import jax
import jax.numpy as jnp
from jax import lax
from jax.experimental import pallas as pl
from jax.experimental.pallas import tpu as pltpu

F32 = jnp.float32
BF16 = jnp.bfloat16
HIGHEST = lax.Precision.HIGHEST

LANES = 128
BF16_SUBLANES = 16
VMEM_LIMIT_BYTES = 56 * 1024 * 1024

CHUNK = 128
GMLP_HEAD = 128
RWKV_HEAD = 64
DECAY_LORA = 96
AAA_LORA = 96
MV_LORA = 64
GATE_LORA = 256
N_EXPERTS = 8
LN_EPS = 1e-5
GN_EPS = 64e-5
L2_EPS = 1e-12
DECAY_SCALE = 0.6065306597126334

SCAN_L = 64
SCAN_W = 256
HEADS_PER_GROUP = SCAN_W // RWKV_HEAD
LORA_SLOT = 128
LORA_COLS = 1024


def _params(*sem):
    return pltpu.CompilerParams(dimension_semantics=sem, vmem_limit_bytes=VMEM_LIMIT_BYTES)


def _is_seq_start(pos, segs):
    (t1, s1), (_, s2) = segs
    return jnp.where(pos < t1, pos % s1 == 0, (pos - t1) % s2 == 0)


CAST_BLOCK_BYTES = 8 * 1024 * 1024


def _cast_kernel(x_ref, o_ref):
    o_ref[...] = x_ref[...].astype(o_ref.dtype)


def _to_bf16(w):
    w2 = w.reshape(-1, w.shape[-1])
    rows, n = w2.shape
    tb = rows
    while tb * n * w2.dtype.itemsize > CAST_BLOCK_BYTES and tb % 2 == 0 and (tb // 2) % BF16_SUBLANES == 0:
        tb //= 2
    out = pl.pallas_call(
        _cast_kernel,
        grid=(rows // tb,),
        in_specs=[pl.BlockSpec((tb, n), lambda i: (i, 0))],
        out_specs=pl.BlockSpec((tb, n), lambda i: (i, 0)),
        out_shape=jax.ShapeDtypeStruct((rows, n), BF16),
        compiler_params=_params("parallel"),
        name="cast_bf16",
    )(w2)
    return out.reshape(w.shape)


def _mm_kernel(x_ref, w_ref, o_ref, *scratch, nk):
    if nk == 1:
        o_ref[...] = jnp.dot(x_ref[...], w_ref[...], preferred_element_type=F32).astype(o_ref.dtype)
        return
    acc_ref, = scratch
    k = pl.program_id(2)

    @pl.when(k == 0)
    def _():
        acc_ref[...] = jnp.zeros_like(acc_ref)

    acc_ref[...] += jnp.dot(x_ref[...], w_ref[...], preferred_element_type=F32)

    @pl.when(k == nk - 1)
    def _():
        o_ref[...] = acc_ref[...].astype(o_ref.dtype)


def _matmul(x, w, out_dtype, tm=1024, tn=1024, tk=None):
    m, kdim = x.shape
    n = w.shape[1]
    tm, tn = min(tm, m), min(tn, n)
    tk = kdim if tk is None else min(tk, kdim)
    nk = kdim // tk
    assert m % tm == 0 and n % tn == 0 and kdim % tk == 0
    scratch = [] if nk == 1 else [pltpu.VMEM((tm, tn), F32)]
    return pl.pallas_call(
        functools.partial(_mm_kernel, nk=nk),
        grid=(m // tm, n // tn, nk),
        in_specs=[pl.BlockSpec((tm, tk), lambda i, j, k: (i, k)),
                  pl.BlockSpec((tk, tn), lambda i, j, k: (k, j))],
        out_specs=pl.BlockSpec((tm, tn), lambda i, j, k: (i, j)),
        out_shape=jax.ShapeDtypeStruct((m, n), out_dtype),
        scratch_shapes=scratch,
        compiler_params=_params("parallel", "parallel", "arbitrary"),
        name="matmul",
    )(x, w)


def _mm_pair_kernel(xa_ref, wa_ref, xb_ref, wb_ref, o_ref):
    acc = jnp.dot(xa_ref[...], wa_ref[...], preferred_element_type=F32)
    acc += jnp.dot(xb_ref[...], wb_ref[...], preferred_element_type=F32)
    o_ref[...] = acc.astype(o_ref.dtype)


def _matmul_pair(xa, wa, xb, wb, out_dtype, tm=1024, tn=1024):
    m, ka = xa.shape
    kb = xb.shape[1]
    n = wa.shape[1]
    tm, tn = min(tm, m), min(tn, n)
    return pl.pallas_call(
        _mm_pair_kernel,
        grid=(m // tm, n // tn),
        in_specs=[pl.BlockSpec((tm, ka), lambda i, j: (i, 0)),
                  pl.BlockSpec((ka, tn), lambda i, j: (0, j)),
                  pl.BlockSpec((tm, kb), lambda i, j: (i, 0)),
                  pl.BlockSpec((kb, tn), lambda i, j: (0, j))],
        out_specs=pl.BlockSpec((tm, tn), lambda i, j: (i, j)),
        out_shape=jax.ShapeDtypeStruct((m, n), out_dtype),
        compiler_params=_params("parallel", "parallel"),
        name="matmul_pair",
    )(xa, wa, xb, wb)


def _sgu_kernel(u_ref, v_ref, g_ref, b_ref, ws_ref, bs_ref, o_ref, *, n_chunks):
    gv = jax.nn.gelu(v_ref[...].astype(F32))
    mu = jnp.mean(gv, axis=-1, keepdims=True)
    cen = gv - mu
    var = jnp.mean(cen * cen, axis=-1, keepdims=True)
    vn = (cen * lax.rsqrt(var + LN_EPS) * g_ref[...] + b_ref[...]).astype(BF16)
    n_heads = ws_ref.shape[0]
    for c in range(n_chunks):
        rows = slice(c * CHUNK, (c + 1) * CHUNK)
        for h in range(n_heads):
            cols = slice(h * GMLP_HEAD, (h + 1) * GMLP_HEAD)
            sv = jnp.dot(ws_ref[h], vn[rows, cols], preferred_element_type=F32) + bs_ref[:, cols]
            gu = jax.nn.gelu(u_ref[rows, cols].astype(F32))
            o_ref[rows, cols] = (gu * sv).astype(o_ref.dtype)


def _spatial_gating(uv, ln_g, ln_b, w_s, b_s, tb=512):
    t = uv.shape[0]
    dg = uv.shape[1] // 2
    tb = min(tb, t)
    n_heads = w_s.shape[0]
    bs_full = jnp.repeat(b_s.T, GMLP_HEAD, axis=1)
    return pl.pallas_call(
        functools.partial(_sgu_kernel, n_chunks=tb // CHUNK),
        grid=(t // tb,),
        in_specs=[pl.BlockSpec((tb, dg), lambda i: (i, 0)),
                  pl.BlockSpec((tb, dg), lambda i: (i, 1)),
                  pl.BlockSpec((1, dg), lambda i: (0, 0)),
                  pl.BlockSpec((1, dg), lambda i: (0, 0)),
                  pl.BlockSpec((n_heads, CHUNK, CHUNK), lambda i: (0, 0, 0)),
                  pl.BlockSpec((CHUNK, dg), lambda i: (0, 0))],
        out_specs=pl.BlockSpec((tb, dg), lambda i: (i, 0)),
        out_shape=jax.ShapeDtypeStruct((t, dg), BF16),
        compiler_params=_params("parallel"),
        name="spatial_gating",
    )(uv, uv, ln_g.reshape(1, dg), ln_b.reshape(1, dg), w_s.astype(BF16), bs_full)


def _head_sum(x, ones_bd):
    parts = []
    for g in range(x.shape[1] // SCAN_W):
        xs = x[:, g * SCAN_W:(g + 1) * SCAN_W]
        hi = xs.astype(BF16)
        lo = (xs - hi.astype(F32)).astype(BF16)
        parts.append(jnp.dot(hi, ones_bd, preferred_element_type=F32)
                     + jnp.dot(lo, ones_bd, preferred_element_type=F32))
    return jnp.concatenate(parts, axis=1)


def _prep_kernel(*refs, layer1, tb, segs, dr):
    if layer1:
        (z_ref, zp_ref, zn_ref, conv_ref, w0_ref, w2_ref, a0_ref, a2_ref, g2_ref, kk_ref, ka_ref, rk_ref,
         ones_ref, vf_ref, v0_ref, v2_ref, r_o, v_o, kkn_o, g_o, bonus_o, lw_o, k_o, b_o) = refs
    else:
        (z_ref, zp_ref, zn_ref, conv_ref, w0_ref, w2_ref, a0_ref, a2_ref, g2_ref, kk_ref, ka_ref, rk_ref,
         ones_ref, r_o, v_o, kkn_o, g_o, bonus_o, lw_o, k_o, b_o) = refs
    pos0 = pl.program_id(0) * tb
    first = _is_seq_start(pos0, segs)
    last = _is_seq_start(pos0 + tb, segs)
    row = lax.broadcasted_iota(jnp.int32, (tb, 1), 0)

    def conv(c0, width):
        cols = slice(c0, c0 + width)
        z = z_ref[:, cols].astype(F32)
        prev_row = jnp.where(first, 0.0, zp_ref[:, cols].astype(F32)[BF16_SUBLANES - 1:BF16_SUBLANES])
        next_row = jnp.where(last, 0.0, zn_ref[:, cols].astype(F32)[0:1])
        zprev = jnp.where(row == 0, prev_row, pltpu.roll(z, 1, 0))
        znext = jnp.where(row == tb - 1, next_row, pltpu.roll(z, tb - 1, 0))
        return zprev * conv_ref[0:1, cols] + z * conv_ref[1:2, cols] + znext * conv_ref[2:3, cols]

    lo = conv(3 * dr, LORA_COLS)
    s = LORA_SLOT
    wd = (lo[:, 0:s], lo[:, s:2 * s])
    ad = (lo[:, 2 * s:3 * s], lo[:, 3 * s:4 * s])
    gd = lo[:, 4 * s:6 * s]

    def lora(x, w):
        return jnp.dot(x.astype(BF16), w, preferred_element_type=F32)

    g_o[...] = lora(jax.nn.sigmoid(gd), g2_ref[...]).astype(g_o.dtype)
    v = conv(2 * dr, dr)
    if layer1:
        mv = lo[:, 6 * s:7 * s]
        v = v + (vf_ref[...].astype(F32) - v) * jax.nn.sigmoid(v0_ref[...] + lora(mv, v2_ref[...]))
    v_o[...] = v.astype(v_o.dtype)
    r = conv(0, dr)
    r_o[...] = r.astype(r_o.dtype)
    k = conv(dr, dr)
    ones_bd = ones_ref[...]
    kk = k * kk_ref[...]
    kk = kk * lax.rsqrt(_head_sum(kk * kk, ones_bd) + L2_EPS)
    kkn_o[...] = kk.astype(kkn_o.dtype)
    for d in range(2):
        zw = w0_ref[d:d + 1, :] + lora(jnp.tanh(wd[d]), w2_ref[d])
        lw_o[d] = -DECAY_SCALE / (1.0 + jnp.exp(-zw))
        a = jax.nn.sigmoid(a0_ref[d:d + 1, :] + lora(ad[d], a2_ref[d]))
        k_d = k * (1.0 + (a - 1.0) * ka_ref[...])
        k_o[d] = k_d.astype(k_o.dtype)
        b_o[d] = (kk * a).astype(b_o.dtype)
        if d == 0:
            bonus_o[...] = (_head_sum(r * k_d * rk_ref[...], ones_bd) * v).astype(bonus_o.dtype)


def _rwkv_prep(zr, conv_w, w0, w2, a0, a2, g2, k_k, k_a, r_k, ones_bd, segs, v_first=None, v0=None, v2=None,
               tb=128):
    t, c = zr.shape
    dr = k_k.shape[-1]
    tb = min(tb, t)
    layer1 = v_first is not None
    hb = tb // BF16_SUBLANES
    n_halo = t // BF16_SUBLANES
    const2 = lambda i: (0, 0)
    const3 = lambda i: (0, 0, 0)
    in_specs = [pl.BlockSpec((tb, c), lambda i: (i, 0)),
                pl.BlockSpec((BF16_SUBLANES, c), lambda i: (jnp.maximum(i * hb - 1, 0), 0)),
                pl.BlockSpec((BF16_SUBLANES, c), lambda i: (jnp.minimum((i + 1) * hb, n_halo - 1), 0)),
                pl.BlockSpec((3, c), const2),
                pl.BlockSpec((2, dr), const2),
                pl.BlockSpec((2, LORA_SLOT, dr), const3),
                pl.BlockSpec((2, dr), const2),
                pl.BlockSpec((2, LORA_SLOT, dr), const3),
                pl.BlockSpec((GATE_LORA, dr), const2),
                pl.BlockSpec((1, dr), const2),
                pl.BlockSpec((1, dr), const2),
                pl.BlockSpec((1, dr), const2),
                pl.BlockSpec((SCAN_W, SCAN_W), const2)]
    args = [zr, zr, zr, conv_w, w0, w2, a0, a2, g2, k_k.reshape(1, dr), k_a.reshape(1, dr), r_k.reshape(1, dr),
            ones_bd]
    if layer1:
        in_specs += [pl.BlockSpec((tb, dr), lambda i: (i, 0)),
                     pl.BlockSpec((1, dr), const2),
                     pl.BlockSpec((LORA_SLOT, dr), const2)]
        args += [v_first, v0.reshape(1, dr), v2]
    tok = pl.BlockSpec((tb, dr), lambda i: (i, 0))
    tok2 = pl.BlockSpec((2, tb, dr), lambda i: (0, i, 0))
    one = jax.ShapeDtypeStruct((t, dr), BF16)
    two = jax.ShapeDtypeStruct((2, t, dr), BF16)
    log_decay = jax.ShapeDtypeStruct((2, t, dr), F32)
    return pl.pallas_call(
        functools.partial(_prep_kernel, layer1=layer1, tb=tb, segs=segs, dr=dr),
        grid=(t // tb,),
        in_specs=in_specs,
        out_specs=[tok, tok, tok, tok, tok, tok2, tok2, tok2],
        out_shape=[one, one, one, one, one, log_decay, two, two],
        compiler_params=_params("parallel"),
        name="rwkv_prep",
    )(*args)


def _scan_consts(reverse):
    L, W, G = SCAN_L, SCAN_W, HEADS_PER_GROUP
    sgn = -1 if reverse else 1
    t_i = lax.broadcasted_iota(jnp.int32, (L, 3 * L), 0)
    s_i = lax.broadcasted_iota(jnp.int32, (L, 3 * L), 1) % L
    tri3 = jnp.where(sgn * (t_i - s_i) >= 0, 1.0, 0.0).astype(BF16)
    tc = lax.broadcasted_iota(jnp.int32, (L, G * L), 0)
    sc = lax.broadcasted_iota(jnp.int32, (L, G * L), 1) % L
    before = sgn * (tc - sc)
    strict = before > 0
    rmask = before > 0 if reverse else before >= 0
    eye = jnp.where(sc == tc, 1.0, 0.0).astype(F32)
    bi = lax.broadcasted_iota(jnp.int32, (W, W), 0) // RWKV_HEAD
    bj = lax.broadcasted_iota(jnp.int32, (W, W), 1) // RWKV_HEAD
    return tri3, strict, rmask, eye, bi == bj


def _bd(x):
    per_tile = LANES // RWKV_HEAD
    n_tiles = x.shape[1] // LANES
    lane_head = lax.broadcasted_iota(jnp.int32, (x.shape[0], LANES), 1) // RWKV_HEAD
    zero = jnp.zeros((x.shape[0], LANES), x.dtype)
    blocks = []
    for h in range(HEADS_PER_GROUP):
        own = h // per_tile
        kept = jnp.where(lane_head == h % per_tile, x[:, own * LANES:(own + 1) * LANES], zero)
        blocks.append(jnp.concatenate([kept if tile == own else zero for tile in range(n_tiles)], axis=1))
    return jnp.concatenate(blocks, axis=0)


_NT = (((1,), (1,)), ((), ()))
_TN = (((0,), (0,)), ((), ()))


def _chunk_operators(r, v, kk, lw, k, b, consts, reverse):
    L, W, G = SCAN_L, SCAN_W, HEADS_PER_GROUP
    tri3, strict, rmask, eye, _ = consts
    bd = _bd

    def dot(a_, b_):
        return jnp.dot(a_, b_, preferred_element_type=F32)

    hi = lw.astype(BF16)
    rem = lw - hi.astype(F32)
    mid = rem.astype(BF16)
    low = (rem - mid.astype(F32)).astype(BF16)
    cum = dot(tri3, jnp.concatenate([hi, mid, low], axis=0))
    yield
    tot = jnp.sum(lw, axis=0, keepdims=True)
    e_ex = jnp.exp(cum - lw)
    e_neg = jnp.exp(-cum)
    e_rem = jnp.exp(tot - cum)
    kq = kk * e_ex
    rq = r * (e_ex if reverse else jnp.exp(cum))
    vb = v.astype(BF16)
    q = jnp.concatenate([kq, rq], axis=0).astype(BF16)
    keys = jnp.concatenate([bd((k * e_neg).astype(BF16)), bd((b * e_neg).astype(BF16))], axis=0)
    a = lax.dot_general(q, keys, _NT, preferred_element_type=F32)
    yield
    a_kk = jnp.where(strict, a[:L, :G * L], 0.0)
    a_kb = jnp.where(strict, a[:L, G * L:], 0.0)
    a_rk = jnp.where(rmask, a[L:, :G * L], 0.0)
    a_rb = jnp.where(rmask, a[L:, G * L:], 0.0).astype(BF16)
    p = -a_kb
    tinv = eye + p
    av = dot(jnp.concatenate([a_kk, a_rk], axis=0).astype(BF16), bd(vb))
    p = dot(p.astype(BF16), bd(p.astype(BF16)))
    yield
    levels = L.bit_length() - 2
    for lvl in range(levels):
        wt = bd(p.astype(BF16))
        if lvl < levels - 1:
            tp = dot(jnp.concatenate([tinv, p], axis=0).astype(BF16), wt)
            tinv = tinv + tp[:L]
            p = tp[L:]
        else:
            tinv = tinv + dot(tinv.astype(BF16), wt)
        yield
    tinv = tinv.astype(BF16)
    tk = dot(tinv, jnp.concatenate([bd(kq.astype(BF16)), bd(av[:L].astype(BF16))], axis=1))
    yield
    kq2 = tk[:, :W]
    cu = tk[:, W:]
    ar = dot(a_rb, jnp.concatenate([bd(kq2.astype(BF16)), bd(cu.astype(BF16))], axis=1))
    yield
    rq2 = rq - ar[:, :W]
    yc = av[L:] - ar[:, W:]
    qs = jnp.concatenate([rq2, kq2], axis=0).astype(BF16)
    kb = jnp.concatenate([k * e_rem, b * e_rem], axis=0).astype(BF16)
    return qs, yc, cu, jnp.exp(tot), kb, vb


def _run_in_lockstep(gens):
    results = [None] * len(gens)
    active = list(enumerate(gens))
    while active:
        still = []
        for i, gen in active:
            try:
                next(gen)
                still.append((i, gen))
            except StopIteration as done:
                results[i] = done.value
        active = still
    return results


def _scan_kernel(rf_ref, vf_ref, kkf_ref, lwf_ref, kf_ref, bf_ref, rb_ref, vb_ref, kkb_ref, lwb_ref, kb_ref,
                 bb_ref, yf_ref, yb_ref, s_ref, *, tb, segs, nblk, ng):
    L, W = SCAN_L, SCAN_W
    c = pl.program_id(1)
    start_f = _is_seq_start(c * tb, segs)
    start_b = _is_seq_start((nblk - c) * tb, segs)

    @pl.when(start_f)
    def _():
        s_ref[0] = jnp.zeros(s_ref.shape[1:], F32)

    @pl.when(start_b)
    def _():
        s_ref[1] = jnp.zeros(s_ref.shape[1:], F32)

    n_chunks = tb // L
    dirs = ((rf_ref, vf_ref, kkf_ref, lwf_ref, kf_ref, bf_ref, yf_ref),
            (rb_ref, vb_ref, kkb_ref, lwb_ref, kb_ref, bb_ref, yb_ref))
    streams = [(d, g) for d in range(2) for g in range(ng)]
    order = {d: [n_chunks - 1 - j if d == 1 else j for j in range(n_chunks)] for d in range(2)}
    consts = {d: _scan_consts(reverse=d == 1) for d in range(2)}
    gens = []
    for d, g in streams:
        r_ref, v_ref, kk_ref, lw_ref, k_ref, b_ref, _ = dirs[d]
        cols = slice(g * W, (g + 1) * W)
        for ci in order[d]:
            rows = slice(ci * L, (ci + 1) * L)
            gens.append(_chunk_operators(*(ref[rows, cols].astype(F32)
                                           for ref in (r_ref, v_ref, kk_ref, lw_ref, k_ref, b_ref)),
                                         consts[d], reverse=d == 1))
    ops = _run_in_lockstep(gens)
    states = [s_ref[d, g] for d, g in streams]
    for j in range(n_chunks):
        for si, (d, g) in enumerate(streams):
            qs, yc, cu, decay, kb, vb = ops[si * n_chunks + j]
            ci = order[d][j]
            yu = lax.dot_general(qs, states[si].astype(BF16), _NT, preferred_element_type=F32)
            y_ref = dirs[d][6]
            y_ref[ci * L:(ci + 1) * L, g * W:(g + 1) * W] = (yu[:L] + yc).astype(y_ref.dtype)
            vu = jnp.concatenate([vb, (-(yu[L:] + cu)).astype(BF16)], axis=0)
            upd = lax.dot_general(vu, kb, _TN, preferred_element_type=F32)
            states[si] = states[si] * decay + jnp.where(consts[d][4], upd, 0.0)
    for si, (d, g) in enumerate(streams):
        s_ref[d, g] = states[si]


def _rwkv_scan(r, v, kk, lw, k, b, segs, tb=256, ng=2):
    t, dr = r.shape
    tb = min(tb, t)
    nblk = t // tb
    w = ng * SCAN_W
    fwd = pl.BlockSpec((tb, w), lambda g, c: (c, g))
    bwd = pl.BlockSpec((tb, w), lambda g, c: (nblk - 1 - c, g))
    fwd2 = pl.BlockSpec((None, tb, w), lambda g, c: (0, c, g))
    bwd2 = pl.BlockSpec((None, tb, w), lambda g, c: (1, nblk - 1 - c, g))
    out = jax.ShapeDtypeStruct((t, dr), BF16)
    return pl.pallas_call(
        functools.partial(_scan_kernel, tb=tb, segs=segs, nblk=nblk, ng=ng),
        grid=(dr // w, nblk),
        in_specs=[fwd, fwd, fwd, fwd2, fwd2, fwd2, bwd, bwd, bwd, bwd2, bwd2, bwd2],
        out_specs=[fwd, bwd],
        out_shape=[out, out],
        scratch_shapes=[pltpu.VMEM((2, ng, SCAN_W, SCAN_W), F32)],
        compiler_params=_params("parallel", "arbitrary"),
        name="rwkv_scan",
    )(r, v, kk, lw, k, b, r, v, kk, lw, k, b)


def _post_kernel(yf_ref, yb_ref, g_ref, bonus_ref, gng_ref, gnb_ref, ones_ref, o_ref):
    y = yf_ref[...].astype(F32) + yb_ref[...].astype(F32)
    ones_bd = ones_ref[...]
    mu = _head_sum(y, ones_bd) * (1.0 / RWKV_HEAD)
    cen = y - mu
    var = _head_sum(cen * cen, ones_bd) * (1.0 / RWKV_HEAD)
    yn = cen * lax.rsqrt(var + GN_EPS) * gng_ref[...] + gnb_ref[...]
    o_ref[...] = ((yn + bonus_ref[...].astype(F32)) * g_ref[...].astype(F32)).astype(o_ref.dtype)


def _rwkv_post(y_f, y_b, g, bonus, gn_g, gn_b, ones_bd, tb=256):
    t, dr = y_f.shape
    tb = min(tb, t)
    tok = pl.BlockSpec((tb, dr), lambda i: (i, 0))
    vec = pl.BlockSpec((1, dr), lambda i: (0, 0))
    return pl.pallas_call(
        _post_kernel,
        grid=(t // tb,),
        in_specs=[tok, tok, tok, tok, vec, vec, pl.BlockSpec((SCAN_W, SCAN_W), lambda i: (0, 0))],
        out_specs=tok,
        out_shape=jax.ShapeDtypeStruct((t, dr), BF16),
        compiler_params=_params("parallel"),
        name="rwkv_post",
    )(y_f, y_b, g, bonus, gn_g.reshape(1, dr), gn_b.reshape(1, dr), ones_bd)


def _residual_layernorm(x, m, g, b, alpha):
    h = alpha * x + m.astype(F32)
    mu = jnp.mean(h, axis=-1, keepdims=True)
    cen = h - mu
    var = jnp.mean(cen * cen, axis=-1, keepdims=True)
    return cen * lax.rsqrt(var + LN_EPS) * g + b


def _ln_kernel(*refs, alpha, n_first):
    *x_refs, m_ref, g_ref, b_ref, o_ref, ob_ref = refs
    if n_first is None:
        x = x_refs[0][...]
    else:
        x = jnp.where(pl.program_id(0) < n_first, x_refs[0][...], x_refs[1][...])
    y = _residual_layernorm(x, m_ref[...], g_ref[...], b_ref[...], alpha)
    o_ref[...] = y
    ob_ref[...] = y.astype(BF16)


def _residual_ln(x, mix, g, b, alpha, tb=256):
    t, dm = mix.shape
    tb = min(tb, t)
    tok = pl.BlockSpec((tb, dm), lambda i: (i, 0))
    vec = pl.BlockSpec((1, dm), lambda i: (0, 0))
    if isinstance(x, tuple):
        xa, xb = x
        n_first = xa.shape[0] // tb
        assert xa.shape[0] % tb == 0 and xb.shape[0] % tb == 0
        n_last = xb.shape[0] // tb - 1
        x_args = [xa, xb]
        x_specs = [pl.BlockSpec((tb, dm), lambda i: (jnp.minimum(i, n_first - 1), 0)),
                   pl.BlockSpec((tb, dm), lambda i: (jnp.clip(i - n_first, 0, n_last), 0))]
    else:
        n_first, x_args, x_specs = None, [x], [tok]
    return pl.pallas_call(
        functools.partial(_ln_kernel, alpha=alpha, n_first=n_first),
        grid=(t // tb,),
        in_specs=x_specs + [tok, vec, vec],
        out_specs=[tok, tok],
        out_shape=[jax.ShapeDtypeStruct((t, dm), F32), jax.ShapeDtypeStruct((t, dm), BF16)],
        compiler_params=_params("parallel"),
        name="residual_ln",
    )(*x_args, mix, g.reshape(1, dm), b.reshape(1, dm))


def _ffn_up_kernel(x_ref, wg_ref, wu_ref, h_ref):
    x = x_ref[...]
    a = jnp.dot(x, wg_ref[...], preferred_element_type=F32)
    u = jnp.dot(x, wu_ref[...], preferred_element_type=F32)
    h_ref[...] = (a * jax.nn.sigmoid(a) * u).astype(h_ref.dtype)


def _ffn_up(xb, wg, wu, tm=1024, tn=512):
    t, dm = xb.shape
    f = wg.shape[1]
    tm = min(tm, t)
    wspec = pl.BlockSpec((dm, tn), lambda i, j: (0, j))
    return pl.pallas_call(
        _ffn_up_kernel,
        grid=(t // tm, f // tn),
        in_specs=[pl.BlockSpec((tm, dm), lambda i, j: (i, 0)), wspec, wspec],
        out_specs=pl.BlockSpec((tm, tn), lambda i, j: (i, j)),
        out_shape=jax.ShapeDtypeStruct((t, f), BF16),
        compiler_params=_params("parallel", "parallel"),
        name="ffn_up",
    )(xb, wg, wu)


def _top2(x, w):
    logits = jnp.dot(x, w, precision=HIGHEST, preferred_element_type=F32)
    lane = lax.broadcasted_iota(jnp.int32, logits.shape, 1)
    neg = jnp.float32(-jnp.inf)
    logits = jnp.where(lane < N_EXPERTS, logits, neg)
    m1 = jnp.max(logits, axis=1, keepdims=True)
    i1 = jnp.min(jnp.where(logits == m1, lane, LANES), axis=1, keepdims=True)
    rest = jnp.where(lane == i1, neg, logits)
    m2 = jnp.max(rest, axis=1, keepdims=True)
    i2 = jnp.min(jnp.where(rest == m2, lane, LANES), axis=1, keepdims=True)
    e2 = jnp.exp(m2 - m1)
    w1 = 1.0 / (1.0 + e2)
    w2 = e2 / (1.0 + e2)
    idx = jnp.where(lane == 0, i1, jnp.where(lane == 1, i2, 0))
    wts = jnp.where(lane == 0, w1, jnp.where(lane == 1, w2, 0.0))
    return idx, wts


def _ln_router_kernel(x_ref, m_ref, g_ref, b_ref, rw_ref, o_ref, ob_ref, idx_ref, wt_ref, *, alpha):
    y = _residual_layernorm(x_ref[...], m_ref[...], g_ref[...], b_ref[...], alpha)
    o_ref[...] = y
    ob_ref[...] = y.astype(BF16)
    idx_ref[...], wt_ref[...] = _top2(y, rw_ref[...])


def _residual_ln_router(x, mix, g, b, alpha, router_w, tb=256):
    t, dm = x.shape
    tb = min(tb, t)
    w = jnp.pad(router_w, ((0, 0), (0, LANES - router_w.shape[1])))
    tok = pl.BlockSpec((tb, dm), lambda i: (i, 0))
    vec = pl.BlockSpec((1, dm), lambda i: (0, 0))
    route = pl.BlockSpec((tb, LANES), lambda i: (i, 0))
    return pl.pallas_call(
        functools.partial(_ln_router_kernel, alpha=alpha),
        grid=(t // tb,),
        in_specs=[tok, tok, vec, vec, pl.BlockSpec((dm, LANES), lambda i: (0, 0))],
        out_specs=[tok, tok, route, route],
        out_shape=[jax.ShapeDtypeStruct((t, dm), F32), jax.ShapeDtypeStruct((t, dm), BF16),
                   jax.ShapeDtypeStruct((t, LANES), jnp.int32), jax.ShapeDtypeStruct((t, LANES), F32)],
        compiler_params=_params("parallel"),
        name="residual_ln_router",
    )(x, mix, g.reshape(1, dm), b.reshape(1, dm), w)


MOE_TM = 512
GATHER_UNROLL = 8


def _moe_dispatch(idx, wts, tm):
    t = idx.shape[0]
    n_rows = 2 * t + N_EXPERTS * tm
    n_tiles = n_rows // tm
    e_ids = idx[:, :2]
    per_tok = (e_ids[:, :, None] == jnp.arange(N_EXPERTS)).astype(jnp.int32).sum(axis=1)
    csum = jnp.cumsum(per_tok, axis=0)
    counts = csum[-1]
    rank = csum - per_tok
    padded = (counts + tm - 1) // tm * tm
    ends = jnp.cumsum(padded)
    offs = ends - padded
    pos = offs[e_ids] + jnp.take_along_axis(rank, e_ids, axis=1)
    code = jnp.full((n_rows,), -1, jnp.int32).at[pos.reshape(-1)].set(jnp.arange(2 * t, dtype=jnp.int32))
    src = jnp.maximum(code, 0) // 2
    row_w = jnp.where(code >= 0, wts[:, :2].reshape(-1)[jnp.maximum(code, 0)], 0.0)
    tile_start = jnp.arange(n_tiles, dtype=jnp.int32) * tm
    owner = jnp.sum(tile_start[:, None] >= ends[None, :], axis=1).astype(jnp.int32)
    tile_valid = (owner < N_EXPERTS).astype(jnp.int32)
    tile_expert = jnp.minimum(owner, N_EXPERTS - 1)
    return src, row_w, pos.astype(jnp.int32), tile_expert, tile_valid


def _start_row_gather(idx_ref, n, src_hbm, dst_ref, sem):
    def issue(r, carry):
        pltpu.make_async_copy(src_hbm.at[pl.ds(idx_ref[0, r], 1), :], dst_ref.at[pl.ds(r, 1), :], sem).start()
        return carry

    lax.fori_loop(0, n, issue, 0, unroll=GATHER_UNROLL)


def _wait_row_gather(n, src_hbm, dst_ref, sem):
    pltpu.make_async_copy(src_hbm.at[pl.ds(0, n), :], dst_ref.at[pl.ds(0, n), :], sem).wait()


def _gather_rows(idx_ref, n, src_hbm, dst_ref, sem):
    _start_row_gather(idx_ref, n, src_hbm, dst_ref, sem)
    _wait_row_gather(n, src_hbm, dst_ref, sem)


def _moe_up_sparse_kernel(te_ref, tv_ref, src_ref, nxt_ref, x_hbm, wg_ref, wu_ref, roww_ref, h_ref, xbuf, xb_buf,
                          sem, *, tm, n_tiles):
    i = pl.program_id(0)
    valid = tv_ref[i] == 1
    first_col = pl.program_id(1) == 0

    @pl.when(jnp.logical_and(first_col, jnp.logical_and(valid, i == 0)))
    def _():
        _start_row_gather(src_ref, tm, x_hbm, xbuf, sem)

    @pl.when(jnp.logical_and(first_col, valid))
    def _():
        _wait_row_gather(tm, x_hbm, xbuf, sem)
        xb_buf[...] = xbuf[...].astype(BF16)

    nxt = jnp.minimum(i + 1, n_tiles - 1)

    @pl.when(jnp.logical_and(first_col, jnp.logical_and(i + 1 < n_tiles, tv_ref[nxt] == 1)))
    def _():
        _start_row_gather(nxt_ref, tm, x_hbm, xbuf, sem)

    @pl.when(valid)
    def _():
        x = xb_buf[...]
        a = jnp.dot(x, wg_ref[...], preferred_element_type=F32)
        u = jnp.dot(x, wu_ref[...], preferred_element_type=F32)
        h_ref[...] = (a * jax.nn.sigmoid(a) * u * roww_ref[...]).astype(h_ref.dtype)

    @pl.when(jnp.logical_not(valid))
    def _():
        h_ref[...] = jnp.zeros_like(h_ref)


def _moe_up_sparse(x, we_gate, we_up, src, row_w, tile_expert, tile_valid, tm, tn=512):
    t, dm = x.shape
    n_e, _, fe = we_gate.shape
    n_rows = src.shape[0]
    n_tiles = n_rows // tm
    wspec = pl.BlockSpec((None, dm, tn), lambda i, j, te, tv: (te[i], 0, j))
    grid_spec = pltpu.PrefetchScalarGridSpec(
        num_scalar_prefetch=2,
        grid=(n_tiles, fe // tn),
        in_specs=[pl.BlockSpec((None, 1, tm), lambda i, j, te, tv: (i, 0, 0), memory_space=pltpu.SMEM),
                  pl.BlockSpec((None, 1, tm), lambda i, j, te, tv: (jnp.minimum(i + 1, n_tiles - 1), 0, 0),
                               memory_space=pltpu.SMEM),
                  pl.BlockSpec(memory_space=pl.ANY),
                  wspec, wspec,
                  pl.BlockSpec((tm, 1), lambda i, j, te, tv: (i, 0))],
        out_specs=pl.BlockSpec((tm, tn), lambda i, j, te, tv: (i, j)),
        scratch_shapes=[pltpu.VMEM((tm, dm), F32), pltpu.VMEM((tm, dm), BF16), pltpu.SemaphoreType.DMA(())],
    )
    return pl.pallas_call(
        functools.partial(_moe_up_sparse_kernel, tm=tm, n_tiles=n_tiles),
        grid_spec=grid_spec,
        out_shape=jax.ShapeDtypeStruct((n_rows, fe), BF16),
        compiler_params=_params("arbitrary", "arbitrary"),
        name="moe_up_sparse",
    )(tile_expert, tile_valid, src.reshape(n_tiles, 1, tm), src.reshape(n_tiles, 1, tm), x, we_gate, we_up,
      row_w.reshape(n_rows, 1))


def _moe_down_kernel(te_ref, tv_ref, h_ref, w_ref, y_ref):
    valid = tv_ref[pl.program_id(1)] == 1

    @pl.when(valid)
    def _():
        y_ref[...] = jnp.dot(h_ref[...], w_ref[...], preferred_element_type=F32)

    @pl.when(jnp.logical_not(valid))
    def _():
        y_ref[...] = jnp.zeros_like(y_ref)


def _moe_down_sparse(h, we_down, tile_expert, tile_valid, tm, tn=1024):
    n_rows, fe = h.shape
    dm = we_down.shape[2]
    grid_spec = pltpu.PrefetchScalarGridSpec(
        num_scalar_prefetch=2,
        grid=(dm // tn, n_rows // tm),
        in_specs=[pl.BlockSpec((tm, fe), lambda j, i, te, tv: (i, 0)),
                  pl.BlockSpec((None, fe, tn), lambda j, i, te, tv: (te[i], 0, j))],
        out_specs=pl.BlockSpec((tm, tn), lambda j, i, te, tv: (i, j)),
    )
    return pl.pallas_call(
        _moe_down_kernel,
        grid_spec=grid_spec,
        out_shape=jax.ShapeDtypeStruct((n_rows, dm), F32),
        compiler_params=_params("parallel", "parallel"),
        name="moe_down_sparse",
    )(tile_expert, tile_valid, h, we_down)


def _moe_combine_ln_kernel(pos_ref, nxt_ref, y_hbm, x_ref, g_ref, b_ref, o_ref, ob_ref, ybuf, sem, *, alpha, tb,
                           n_steps):
    i = pl.program_id(0)
    slot = i % 2

    @pl.when(i == 0)
    def _():
        _start_row_gather(pos_ref, 2 * tb, y_hbm, ybuf.at[0], sem.at[0])

    @pl.when(i + 1 < n_steps)
    def _():
        _start_row_gather(nxt_ref, 2 * tb, y_hbm, ybuf.at[1 - slot], sem.at[1 - slot])

    _wait_row_gather(2 * tb, y_hbm, ybuf.at[slot], sem.at[slot])
    rows = ybuf.at[slot]
    y = _residual_layernorm(x_ref[...], rows[0:tb, :] + rows[tb:2 * tb, :], g_ref[...], b_ref[...], alpha)
    o_ref[...] = y
    ob_ref[...] = y.astype(BF16)


def _moe_combine_ln(x, ys, pos, g, b, alpha, tb=256):
    t, dm = x.shape
    tb = min(tb, t)
    tok = pl.BlockSpec((tb, dm), lambda i: (i, 0))
    vec = pl.BlockSpec((1, dm), lambda i: (0, 0))
    n_steps = t // tb
    rows = jnp.concatenate([pos[:, 0].reshape(n_steps, 1, tb), pos[:, 1].reshape(n_steps, 1, tb)], axis=2)
    return pl.pallas_call(
        functools.partial(_moe_combine_ln_kernel, alpha=alpha, tb=tb, n_steps=n_steps),
        grid=(n_steps,),
        in_specs=[pl.BlockSpec((None, 1, 2 * tb), lambda i: (i, 0, 0), memory_space=pltpu.SMEM),
                  pl.BlockSpec((None, 1, 2 * tb), lambda i: (jnp.minimum(i + 1, n_steps - 1), 0, 0),
                               memory_space=pltpu.SMEM),
                  pl.BlockSpec(memory_space=pl.ANY), tok, vec, vec],
        out_specs=[tok, tok],
        out_shape=[jax.ShapeDtypeStruct((t, dm), F32), jax.ShapeDtypeStruct((t, dm), BF16)],
        scratch_shapes=[pltpu.VMEM((2, 2 * tb, dm), F32), pltpu.SemaphoreType.DMA((2,))],
        compiler_params=_params("arbitrary"),
        name="moe_combine_ln",
    )(rows, rows, ys, x, g.reshape(1, dm), b.reshape(1, dm))


def _ple_kernel(xb_ref, wg_ref, p_ref, wp_ref, x_ref, o_ref, *maybe_ob_ref):
    gate = jax.nn.sigmoid(jnp.dot(xb_ref[...], wg_ref[...], preferred_element_type=F32))
    emb = jnp.dot(p_ref[...], wp_ref[...], preferred_element_type=F32)
    y = x_ref[...] + gate * emb
    o_ref[...] = y
    for ob_ref in maybe_ob_ref:
        ob_ref[...] = y.astype(BF16)


def _ple(xb, x, w_pgate, p, w_pproj, rows=None, tm=1024, tn=512):
    t, dm = xb.shape
    pd = p.shape[1]
    start, count = (0, t) if rows is None else rows
    tm = min(tm, count)
    assert start % tm == 0 and count % tm == 0
    off = start // tm
    out = pl.BlockSpec((tm, tn), lambda i, j: (i, j))
    outs = [jax.ShapeDtypeStruct((count, dm), F32)] + ([jax.ShapeDtypeStruct((count, dm), BF16)] if rows is None else [])
    return pl.pallas_call(
        _ple_kernel,
        grid=(count // tm, dm // tn),
        in_specs=[pl.BlockSpec((tm, dm), lambda i, j: (i + off, 0)),
                  pl.BlockSpec((dm, tn), lambda i, j: (0, j)),
                  pl.BlockSpec((tm, pd), lambda i, j: (i + off, 0)),
                  pl.BlockSpec((pd, tn), lambda i, j: (0, j)),
                  pl.BlockSpec((tm, tn), lambda i, j: (i + off, j))],
        out_specs=[out] * len(outs),
        out_shape=outs,
        compiler_params=_params("parallel", "parallel"),
        name="ple_gate",
    )(xb, w_pgate, p, w_pproj, x)


def _pad_cols(w, width):
    return jnp.pad(w, ((0, 0), (0, width - w.shape[1])))


def _pad_rows(w, rows):
    pad = [(0, 0)] * w.ndim
    pad[-2] = (0, rows - w.shape[-2])
    return jnp.pad(w, pad)


def _split_rwkv_cols(w, dg, dr, has_mv):
    base = 2 * dg
    rkv = w[:, base:base + 3 * dr]
    o = base + 3 * dr
    sizes = [DECAY_LORA, DECAY_LORA, AAA_LORA, AAA_LORA, GATE_LORA] + ([MV_LORA] if has_mv else [])
    slots = [LORA_SLOT, LORA_SLOT, LORA_SLOT, LORA_SLOT, GATE_LORA, LORA_SLOT]
    parts = []
    for n, slot in zip(sizes, slots):
        parts.append(_pad_cols(w[:, o:o + n], slot))
        o += n
    lora = _pad_cols(jnp.concatenate(parts, axis=1), LORA_COLS)
    return jnp.concatenate([rkv, lora], axis=1)


def _trunk(x, p, P, segs):
    (t1, _), (t2, _) = segs
    t = t1 + t2
    depth = P['w_out'].shape[0]
    alpha = (2 * depth) ** 0.25
    dg = P['sgu_ln_g'].shape[1]
    dr = P['k_k'].shape[1]
    hb = lax.broadcasted_iota(jnp.int32, (SCAN_W, SCAN_W), 0) // RWKV_HEAD
    hc = lax.broadcasted_iota(jnp.int32, (SCAN_W, SCAN_W), 1) // RWKV_HEAD
    ones_bd = (hb == hc).astype(BF16)
    xb = jnp.concatenate([part.astype(BF16) for part in x], axis=0)
    w_out_b = _to_bf16(P['w_out'])
    w_pgate_b = _to_bf16(P['w_pgate'])
    v_first = None
    for l in range(depth):
        if l == 0:
            w_in_l, conv_l, v0_l, v2_l = P['w_in0'], P['conv0'], None, None
        else:
            w_in_l, conv_l, v0_l, v2_l = P['w_in'][l - 1], P['conv'][l - 1], P['v0'][l - 1], P['v2'][l - 1]
        w_in_b = _to_bf16(w_in_l)
        uv = _matmul(xb, w_in_b[:, :2 * dg], BF16)
        zr = _matmul(xb, _split_rwkv_cols(w_in_b, dg, dr, l > 0), BF16)
        y_g = _spatial_gating(uv, P['sgu_ln_g'][l], P['sgu_ln_b'][l], P['w_s'][l], P['b_s'][l])
        conv_r = _split_rwkv_cols(jnp.pad(conv_l, ((0, 0), (2 * dg, 0))), dg, dr, l > 0)
        prep_args = (zr, conv_r, P['w0'][l], _pad_rows(P['w2'][l], LORA_SLOT).astype(BF16), P['a0'][l],
                     _pad_rows(P['a2'][l], LORA_SLOT).astype(BF16), P['g2'][l].astype(BF16),
                     P['k_k'][l], P['k_a'][l], P['r_k'][l].reshape(-1), ones_bd, segs)
        if l == 0:
            r, v, kk, g, bonus, lw, k, b = _rwkv_prep(*prep_args)
            v_first = v
        else:
            r, v, kk, g, bonus, lw, k, b = _rwkv_prep(*prep_args, v_first=v_first, v0=v0_l,
                                                      v2=_pad_rows(v2_l, LORA_SLOT).astype(BF16))
        y_f, y_b = _rwkv_scan(r, v, kk, lw, k, b, segs)
        y_r = _rwkv_post(y_f, y_b, g, bonus, P['gn_g'][l], P['gn_b'][l], ones_bd)
        w_out = w_out_b[l]
        mix = _matmul_pair(y_g, w_out[:dg], y_r, w_out[dg:], BF16)
        j = l // 2
        if l % 2 == 0:
            x, xb = _residual_ln(x, mix, P['ln1_g'][l], P['ln1_b'][l], alpha)
            h = _ffn_up(xb, _to_bf16(P['w_ff_gate'][j]), _to_bf16(P['w_ff_up'][j]))
            ff = _matmul(h, _to_bf16(P['w_ff_down'][j]), BF16, tk=h.shape[1] // 4)
        else:
            if isinstance(x, tuple):
                x = jnp.concatenate(x, axis=0)
            x, xb, idx, wts = _residual_ln_router(x, mix, P['ln1_g'][l], P['ln1_b'][l], alpha, P['router'][j])
            tm = min(MOE_TM, t)
            src, row_w, pos, tile_expert, tile_valid = _moe_dispatch(idx, wts, tm)
            h = _moe_up_sparse(x, _to_bf16(P['we_gate'][j]), _to_bf16(P['we_up'][j]), src, row_w,
                               tile_expert, tile_valid, tm)
            ys = _moe_down_sparse(h, _to_bf16(P['we_down'][j]), tile_expert, tile_valid, tm)
            x, xb = _moe_combine_ln(x, ys, pos, P['ln2_g'][l], P['ln2_b'][l], alpha)
        if l % 2 == 0:
            x, xb = _residual_ln(x, ff, P['ln2_g'][l], P['ln2_b'][l], alpha)
        ple_args = (xb, x, w_pgate_b[l], p[l].astype(BF16), P['w_pproj'][l].astype(BF16))
        if l < depth - 1:
            x, xb = _ple(*ple_args)
    (y1,), (y2,) = _ple(*ple_args, rows=(0, t1)), _ple(*ple_args, rows=(t1, t2))
    return y1, y2


def kernel(x_prompt, x_sample, p_prompt, p_sample, w_in0, conv0, w_in, conv, sgu_ln_g, sgu_ln_b, w_s, b_s, w0, w2, a0, a2, g2, k_k, k_a, r_k, gn_g, gn_b, v0, v2, w_out, ln1_g, ln1_b, ln2_g, ln2_b, w_ff_gate, w_ff_up, w_ff_down, router, we_gate, we_up, we_down, w_pproj, w_pgate):
    P = dict(w_in0=w_in0, conv0=conv0, w_in=w_in, conv=conv, sgu_ln_g=sgu_ln_g, sgu_ln_b=sgu_ln_b,
             w_s=w_s, b_s=b_s, w0=w0, w2=w2, a0=a0, a2=a2, g2=g2, k_k=k_k, k_a=k_a, r_k=r_k,
             gn_g=gn_g, gn_b=gn_b, v0=v0, v2=v2, w_out=w_out, ln1_g=ln1_g, ln1_b=ln1_b,
             ln2_g=ln2_g, ln2_b=ln2_b, w_ff_gate=w_ff_gate, w_ff_up=w_ff_up, w_ff_down=w_ff_down,
             router=router, we_gate=we_gate, we_up=we_up, we_down=we_down,
             w_pproj=w_pproj, w_pgate=w_pgate)
    b1, s1, dm = x_prompt.shape
    b2, s2, _ = x_sample.shape
    depth, pd = p_prompt.shape[0], p_prompt.shape[-1]
    t1, t2 = b1 * s1, b2 * s2
    segs = ((t1, s1), (t2, s2))
    p = jnp.concatenate([p_prompt.reshape(depth, t1, pd), p_sample.reshape(depth, t2, pd)], axis=1)
    y1, y2 = _trunk((x_prompt.reshape(t1, dm), x_sample.reshape(t2, dm)), p, P, segs)
    return y1.reshape(b1, s1, dm), y2.reshape(b2, s2, dm)
```

```python
import functools

import jax
import jax.numpy as jnp
from jax import lax
from jax.experimental import pallas as pl
from jax.experimental.pallas import tpu as pltpu

F32 = jnp.float32
BF16 = jnp.bfloat16
HIGHEST = lax.Precision.HIGHEST

LANES = 128
BF16_SUBLANES = 16
VMEM_LIMIT_BYTES = 56 * 1024 * 1024

CHUNK = 128
GMLP_HEAD = 128
RWKV_HEAD = 64
DECAY_LORA = 96
AAA_LORA = 96
MV_LORA = 64
GATE_LORA = 256
N_EXPERTS = 8
LN_EPS = 1e-5
GN_EPS = 64e-5
L2_EPS = 1e-12
DECAY_SCALE = 0.6065306597126334

SCAN_L = 64
SCAN_W = 256
HEADS_PER_GROUP = SCAN_W // RWKV_HEAD
LORA_SLOT = 128
LORA_COLS = 1024


def _params(*sem):
    return pltpu.CompilerParams(dimension_semantics=sem, vmem_limit_bytes=VMEM_LIMIT_BYTES)


def _is_seq_start(pos, segs):
    (t1, s1), (_, s2) = segs
    return jnp.where(pos < t1, pos % s1 == 0, (pos - t1) % s2 == 0)


CAST_BLOCK_BYTES = 8 * 1024 * 1024


def _cast_kernel(x_ref, o_ref):
    o_ref[...] = x_ref[...].astype(o_ref.dtype)


def _to_bf16(w):
    w2 = w.reshape(-1, w.shape[-1])
    rows, n = w2.shape
    tb = rows
    while tb * n * w2.dtype.itemsize > CAST_BLOCK_BYTES and tb % 2 == 0 and (tb // 2) % BF16_SUBLANES == 0:
        tb //= 2
    out = pl.pallas_call(
        _cast_kernel,
        grid=(rows // tb,),
        in_specs=[pl.BlockSpec((tb, n), lambda i: (i, 0))],
        out_specs=pl.BlockSpec((tb, n), lambda i: (i, 0)),
        out_shape=jax.ShapeDtypeStruct((rows, n), BF16),
        compiler_params=_params("parallel"),
        name="cast_bf16",
    )(w2)
    return out.reshape(w.shape)


def _mm_kernel(x_ref, w_ref, o_ref, *scratch, nk):
    if nk == 1:
        o_ref[...] = jnp.dot(x_ref[...], w_ref[...], preferred_element_type=F32).astype(o_ref.dtype)
        return
    acc_ref, = scratch
    k = pl.program_id(2)

    @pl.when(k == 0)
    def _():
        acc_ref[...] = jnp.zeros_like(acc_ref)

    acc_ref[...] += jnp.dot(x_ref[...], w_ref[...], preferred_element_type=F32)

    @pl.when(k == nk - 1)
    def _():
        o_ref[...] = acc_ref[...].astype(o_ref.dtype)


def _matmul(x, w, out_dtype, tm=1024, tn=1024, tk=None):
    m, kdim = x.shape
    n = w.shape[1]
    tm, tn = min(tm, m), min(tn, n)
    tk = kdim if tk is None else min(tk, kdim)
    nk = kdim // tk
    assert m % tm == 0 and n % tn == 0 and kdim % tk == 0
    scratch = [] if nk == 1 else [pltpu.VMEM((tm, tn), F32)]
    return pl.pallas_call(
        functools.partial(_mm_kernel, nk=nk),
        grid=(m // tm, n // tn, nk),
        in_specs=[pl.BlockSpec((tm, tk), lambda i, j, k: (i, k)),
                  pl.BlockSpec((tk, tn), lambda i, j, k: (k, j))],
        out_specs=pl.BlockSpec((tm, tn), lambda i, j, k: (i, j)),
        out_shape=jax.ShapeDtypeStruct((m, n), out_dtype),
        scratch_shapes=scratch,
        compiler_params=_params("parallel", "parallel", "arbitrary"),
        name="matmul",
    )(x, w)


def _mm_pair_kernel(xa_ref, wa_ref, xb_ref, wb_ref, o_ref):
    acc = jnp.dot(xa_ref[...], wa_ref[...], preferred_element_type=F32)
    acc += jnp.dot(xb_ref[...], wb_ref[...], preferred_element_type=F32)
    o_ref[...] = acc.astype(o_ref.dtype)


def _matmul_pair(xa, wa, xb, wb, out_dtype, tm=1024, tn=1024):
    m, ka = xa.shape
    kb = xb.shape[1]
    n = wa.shape[1]
    tm, tn = min(tm, m), min(tn, n)
    return pl.pallas_call(
        _mm_pair_kernel,
        grid=(m // tm, n // tn),
        in_specs=[pl.BlockSpec((tm, ka), lambda i, j: (i, 0)),
                  pl.BlockSpec((ka, tn), lambda i, j: (0, j)),
                  pl.BlockSpec((tm, kb), lambda i, j: (i, 0)),
                  pl.BlockSpec((kb, tn), lambda i, j: (0, j))],
        out_specs=pl.BlockSpec((tm, tn), lambda i, j: (i, j)),
        out_shape=jax.ShapeDtypeStruct((m, n), out_dtype),
        compiler_params=_params("parallel", "parallel"),
        name="matmul_pair",
    )(xa, wa, xb, wb)


def _sgu_kernel(u_ref, v_ref, g_ref, b_ref, ws_ref, bs_ref, o_ref, *, n_chunks):
    gv = jax.nn.gelu(v_ref[...].astype(F32))
    mu = jnp.mean(gv, axis=-1, keepdims=True)
    cen = gv - mu
    var = jnp.mean(cen * cen, axis=-1, keepdims=True)
    vn = (cen * lax.rsqrt(var + LN_EPS) * g_ref[...] + b_ref[...]).astype(BF16)
    n_heads = ws_ref.shape[0]
    for c in range(n_chunks):
        rows = slice(c * CHUNK, (c + 1) * CHUNK)
        for h in range(n_heads):
            cols = slice(h * GMLP_HEAD, (h + 1) * GMLP_HEAD)
            sv = jnp.dot(ws_ref[h], vn[rows, cols], preferred_element_type=F32) + bs_ref[:, cols]
            gu = jax.nn.gelu(u_ref[rows, cols].astype(F32))
            o_ref[rows, cols] = (gu * sv).astype(o_ref.dtype)


def _spatial_gating(uv, ln_g, ln_b, w_s, b_s, tb=512):
    t = uv.shape[0]
    dg = uv.shape[1] // 2
    tb = min(tb, t)
    n_heads = w_s.shape[0]
    bs_full = jnp.repeat(b_s.T, GMLP_HEAD, axis=1)
    return pl.pallas_call(
        functools.partial(_sgu_kernel, n_chunks=tb // CHUNK),
        grid=(t // tb,),
        in_specs=[pl.BlockSpec((tb, dg), lambda i: (i, 0)),
                  pl.BlockSpec((tb, dg), lambda i: (i, 1)),
                  pl.BlockSpec((1, dg), lambda i: (0, 0)),
                  pl.BlockSpec((1, dg), lambda i: (0, 0)),
                  pl.BlockSpec((n_heads, CHUNK, CHUNK), lambda i: (0, 0, 0)),
                  pl.BlockSpec((CHUNK, dg), lambda i: (0, 0))],
        out_specs=pl.BlockSpec((tb, dg), lambda i: (i, 0)),
        out_shape=jax.ShapeDtypeStruct((t, dg), BF16),
        compiler_params=_params("parallel"),
        name="spatial_gating",
    )(uv, uv, ln_g.reshape(1, dg), ln_b.reshape(1, dg), w_s.astype(BF16), bs_full)


def _head_sum(x, ones_bd):
    parts = []
    for g in range(x.shape[1] // SCAN_W):
        xs = x[:, g * SCAN_W:(g + 1) * SCAN_W]
        hi = xs.astype(BF16)
        lo = (xs - hi.astype(F32)).astype(BF16)
        parts.append(jnp.dot(hi, ones_bd, preferred_element_type=F32)
                     + jnp.dot(lo, ones_bd, preferred_element_type=F32))
    return jnp.concatenate(parts, axis=1)


def _prep_kernel(*refs, layer1, tb, segs, dr):
    if layer1:
        (z_ref, zp_ref, zn_ref, conv_ref, w0_ref, w2_ref, a0_ref, a2_ref, g2_ref, kk_ref, ka_ref, rk_ref,
         ones_ref, vf_ref, v0_ref, v2_ref, r_o, v_o, kkn_o, g_o, bonus_o, lw_o, k_o, b_o) = refs
    else:
        (z_ref, zp_ref, zn_ref, conv_ref, w0_ref, w2_ref, a0_ref, a2_ref, g2_ref, kk_ref, ka_ref, rk_ref,
         ones_ref, r_o, v_o, kkn_o, g_o, bonus_o, lw_o, k_o, b_o) = refs
    pos0 = pl.program_id(0) * tb
    first = _is_seq_start(pos0, segs)
    last = _is_seq_start(pos0 + tb, segs)
    row = lax.broadcasted_iota(jnp.int32, (tb, 1), 0)

    def conv(c0, width):
        cols = slice(c0, c0 + width)
        z = z_ref[:, cols].astype(F32)
        prev_row = jnp.where(first, 0.0, zp_ref[:, cols].astype(F32)[BF16_SUBLANES - 1:BF16_SUBLANES])
        next_row = jnp.where(last, 0.0, zn_ref[:, cols].astype(F32)[0:1])
        zprev = jnp.where(row == 0, prev_row, pltpu.roll(z, 1, 0))
        znext = jnp.where(row == tb - 1, next_row, pltpu.roll(z, tb - 1, 0))
        return zprev * conv_ref[0:1, cols] + z * conv_ref[1:2, cols] + znext * conv_ref[2:3, cols]

    lo = conv(3 * dr, LORA_COLS)
    s = LORA_SLOT
    wd = (lo[:, 0:s], lo[:, s:2 * s])
    ad = (lo[:, 2 * s:3 * s], lo[:, 3 * s:4 * s])
    gd = lo[:, 4 * s:6 * s]

    def lora(x, w):
        return jnp.dot(x.astype(BF16), w, preferred_element_type=F32)

    g_o[...] = lora(jax.nn.sigmoid(gd), g2_ref[...]).astype(g_o.dtype)
    v = conv(2 * dr, dr)
    if layer1:
        mv = lo[:, 6 * s:7 * s]
        v = v + (vf_ref[...].astype(F32) - v) * jax.nn.sigmoid(v0_ref[...] + lora(mv, v2_ref[...]))
    v_o[...] = v.astype(v_o.dtype)
    r = conv(0, dr)
    r_o[...] = r.astype(r_o.dtype)
    k = conv(dr, dr)
    ones_bd = ones_ref[...]
    kk = k * kk_ref[...]
    kk = kk * lax.rsqrt(_head_sum(kk * kk, ones_bd) + L2_EPS)
    kkn_o[...] = kk.astype(kkn_o.dtype)
    for d in range(2):
        zw = w0_ref[d:d + 1, :] + lora(jnp.tanh(wd[d]), w2_ref[d])
        lw_o[d] = -DECAY_SCALE / (1.0 + jnp.exp(-zw))
        a = jax.nn.sigmoid(a0_ref[d:d + 1, :] + lora(ad[d], a2_ref[d]))
        k_d = k * (1.0 + (a - 1.0) * ka_ref[...])
        k_o[d] = k_d.astype(k_o.dtype)
        b_o[d] = (kk * a).astype(b_o.dtype)
        if d == 0:
            bonus_o[...] = (_head_sum(r * k_d * rk_ref[...], ones_bd) * v).astype(bonus_o.dtype)


def _rwkv_prep(zr, conv_w, w0, w2, a0, a2, g2, k_k, k_a, r_k, ones_bd, segs, v_first=None, v0=None, v2=None,
               tb=128):
    t, c = zr.shape
    dr = k_k.shape[-1]
    tb = min(tb, t)
    layer1 = v_first is not None
    hb = tb // BF16_SUBLANES
    n_halo = t // BF16_SUBLANES
    const2 = lambda i: (0, 0)
    const3 = lambda i: (0, 0, 0)
    in_specs = [pl.BlockSpec((tb, c), lambda i: (i, 0)),
                pl.BlockSpec((BF16_SUBLANES, c), lambda i: (jnp.maximum(i * hb - 1, 0), 0)),
                pl.BlockSpec((BF16_SUBLANES, c), lambda i: (jnp.minimum((i + 1) * hb, n_halo - 1), 0)),
                pl.BlockSpec((3, c), const2),
                pl.BlockSpec((2, dr), const2),
                pl.BlockSpec((2, LORA_SLOT, dr), const3),
                pl.BlockSpec((2, dr), const2),
                pl.BlockSpec((2, LORA_SLOT, dr), const3),
                pl.BlockSpec((GATE_LORA, dr), const2),
                pl.BlockSpec((1, dr), const2),
                pl.BlockSpec((1, dr), const2),
                pl.BlockSpec((1, dr), const2),
                pl.BlockSpec((SCAN_W, SCAN_W), const2)]
    args = [zr, zr, zr, conv_w, w0, w2, a0, a2, g2, k_k.reshape(1, dr), k_a.reshape(1, dr), r_k.reshape(1, dr),
            ones_bd]
    if layer1:
        in_specs += [pl.BlockSpec((tb, dr), lambda i: (i, 0)),
                     pl.BlockSpec((1, dr), const2),
                     pl.BlockSpec((LORA_SLOT, dr), const2)]
        args += [v_first, v0.reshape(1, dr), v2]
    tok = pl.BlockSpec((tb, dr), lambda i: (i, 0))
    tok2 = pl.BlockSpec((2, tb, dr), lambda i: (0, i, 0))
    one = jax.ShapeDtypeStruct((t, dr), BF16)
    two = jax.ShapeDtypeStruct((2, t, dr), BF16)
    log_decay = jax.ShapeDtypeStruct((2, t, dr), F32)
    return pl.pallas_call(
        functools.partial(_prep_kernel, layer1=layer1, tb=tb, segs=segs, dr=dr),
        grid=(t // tb,),
        in_specs=in_specs,
        out_specs=[tok, tok, tok, tok, tok, tok2, tok2, tok2],
        out_shape=[one, one, one, one, one, log_decay, two, two],
        compiler_params=_params("parallel"),
        name="rwkv_prep",
    )(*args)


def _scan_consts(reverse):
    L, W, G = SCAN_L, SCAN_W, HEADS_PER_GROUP
    sgn = -1 if reverse else 1
    t_i = lax.broadcasted_iota(jnp.int32, (L, 3 * L), 0)
    s_i = lax.broadcasted_iota(jnp.int32, (L, 3 * L), 1) % L
    tri3 = jnp.where(sgn * (t_i - s_i) >= 0, 1.0, 0.0).astype(BF16)
    tc = lax.broadcasted_iota(jnp.int32, (L, G * L), 0)
    sc = lax.broadcasted_iota(jnp.int32, (L, G * L), 1) % L
    before = sgn * (tc - sc)
    strict = before > 0
    rmask = before > 0 if reverse else before >= 0
    eye = jnp.where(sc == tc, 1.0, 0.0).astype(F32)
    bi = lax.broadcasted_iota(jnp.int32, (W, W), 0) // RWKV_HEAD
    bj = lax.broadcasted_iota(jnp.int32, (W, W), 1) // RWKV_HEAD
    return tri3, strict, rmask, eye, bi == bj


def _bd(x):
    per_tile = LANES // RWKV_HEAD
    n_tiles = x.shape[1] // LANES
    lane_head = lax.broadcasted_iota(jnp.int32, (x.shape[0], LANES), 1) // RWKV_HEAD
    zero = jnp.zeros((x.shape[0], LANES), x.dtype)
    blocks = []
    for h in range(HEADS_PER_GROUP):
        own = h // per_tile
        kept = jnp.where(lane_head == h % per_tile, x[:, own * LANES:(own + 1) * LANES], zero)
        blocks.append(jnp.concatenate([kept if tile == own else zero for tile in range(n_tiles)], axis=1))
    return jnp.concatenate(blocks, axis=0)


_NT = (((1,), (1,)), ((), ()))
_TN = (((0,), (0,)), ((), ()))


def _chunk_operators(r, v, kk, lw, k, b, consts, reverse):
    L, W, G = SCAN_L, SCAN_W, HEADS_PER_GROUP
    tri3, strict, rmask, eye, _ = consts
    bd = _bd

    def dot(a_, b_):
        return jnp.dot(a_, b_, preferred_element_type=F32)

    hi = lw.astype(BF16)
    rem = lw - hi.astype(F32)
    mid = rem.astype(BF16)
    low = (rem - mid.astype(F32)).astype(BF16)
    cum = dot(tri3, jnp.concatenate([hi, mid, low], axis=0))
    yield
    tot = jnp.sum(lw, axis=0, keepdims=True)
    e_ex = jnp.exp(cum - lw)
    e_neg = jnp.exp(-cum)
    e_rem = jnp.exp(tot - cum)
    kq = kk * e_ex
    rq = r * (e_ex if reverse else jnp.exp(cum))
    vb = v.astype(BF16)
    q = jnp.concatenate([kq, rq], axis=0).astype(BF16)
    keys = jnp.concatenate([bd((k * e_neg).astype(BF16)), bd((b * e_neg).astype(BF16))], axis=0)
    a = lax.dot_general(q, keys, _NT, preferred_element_type=F32)
    yield
    a_kk = jnp.where(strict, a[:L, :G * L], 0.0)
    a_kb = jnp.where(strict, a[:L, G * L:], 0.0)
    a_rk = jnp.where(rmask, a[L:, :G * L], 0.0)
    a_rb = jnp.where(rmask, a[L:, G * L:], 0.0).astype(BF16)
    p = -a_kb
    tinv = eye + p
    av = dot(jnp.concatenate([a_kk, a_rk], axis=0).astype(BF16), bd(vb))
    p = dot(p.astype(BF16), bd(p.astype(BF16)))
    yield
    levels = L.bit_length() - 2
    for lvl in range(levels):
        wt = bd(p.astype(BF16))
        if lvl < levels - 1:
            tp = dot(jnp.concatenate([tinv, p], axis=0).astype(BF16), wt)
            tinv = tinv + tp[:L]
            p = tp[L:]
        else:
            tinv = tinv + dot(tinv.astype(BF16), wt)
        yield
    tinv = tinv.astype(BF16)
    tk = dot(tinv, jnp.concatenate([bd(kq.astype(BF16)), bd(av[:L].astype(BF16))], axis=1))
    yield
    kq2 = tk[:, :W]
    cu = tk[:, W:]
    ar = dot(a_rb, jnp.concatenate([bd(kq2.astype(BF16)), bd(cu.astype(BF16))], axis=1))
    yield
    rq2 = rq - ar[:, :W]
    yc = av[L:] - ar[:, W:]
    qs = jnp.concatenate([rq2, kq2], axis=0).astype(BF16)
    kb = jnp.concatenate([k * e_rem, b * e_rem], axis=0).astype(BF16)
    return qs, yc, cu, jnp.exp(tot), kb, vb


def _run_in_lockstep(gens):
    results = [None] * len(gens)
    active = list(enumerate(gens))
    while active:
        still = []
        for i, gen in active:
            try:
                next(gen)
                still.append((i, gen))
            except StopIteration as done:
                results[i] = done.value
        active = still
    return results


def _scan_kernel(rf_ref, vf_ref, kkf_ref, lwf_ref, kf_ref, bf_ref, rb_ref, vb_ref, kkb_ref, lwb_ref, kb_ref,
                 bb_ref, yf_ref, yb_ref, s_ref, *, tb, segs, nblk, ng):
    L, W = SCAN_L, SCAN_W
    c = pl.program_id(1)
    start_f = _is_seq_start(c * tb, segs)
    start_b = _is_seq_start((nblk - c) * tb, segs)

    @pl.when(start_f)
    def _():
        s_ref[0] = jnp.zeros(s_ref.shape[1:], F32)

    @pl.when(start_b)
    def _():
        s_ref[1] = jnp.zeros(s_ref.shape[1:], F32)

    n_chunks = tb // L
    dirs = ((rf_ref, vf_ref, kkf_ref, lwf_ref, kf_ref, bf_ref, yf_ref),
            (rb_ref, vb_ref, kkb_ref, lwb_ref, kb_ref, bb_ref, yb_ref))
    streams = [(d, g) for d in range(2) for g in range(ng)]
    order = {d: [n_chunks - 1 - j if d == 1 else j for j in range(n_chunks)] for d in range(2)}
    consts = {d: _scan_consts(reverse=d == 1) for d in range(2)}
    gens = []
    for d, g in streams:
        r_ref, v_ref, kk_ref, lw_ref, k_ref, b_ref, _ = dirs[d]
        cols = slice(g * W, (g + 1) * W)
        for ci in order[d]:
            rows = slice(ci * L, (ci + 1) * L)
            gens.append(_chunk_operators(*(ref[rows, cols].astype(F32)
                                           for ref in (r_ref, v_ref, kk_ref, lw_ref, k_ref, b_ref)),
                                         consts[d], reverse=d == 1))
    ops = _run_in_lockstep(gens)
    states = [s_ref[d, g] for d, g in streams]
    for j in range(n_chunks):
        for si, (d, g) in enumerate(streams):
            qs, yc, cu, decay, kb, vb = ops[si * n_chunks + j]
            ci = order[d][j]
            yu = lax.dot_general(qs, states[si].astype(BF16), _NT, preferred_element_type=F32)
            y_ref = dirs[d][6]
            y_ref[ci * L:(ci + 1) * L, g * W:(g + 1) * W] = (yu[:L] + yc).astype(y_ref.dtype)
            vu = jnp.concatenate([vb, (-(yu[L:] + cu)).astype(BF16)], axis=0)
            upd = lax.dot_general(vu, kb, _TN, preferred_element_type=F32)
            states[si] = states[si] * decay + jnp.where(consts[d][4], upd, 0.0)
    for si, (d, g) in enumerate(streams):
        s_ref[d, g] = states[si]


def _rwkv_scan(r, v, kk, lw, k, b, segs, tb=256, ng=4):
    t, dr = r.shape
    tb = min(tb, t)
    nblk = t // tb
    w = ng * SCAN_W
    fwd = pl.BlockSpec((tb, w), lambda g, c: (c, g))
    bwd = pl.BlockSpec((tb, w), lambda g, c: (nblk - 1 - c, g))
    fwd2 = pl.BlockSpec((None, tb, w), lambda g, c: (0, c, g))
    bwd2 = pl.BlockSpec((None, tb, w), lambda g, c: (1, nblk - 1 - c, g))
    out = jax.ShapeDtypeStruct((t, dr), BF16)
    return pl.pallas_call(
        functools.partial(_scan_kernel, tb=tb, segs=segs, nblk=nblk, ng=ng),
        grid=(dr // w, nblk),
        in_specs=[fwd, fwd, fwd, fwd2, fwd2, fwd2, bwd, bwd, bwd, bwd2, bwd2, bwd2],
        out_specs=[fwd, bwd],
        out_shape=[out, out],
        scratch_shapes=[pltpu.VMEM((2, ng, SCAN_W, SCAN_W), F32)],
        compiler_params=_params("parallel", "arbitrary"),
        name="rwkv_scan",
    )(r, v, kk, lw, k, b, r, v, kk, lw, k, b)


def _post_kernel(yf_ref, yb_ref, g_ref, bonus_ref, gng_ref, gnb_ref, ones_ref, o_ref):
    y = yf_ref[...].astype(F32) + yb_ref[...].astype(F32)
    ones_bd = ones_ref[...]
    mu = _head_sum(y, ones_bd) * (1.0 / RWKV_HEAD)
    cen = y - mu
    var = _head_sum(cen * cen, ones_bd) * (1.0 / RWKV_HEAD)
    yn = cen * lax.rsqrt(var + GN_EPS) * gng_ref[...] + gnb_ref[...]
    o_ref[...] = ((yn + bonus_ref[...].astype(F32)) * g_ref[...].astype(F32)).astype(o_ref.dtype)


def _rwkv_post(y_f, y_b, g, bonus, gn_g, gn_b, ones_bd, tb=256):
    t, dr = y_f.shape
    tb = min(tb, t)
    tok = pl.BlockSpec((tb, dr), lambda i: (i, 0))
    vec = pl.BlockSpec((1, dr), lambda i: (0, 0))
    return pl.pallas_call(
        _post_kernel,
        grid=(t // tb,),
        in_specs=[tok, tok, tok, tok, vec, vec, pl.BlockSpec((SCAN_W, SCAN_W), lambda i: (0, 0))],
        out_specs=tok,
        out_shape=jax.ShapeDtypeStruct((t, dr), BF16),
        compiler_params=_params("parallel"),
        name="rwkv_post",
    )(y_f, y_b, g, bonus, gn_g.reshape(1, dr), gn_b.reshape(1, dr), ones_bd)


def _residual_layernorm(x, m, g, b, alpha):
    h = alpha * x + m.astype(F32)
    mu = jnp.mean(h, axis=-1, keepdims=True)
    cen = h - mu
    var = jnp.mean(cen * cen, axis=-1, keepdims=True)
    return cen * lax.rsqrt(var + LN_EPS) * g + b


def _ln_kernel(*refs, alpha, n_first):
    *x_refs, m_ref, g_ref, b_ref, o_ref, ob_ref = refs
    if n_first is None:
        x = x_refs[0][...]
    else:
        x = jnp.where(pl.program_id(0) < n_first, x_refs[0][...], x_refs[1][...])
    y = _residual_layernorm(x, m_ref[...], g_ref[...], b_ref[...], alpha)
    o_ref[...] = y
    ob_ref[...] = y.astype(BF16)


def _residual_ln(x, mix, g, b, alpha, tb=256):
    t, dm = mix.shape
    tb = min(tb, t)
    tok = pl.BlockSpec((tb, dm), lambda i: (i, 0))
    vec = pl.BlockSpec((1, dm), lambda i: (0, 0))
    if isinstance(x, tuple):
        xa, xb = x
        n_first = xa.shape[0] // tb
        assert xa.shape[0] % tb == 0 and xb.shape[0] % tb == 0
        n_last = xb.shape[0] // tb - 1
        x_args = [xa, xb]
        x_specs = [pl.BlockSpec((tb, dm), lambda i: (jnp.minimum(i, n_first - 1), 0)),
                   pl.BlockSpec((tb, dm), lambda i: (jnp.clip(i - n_first, 0, n_last), 0))]
    else:
        n_first, x_args, x_specs = None, [x], [tok]
    return pl.pallas_call(
        functools.partial(_ln_kernel, alpha=alpha, n_first=n_first),
        grid=(t // tb,),
        in_specs=x_specs + [tok, vec, vec],
        out_specs=[tok, tok],
        out_shape=[jax.ShapeDtypeStruct((t, dm), F32), jax.ShapeDtypeStruct((t, dm), BF16)],
        compiler_params=_params("parallel"),
        name="residual_ln",
    )(*x_args, mix, g.reshape(1, dm), b.reshape(1, dm))


def _ffn_up_kernel(x_ref, wg_ref, wu_ref, h_ref):
    x = x_ref[...]
    a = jnp.dot(x, wg_ref[...], preferred_element_type=F32)
    u = jnp.dot(x, wu_ref[...], preferred_element_type=F32)
    h_ref[...] = (a * jax.nn.sigmoid(a) * u).astype(h_ref.dtype)


def _ffn_up(xb, wg, wu, tm=1024, tn=512):
    t, dm = xb.shape
    f = wg.shape[1]
    tm = min(tm, t)
    wspec = pl.BlockSpec((dm, tn), lambda i, j: (0, j))
    return pl.pallas_call(
        _ffn_up_kernel,
        grid=(t // tm, f // tn),
        in_specs=[pl.BlockSpec((tm, dm), lambda i, j: (i, 0)), wspec, wspec],
        out_specs=pl.BlockSpec((tm, tn), lambda i, j: (i, j)),
        out_shape=jax.ShapeDtypeStruct((t, f), BF16),
        compiler_params=_params("parallel", "parallel"),
        name="ffn_up",
    )(xb, wg, wu)


def _top2(x, w):
    logits = jnp.dot(x, w, precision=HIGHEST, preferred_element_type=F32)
    lane = lax.broadcasted_iota(jnp.int32, logits.shape, 1)
    neg = jnp.float32(-jnp.inf)
    logits = jnp.where(lane < N_EXPERTS, logits, neg)
    m1 = jnp.max(logits, axis=1, keepdims=True)
    i1 = jnp.min(jnp.where(logits == m1, lane, LANES), axis=1, keepdims=True)
    rest = jnp.where(lane == i1, neg, logits)
    m2 = jnp.max(rest, axis=1, keepdims=True)
    i2 = jnp.min(jnp.where(rest == m2, lane, LANES), axis=1, keepdims=True)
    e2 = jnp.exp(m2 - m1)
    w1 = 1.0 / (1.0 + e2)
    w2 = e2 / (1.0 + e2)
    idx = jnp.where(lane == 0, i1, jnp.where(lane == 1, i2, 0))
    wts = jnp.where(lane == 0, w1, jnp.where(lane == 1, w2, 0.0))
    return idx, wts


def _ln_router_kernel(x_ref, m_ref, g_ref, b_ref, rw_ref, o_ref, ob_ref, idx_ref, wt_ref, *, alpha):
    y = _residual_layernorm(x_ref[...], m_ref[...], g_ref[...], b_ref[...], alpha)
    o_ref[...] = y
    ob_ref[...] = y.astype(BF16)
    idx_ref[...], wt_ref[...] = _top2(y, rw_ref[...])


def _residual_ln_router(x, mix, g, b, alpha, router_w, tb=256):
    t, dm = x.shape
    tb = min(tb, t)
    w = jnp.pad(router_w, ((0, 0), (0, LANES - router_w.shape[1])))
    tok = pl.BlockSpec((tb, dm), lambda i: (i, 0))
    vec = pl.BlockSpec((1, dm), lambda i: (0, 0))
    route = pl.BlockSpec((tb, LANES), lambda i: (i, 0))
    return pl.pallas_call(
        functools.partial(_ln_router_kernel, alpha=alpha),
        grid=(t // tb,),
        in_specs=[tok, tok, vec, vec, pl.BlockSpec((dm, LANES), lambda i: (0, 0))],
        out_specs=[tok, tok, route, route],
        out_shape=[jax.ShapeDtypeStruct((t, dm), F32), jax.ShapeDtypeStruct((t, dm), BF16),
                   jax.ShapeDtypeStruct((t, LANES), jnp.int32), jax.ShapeDtypeStruct((t, LANES), F32)],
        compiler_params=_params("parallel"),
        name="residual_ln_router",
    )(x, mix, g.reshape(1, dm), b.reshape(1, dm), w)


MOE_TM = 512
GATHER_UNROLL = 8


def _moe_dispatch(idx, wts, tm):
    t = idx.shape[0]
    n_rows = 2 * t + N_EXPERTS * tm
    n_tiles = n_rows // tm
    e_ids = idx[:, :2]
    per_tok = (e_ids[:, :, None] == jnp.arange(N_EXPERTS)).astype(jnp.int32).sum(axis=1)
    csum = jnp.cumsum(per_tok, axis=0)
    counts = csum[-1]
    rank = csum - per_tok
    padded = (counts + tm - 1) // tm * tm
    ends = jnp.cumsum(padded)
    offs = ends - padded
    pos = offs[e_ids] + jnp.take_along_axis(rank, e_ids, axis=1)
    code = jnp.full((n_rows,), -1, jnp.int32).at[pos.reshape(-1)].set(jnp.arange(2 * t, dtype=jnp.int32))
    src = jnp.maximum(code, 0) // 2
    row_w = jnp.where(code >= 0, wts[:, :2].reshape(-1)[jnp.maximum(code, 0)], 0.0)
    tile_start = jnp.arange(n_tiles, dtype=jnp.int32) * tm
    owner = jnp.sum(tile_start[:, None] >= ends[None, :], axis=1).astype(jnp.int32)
    tile_valid = (owner < N_EXPERTS).astype(jnp.int32)
    tile_expert = jnp.minimum(owner, N_EXPERTS - 1)
    return src, row_w, pos.astype(jnp.int32), tile_expert, tile_valid


def _start_row_gather(idx_ref, n, src_hbm, dst_ref, sem):
    def issue(r, carry):
        pltpu.make_async_copy(src_hbm.at[pl.ds(idx_ref[0, r], 1), :], dst_ref.at[pl.ds(r, 1), :], sem).start()
        return carry

    lax.fori_loop(0, n, issue, 0, unroll=GATHER_UNROLL)


def _wait_row_gather(n, src_hbm, dst_ref, sem):
    pltpu.make_async_copy(src_hbm.at[pl.ds(0, n), :], dst_ref.at[pl.ds(0, n), :], sem).wait()


def _gather_rows(idx_ref, n, src_hbm, dst_ref, sem):
    _start_row_gather(idx_ref, n, src_hbm, dst_ref, sem)
    _wait_row_gather(n, src_hbm, dst_ref, sem)


def _moe_up_sparse_kernel(te_ref, tv_ref, src_ref, nxt_ref, x_hbm, wg_ref, wu_ref, roww_ref, h_ref, xbuf, xb_buf,
                          sem, *, tm, n_tiles):
    i = pl.program_id(0)
    valid = tv_ref[i] == 1
    first_col = pl.program_id(1) == 0

    @pl.when(jnp.logical_and(first_col, jnp.logical_and(valid, i == 0)))
    def _():
        _start_row_gather(src_ref, tm, x_hbm, xbuf, sem)

    @pl.when(jnp.logical_and(first_col, valid))
    def _():
        _wait_row_gather(tm, x_hbm, xbuf, sem)
        xb_buf[...] = xbuf[...].astype(BF16)

    nxt = jnp.minimum(i + 1, n_tiles - 1)

    @pl.when(jnp.logical_and(first_col, jnp.logical_and(i + 1 < n_tiles, tv_ref[nxt] == 1)))
    def _():
        _start_row_gather(nxt_ref, tm, x_hbm, xbuf, sem)

    @pl.when(valid)
    def _():
        x = xb_buf[...]
        a = jnp.dot(x, wg_ref[...], preferred_element_type=F32)
        u = jnp.dot(x, wu_ref[...], preferred_element_type=F32)
        h_ref[...] = (a * jax.nn.sigmoid(a) * u * roww_ref[...]).astype(h_ref.dtype)

    @pl.when(jnp.logical_not(valid))
    def _():
        h_ref[...] = jnp.zeros_like(h_ref)


def _moe_up_sparse(x, we_gate, we_up, src, row_w, tile_expert, tile_valid, tm, tn=512):
    t, dm = x.shape
    n_e, _, fe = we_gate.shape
    n_rows = src.shape[0]
    n_tiles = n_rows // tm
    wspec = pl.BlockSpec((None, dm, tn), lambda i, j, te, tv: (te[i], 0, j))
    grid_spec = pltpu.PrefetchScalarGridSpec(
        num_scalar_prefetch=2,
        grid=(n_tiles, fe // tn),
        in_specs=[pl.BlockSpec((None, 1, tm), lambda i, j, te, tv: (i, 0, 0), memory_space=pltpu.SMEM),
                  pl.BlockSpec((None, 1, tm), lambda i, j, te, tv: (jnp.minimum(i + 1, n_tiles - 1), 0, 0),
                               memory_space=pltpu.SMEM),
                  pl.BlockSpec(memory_space=pl.ANY),
                  wspec, wspec,
                  pl.BlockSpec((tm, 1), lambda i, j, te, tv: (i, 0))],
        out_specs=pl.BlockSpec((tm, tn), lambda i, j, te, tv: (i, j)),
        scratch_shapes=[pltpu.VMEM((tm, dm), F32), pltpu.VMEM((tm, dm), BF16), pltpu.SemaphoreType.DMA(())],
    )
    return pl.pallas_call(
        functools.partial(_moe_up_sparse_kernel, tm=tm, n_tiles=n_tiles),
        grid_spec=grid_spec,
        out_shape=jax.ShapeDtypeStruct((n_rows, fe), BF16),
        compiler_params=_params("arbitrary", "arbitrary"),
        name="moe_up_sparse",
    )(tile_expert, tile_valid, src.reshape(n_tiles, 1, tm), src.reshape(n_tiles, 1, tm), x, we_gate, we_up,
      row_w.reshape(n_rows, 1))


def _moe_down_kernel(te_ref, tv_ref, h_ref, w_ref, y_ref):
    valid = tv_ref[pl.program_id(1)] == 1

    @pl.when(valid)
    def _():
        y_ref[...] = jnp.dot(h_ref[...], w_ref[...], preferred_element_type=F32)

    @pl.when(jnp.logical_not(valid))
    def _():
        y_ref[...] = jnp.zeros_like(y_ref)


def _moe_down_sparse(h, we_down, tile_expert, tile_valid, tm, tn=1024):
    n_rows, fe = h.shape
    dm = we_down.shape[2]
    grid_spec = pltpu.PrefetchScalarGridSpec(
        num_scalar_prefetch=2,
        grid=(dm // tn, n_rows // tm),
        in_specs=[pl.BlockSpec((tm, fe), lambda j, i, te, tv: (i, 0)),
                  pl.BlockSpec((None, fe, tn), lambda j, i, te, tv: (te[i], 0, j))],
        out_specs=pl.BlockSpec((tm, tn), lambda j, i, te, tv: (i, j)),
    )
    return pl.pallas_call(
        _moe_down_kernel,
        grid_spec=grid_spec,
        out_shape=jax.ShapeDtypeStruct((n_rows, dm), F32),
        compiler_params=_params("parallel", "parallel"),
        name="moe_down_sparse",
    )(tile_expert, tile_valid, h, we_down)


def _moe_combine_ln_kernel(pos_ref, nxt_ref, y_hbm, x_ref, g_ref, b_ref, o_ref, ob_ref, ybuf, sem, *, alpha, tb,
                           n_steps):
    i = pl.program_id(0)
    slot = i % 2

    @pl.when(i == 0)
    def _():
        _start_row_gather(pos_ref, 2 * tb, y_hbm, ybuf.at[0], sem.at[0])

    @pl.when(i + 1 < n_steps)
    def _():
        _start_row_gather(nxt_ref, 2 * tb, y_hbm, ybuf.at[1 - slot], sem.at[1 - slot])

    _wait_row_gather(2 * tb, y_hbm, ybuf.at[slot], sem.at[slot])
    rows = ybuf.at[slot]
    y = _residual_layernorm(x_ref[...], rows[0:tb, :] + rows[tb:2 * tb, :], g_ref[...], b_ref[...], alpha)
    o_ref[...] = y
    ob_ref[...] = y.astype(BF16)


def _moe_combine_ln(x, ys, pos, g, b, alpha, tb=256):
    t, dm = x.shape
    tb = min(tb, t)
    tok = pl.BlockSpec((tb, dm), lambda i: (i, 0))
    vec = pl.BlockSpec((1, dm), lambda i: (0, 0))
    n_steps = t // tb
    rows = jnp.concatenate([pos[:, 0].reshape(n_steps, 1, tb), pos[:, 1].reshape(n_steps, 1, tb)], axis=2)
    return pl.pallas_call(
        functools.partial(_moe_combine_ln_kernel, alpha=alpha, tb=tb, n_steps=n_steps),
        grid=(n_steps,),
        in_specs=[pl.BlockSpec((None, 1, 2 * tb), lambda i: (i, 0, 0), memory_space=pltpu.SMEM),
                  pl.BlockSpec((None, 1, 2 * tb), lambda i: (jnp.minimum(i + 1, n_steps - 1), 0, 0),
                               memory_space=pltpu.SMEM),
                  pl.BlockSpec(memory_space=pl.ANY), tok, vec, vec],
        out_specs=[tok, tok],
        out_shape=[jax.ShapeDtypeStruct((t, dm), F32), jax.ShapeDtypeStruct((t, dm), BF16)],
        scratch_shapes=[pltpu.VMEM((2, 2 * tb, dm), F32), pltpu.SemaphoreType.DMA((2,))],
        compiler_params=_params("arbitrary"),
        name="moe_combine_ln",
    )(rows, rows, ys, x, g.reshape(1, dm), b.reshape(1, dm))


def _ple_kernel(xb_ref, wg_ref, p_ref, wp_ref, x_ref, o_ref, *maybe_ob_ref):
    gate = jax.nn.sigmoid(jnp.dot(xb_ref[...], wg_ref[...], preferred_element_type=F32))
    emb = jnp.dot(p_ref[...], wp_ref[...], preferred_element_type=F32)
    y = x_ref[...] + gate * emb
    o_ref[...] = y
    for ob_ref in maybe_ob_ref:
        ob_ref[...] = y.astype(BF16)


def _ple(xb, x, w_pgate, p, w_pproj, rows=None, tm=1024, tn=512):
    t, dm = xb.shape
    pd = p.shape[1]
    start, count = (0, t) if rows is None else rows
    tm = min(tm, count)
    assert start % tm == 0 and count % tm == 0
    off = start // tm
    out = pl.BlockSpec((tm, tn), lambda i, j: (i, j))
    outs = [jax.ShapeDtypeStruct((count, dm), F32)] + ([jax.ShapeDtypeStruct((count, dm), BF16)] if rows is None else [])
    return pl.pallas_call(
        _ple_kernel,
        grid=(count // tm, dm // tn),
        in_specs=[pl.BlockSpec((tm, dm), lambda i, j: (i + off, 0)),
                  pl.BlockSpec((dm, tn), lambda i, j: (0, j)),
                  pl.BlockSpec((tm, pd), lambda i, j: (i + off, 0)),
                  pl.BlockSpec((pd, tn), lambda i, j: (0, j)),
                  pl.BlockSpec((tm, tn), lambda i, j: (i + off, j))],
        out_specs=[out] * len(outs),
        out_shape=outs,
        compiler_params=_params("parallel", "parallel"),
        name="ple_gate",
    )(xb, w_pgate, p, w_pproj, x)


def _pad_cols(w, width):
    return jnp.pad(w, ((0, 0), (0, width - w.shape[1])))


def _pad_rows(w, rows):
    pad = [(0, 0)] * w.ndim
    pad[-2] = (0, rows - w.shape[-2])
    return jnp.pad(w, pad)


def _split_rwkv_cols(w, dg, dr, has_mv):
    base = 2 * dg
    rkv = w[:, base:base + 3 * dr]
    o = base + 3 * dr
    sizes = [DECAY_LORA, DECAY_LORA, AAA_LORA, AAA_LORA, GATE_LORA] + ([MV_LORA] if has_mv else [])
    slots = [LORA_SLOT, LORA_SLOT, LORA_SLOT, LORA_SLOT, GATE_LORA, LORA_SLOT]
    parts = []
    for n, slot in zip(sizes, slots):
        parts.append(_pad_cols(w[:, o:o + n], slot))
        o += n
    lora = _pad_cols(jnp.concatenate(parts, axis=1), LORA_COLS)
    return jnp.concatenate([rkv, lora], axis=1)


def _trunk(x, p, P, segs):
    (t1, _), (t2, _) = segs
    t = t1 + t2
    depth = P['w_out'].shape[0]
    alpha = (2 * depth) ** 0.25
    dg = P['sgu_ln_g'].shape[1]
    dr = P['k_k'].shape[1]
    hb = lax.broadcasted_iota(jnp.int32, (SCAN_W, SCAN_W), 0) // RWKV_HEAD
    hc = lax.broadcasted_iota(jnp.int32, (SCAN_W, SCAN_W), 1) // RWKV_HEAD
    ones_bd = (hb == hc).astype(BF16)
    xb = jnp.concatenate([part.astype(BF16) for part in x], axis=0)
    w_out_b = _to_bf16(P['w_out'])
    w_pgate_b = _to_bf16(P['w_pgate'])
    v_first = None
    for l in range(depth):
        if l == 0:
            w_in_l, conv_l, v0_l, v2_l = P['w_in0'], P['conv0'], None, None
        else:
            w_in_l, conv_l, v0_l, v2_l = P['w_in'][l - 1], P['conv'][l - 1], P['v0'][l - 1], P['v2'][l - 1]
        w_in_b = _to_bf16(w_in_l)
        uv = _matmul(xb, w_in_b[:, :2 * dg], BF16)
        zr = _matmul(xb, _split_rwkv_cols(w_in_b, dg, dr, l > 0), BF16)
        y_g = _spatial_gating(uv, P['sgu_ln_g'][l], P['sgu_ln_b'][l], P['w_s'][l], P['b_s'][l])
        conv_r = _split_rwkv_cols(jnp.pad(conv_l, ((0, 0), (2 * dg, 0))), dg, dr, l > 0)
        prep_args = (zr, conv_r, P['w0'][l], _pad_rows(P['w2'][l], LORA_SLOT).astype(BF16), P['a0'][l],
                     _pad_rows(P['a2'][l], LORA_SLOT).astype(BF16), P['g2'][l].astype(BF16),
                     P['k_k'][l], P['k_a'][l], P['r_k'][l].reshape(-1), ones_bd, segs)
        if l == 0:
            r, v, kk, g, bonus, lw, k, b = _rwkv_prep(*prep_args)
            v_first = v
        else:
            r, v, kk, g, bonus, lw, k, b = _rwkv_prep(*prep_args, v_first=v_first, v0=v0_l,
                                                      v2=_pad_rows(v2_l, LORA_SLOT).astype(BF16))
        y_f, y_b = _rwkv_scan(r, v, kk, lw, k, b, segs)
        y_r = _rwkv_post(y_f, y_b, g, bonus, P['gn_g'][l], P['gn_b'][l], ones_bd)
        w_out = w_out_b[l]
        mix = _matmul_pair(y_g, w_out[:dg], y_r, w_out[dg:], BF16)
        j = l // 2
        if l % 2 == 0:
            x, xb = _residual_ln(x, mix, P['ln1_g'][l], P['ln1_b'][l], alpha)
            h = _ffn_up(xb, _to_bf16(P['w_ff_gate'][j]), _to_bf16(P['w_ff_up'][j]))
            ff = _matmul(h, _to_bf16(P['w_ff_down'][j]), BF16, tk=h.shape[1] // 4)
        else:
            if isinstance(x, tuple):
                x = jnp.concatenate(x, axis=0)
            x, xb, idx, wts = _residual_ln_router(x, mix, P['ln1_g'][l], P['ln1_b'][l], alpha, P['router'][j])
            tm = min(MOE_TM, t)
            src, row_w, pos, tile_expert, tile_valid = _moe_dispatch(idx, wts, tm)
            h = _moe_up_sparse(x, _to_bf16(P['we_gate'][j]), _to_bf16(P['we_up'][j]), src, row_w,
                               tile_expert, tile_valid, tm)
            ys = _moe_down_sparse(h, _to_bf16(P['we_down'][j]), tile_expert, tile_valid, tm)
            x, xb = _moe_combine_ln(x, ys, pos, P['ln2_g'][l], P['ln2_b'][l], alpha)
        if l % 2 == 0:
            x, xb = _residual_ln(x, ff, P['ln2_g'][l], P['ln2_b'][l], alpha)
        ple_args = (xb, x, w_pgate_b[l], p[l].astype(BF16), P['w_pproj'][l].astype(BF16))
        if l < depth - 1:
            x, xb = _ple(*ple_args)
    (y1,), (y2,) = _ple(*ple_args, rows=(0, t1)), _ple(*ple_args, rows=(t1, t2))
    return y1, y2


def kernel(x_prompt, x_sample, p_prompt, p_sample, w_in0, conv0, w_in, conv, sgu_ln_g, sgu_ln_b, w_s, b_s, w0, w2, a0, a2, g2, k_k, k_a, r_k, gn_g, gn_b, v0, v2, w_out, ln1_g, ln1_b, ln2_g, ln2_b, w_ff_gate, w_ff_up, w_ff_down, router, we_gate, we_up, we_down, w_pproj, w_pgate):
    P = dict(w_in0=w_in0, conv0=conv0, w_in=w_in, conv=conv, sgu_ln_g=sgu_ln_g, sgu_ln_b=sgu_ln_b,
             w_s=w_s, b_s=b_s, w0=w0, w2=w2, a0=a0, a2=a2, g2=g2, k_k=k_k, k_a=k_a, r_k=r_k,
             gn_g=gn_g, gn_b=gn_b, v0=v0, v2=v2, w_out=w_out, ln1_g=ln1_g, ln1_b=ln1_b,
             ln2_g=ln2_g, ln2_b=ln2_b, w_ff_gate=w_ff_gate, w_ff_up=w_ff_up, w_ff_down=w_ff_down,
             router=router, we_gate=we_gate, we_up=we_up, we_down=we_down,
             w_pproj=w_pproj, w_pgate=w_pgate)
    b1, s1, dm = x_prompt.shape
    b2, s2, _ = x_sample.shape
    depth, pd = p_prompt.shape[0], p_prompt.shape[-1]
    t1, t2 = b1 * s1, b2 * s2
    segs = ((t1, s1), (t2, s2))
    p = jnp.concatenate([p_prompt.reshape(depth, t1, pd), p_sample.reshape(depth, t2, pd)], axis=1)
    y1, y2 = _trunk((x_prompt.reshape(t1, dm), x_sample.reshape(t2, dm)), p, P, segs)
    return y1.reshape(b1, s1, dm), y2.reshape(b2, s2, dm)
```
